```python
import jax, jax.numpy as jnp
from jax import lax
import numpy as np

D_MODEL = 2048
BATCH = 32
SEQ = 256
DEPTH = 4
DEC_BATCH = 2
DEC_SEQ = 4096
PAST_LEN = 512

GRID_W = 64
ROPE_THETA = 10000.0
EPS = 1e-6
Q_BLOCK = 128

GQA_HEADS = 6
GQA_KV_HEADS = 2
GQA_HEAD_DIM = 128
GQA_GROUP = GQA_HEADS // GQA_KV_HEADS

HG_HEADS = 4
HG_KEY_DIM = 128
HG_VAL_DIM = 128
HG_CHUNK = 32
HG_WIDTH = HG_HEADS * HG_KEY_DIM

MLA_HEADS = 6
MLA_Q_RANK = 512
MLA_KV_RANK = 512
MLA_NOPE = 128
MLA_ROPE = 64
MLA_V = 128
MLA_QK = MLA_NOPE + MLA_ROPE

PEER_HEADS = 8
PEER_KEYS = 128
PEER_EXPERTS = PEER_KEYS * PEER_KEYS
PEER_QDIM = 256
PEER_TOPK = 16
PEER_TOKEN_BLOCK = 128

IN_SIZES = (GQA_HEADS * GQA_HEAD_DIM, GQA_KV_HEADS * GQA_HEAD_DIM, GQA_KV_HEADS * GQA_HEAD_DIM,
            HG_WIDTH, HG_WIDTH, HG_WIDTH, HG_HEADS * HG_VAL_DIM, HG_HEADS * HG_VAL_DIM,
            MLA_Q_RANK, MLA_KV_RANK, MLA_ROPE)
IN_WIDTH = sum(IN_SIZES)
MIX_WIDTH = GQA_HEADS * GQA_HEAD_DIM + HG_HEADS * HG_VAL_DIM + MLA_HEADS * MLA_V

kernel_name = 'hymba_dit_gqa_hgrn2_mla_peer_step'


def rmsnorm(x, g):
    xf = x.astype(jnp.float32)
    y = xf * lax.rsqrt(jnp.mean(xf * xf, axis=-1, keepdims=True) + EPS)
    return (y * g.astype(jnp.float32)).astype(x.dtype)


def split_cols(a, sizes):
    return jnp.split(a, np.cumsum(sizes)[:-1].tolist(), axis=-1)


def grid_positions(n):
    rows = n // GRID_W
    row = jnp.repeat(jnp.arange(rows, dtype=jnp.int32), GRID_W)
    col = jnp.tile(jnp.arange(GRID_W, dtype=jnp.int32), rows)
    return row, col


def axis_freqs(pos, axis_dim):
    nfreq = axis_dim // 2
    inv = ROPE_THETA ** (-jnp.arange(nfreq, dtype=jnp.float32) / nfreq)
    ang = pos.astype(jnp.float32)[:, None] * inv[None, :]
    return (jnp.cos(ang), jnp.sin(ang))


def rope_tables(row, col, rot_dim):
    return axis_freqs(row, rot_dim // 2) + axis_freqs(col, rot_dim // 2)


def rope_1d(x, cos, sin):
    x1, x2 = jnp.split(x.astype(jnp.float32), 2, axis=-1)
    c = cos[None, :, None, :]
    s = sin[None, :, None, :]
    return jnp.concatenate([x1 * c - x2 * s, x2 * c + x1 * s], axis=-1).astype(x.dtype)


def rope_2d(x, tabs):
    cr, sr, cc, sc = tabs
    r = x.shape[-1] // 2
    return jnp.concatenate([rope_1d(x[..., :r], cr, sr), rope_1d(x[..., r:], cc, sc)], axis=-1)


def block_attention(q, k, v):
    b, sq, hkv, g, dh = q.shape
    nb = sq // Q_BLOCK
    scale = dh ** -0.5
    qb = jnp.moveaxis(q.reshape(b, nb, Q_BLOCK, hkv, g, dh), 1, 0)

    def one_block(qblk):
        s = jnp.einsum('bqhgd,bkhd->bhgqk', qblk, k, preferred_element_type=jnp.float32) * scale
        p = jax.nn.softmax(s, axis=-1)
        return jnp.einsum('bhgqk,bkhe->bqhge', p.astype(v.dtype), v)

    out = lax.map(one_block, qb)
    return jnp.moveaxis(out, 0, 1).reshape(b, sq, hkv, g, v.shape[-1])


def hgrn_scan(q, k, v, logf, s0):
    b, n, h, _ = q.shape
    nc = n // HG_CHUNK

    def chunks(a):
        return jnp.moveaxis(a.astype(jnp.float32).reshape(b, nc, HG_CHUNK, h, a.shape[-1]), 1, 0)

    causal = jnp.tril(jnp.ones((HG_CHUNK, HG_CHUNK), dtype=bool))[None, :, :, None, None]

    def step(S, inp):
        qc, kc, vc, lc = inp
        bcum = jnp.cumsum(lc, axis=1)
        o_inter = jnp.einsum('bchk,bhkv->bchv', qc * jnp.exp(bcum), S)
        diff = bcum[:, :, None] - bcum[:, None, :]
        decay = jnp.where(causal, jnp.exp(jnp.minimum(diff, 0.0)), 0.0)
        A = jnp.einsum('bthk,btshk,bshk->bhts', qc, decay, kc)
        o_intra = jnp.einsum('bhts,bshv->bthv', A, vc)
        blast = bcum[:, -1]
        S_new = jnp.exp(blast)[..., None] * S + jnp.einsum('bshk,bshv->bhkv', kc * jnp.exp(blast[:, None] - bcum), vc)
        return S_new, o_inter + o_intra

    S, o = lax.scan(step, s0.astype(jnp.float32), (chunks(q), chunks(k), chunks(v), chunks(logf)))
    o = jnp.moveaxis(o, 0, 1).reshape(b, n, h, v.shape[-1])
    return o, S


def mla_expand(ckv, krope, w_kvb, kn_g, tabs):
    b, n, _ = ckv.shape
    kv = (ckv @ w_kvb).reshape(b, n, MLA_HEADS, MLA_NOPE + MLA_V)
    k_nope, v = kv[..., :MLA_NOPE], kv[..., MLA_NOPE:]
    k = jnp.concatenate([k_nope, jnp.broadcast_to(krope[:, :, None, :], (b, n, MLA_HEADS, MLA_ROPE)).astype(k_nope.dtype)], axis=-1)
    k = rmsnorm(k, kn_g)
    if tabs is not None:
        k = jnp.concatenate([k[..., :MLA_NOPE], rope_2d(k[..., MLA_NOPE:], tabs)], axis=-1)
    return k, v


def token_mixers(h, lp, lb, ctx, rope):
    b, n, _ = h.shape
    z = h @ lp['w_in']
    gq, gk, gv, hq, hff, hfb, hi, hg, mqa, mkva, mkr = split_cols(z, IN_SIZES)
    gq = rmsnorm(gq.reshape(b, n, GQA_HEADS, GQA_HEAD_DIM), lp['gqa_qn'])
    gk = rmsnorm(gk.reshape(b, n, GQA_KV_HEADS, GQA_HEAD_DIM), lp['gqa_kn'])
    gv = gv.reshape(b, n, GQA_KV_HEADS, GQA_HEAD_DIM)
    mq = (rmsnorm(mqa, lp['mla_qa_n']) @ lp['w_mla_qb']).reshape(b, n, MLA_HEADS, MLA_QK)
    mq = rmsnorm(mq, lp['mla_qn'])
    ckv = rmsnorm(mkva, lp['mla_kva_n'])
    rope_g = None if rope is None else rope[0]
    rope_m = None if rope is None else rope[1]
    mk, mv = mla_expand(ckv, mkr, lp['w_mla_kvb'], lp['mla_kn'], rope_m)
    if rope is None:
        gq_r, gk_r, mq_r = gq, gk, mq
    else:
        gq_r = rope_2d(gq, rope_g)
        gk_r = rope_2d(gk, rope_g)
        mq_r = jnp.concatenate([mq[..., :MLA_NOPE], rope_2d(mq[..., MLA_NOPE:], rope_m)], axis=-1)
    if ctx is None:
        k_all, v_all, mk_all, mv_all = gk_r, gv, mk, mv
        s0 = jnp.zeros((b, 2, HG_HEADS, HG_KEY_DIM, HG_VAL_DIM), jnp.float32)
    else:
        ck, cv, cckv, ckr, cstate = ctx
        cmk, cmv = mla_expand(cckv, ckr, lp['w_mla_kvb'], lp['mla_kn'], None)
        k_all = jnp.concatenate([ck.astype(gk_r.dtype), gk_r], axis=1)
        v_all = jnp.concatenate([cv.astype(gv.dtype), gv], axis=1)
        mk_all = jnp.concatenate([cmk.astype(mk.dtype), mk], axis=1)
        mv_all = jnp.concatenate([cmv.astype(mv.dtype), mv], axis=1)
        s0 = cstate
    gqa_out = block_attention(gq_r.reshape(b, n, GQA_KV_HEADS, GQA_GROUP, GQA_HEAD_DIM), k_all, v_all)
    gqa_out = gqa_out.reshape(b, n, GQA_HEADS * GQA_HEAD_DIM)
    mla_out = block_attention(mq_r[:, :, :, None, :], mk_all, mv_all).reshape(b, n, MLA_HEADS * MLA_V)
    hq = hq.reshape(b, n, HG_HEADS, HG_KEY_DIM)
    hi = hi.reshape(b, n, HG_HEADS, HG_VAL_DIM)

    def forget(zf, lb_d):
        zf = zf.astype(jnp.float32).reshape(b, n, HG_HEADS, HG_KEY_DIM)
        lbd = lb_d.reshape(HG_HEADS, HG_KEY_DIM)
        logf = jnp.logaddexp(jnp.log(lbd), jnp.log1p(-lbd) + jax.nn.log_sigmoid(zf))
        return logf, -jnp.expm1(logf)

    logf_f, k_f = forget(hff, lb[0])
    logf_b, k_b = forget(hfb, lb[1])
    flip = lambda a: jnp.flip(a, axis=1)
    o_f, s_f = hgrn_scan(hq, k_f, hi, logf_f, s0[:, 0])
    o_b, s_b = hgrn_scan(flip(hq), flip(k_b), flip(hi), flip(logf_b), s0[:, 1])
    o = (o_f + flip(o_b)).astype(h.dtype)
    o = rmsnorm(o, lp['hg_on']) * jax.nn.silu(hg.reshape(b, n, HG_HEADS, HG_VAL_DIM))
    hg_out = o.reshape(b, n, HG_HEADS * HG_VAL_DIM)
    out = jnp.concatenate([gqa_out, hg_out, mla_out], axis=-1) @ lp['w_out']
    if ctx is None:
        ctx_out = (gk, gv, ckv, mkr, jnp.stack([s_f, s_b], axis=1).astype(h.dtype))
    else:
        ctx_out = None
    return out, ctx_out


def peer_ffn(h, wq, keys, u, v):
    b, n, d = h.shape
    nt = b * n
    t = h.reshape(nt, d)
    q = (t @ wq).reshape(nt, PEER_HEADS, 2, PEER_QDIM // 2)
    s = jnp.einsum('thpc,hpkc->thpk', q, keys, preferred_element_type=jnp.float32)
    sv, si = lax.top_k(s, PEER_TOPK)
    cand_s = (sv[:, :, 0, :, None] + sv[:, :, 1, None, :]).reshape(nt, PEER_HEADS, PEER_TOPK * PEER_TOPK)
    cand_i = (si[:, :, 0, :, None] * PEER_KEYS + si[:, :, 1, None, :]).reshape(nt, PEER_HEADS, PEER_TOPK * PEER_TOPK)
    top_s, top_p = lax.top_k(cand_s, PEER_TOPK)
    eidx = jnp.take_along_axis(cand_i, top_p, axis=-1)
    gate = jax.nn.softmax(top_s, axis=-1).astype(h.dtype)
    nb = nt // PEER_TOKEN_BLOCK
    e_tok = PEER_HEADS * PEER_TOPK

    def expert_block(args):
        tb, ib, gb = args
        act = jax.nn.gelu(jnp.einsum('td,ted->te', tb, u[ib]), approximate=False)
        return jnp.einsum('te,ted->td', gb * act, v[ib])

    out = lax.map(expert_block, (t.reshape(nb, PEER_TOKEN_BLOCK, d),
                                 eidx.reshape(nb, PEER_TOKEN_BLOCK, e_tok),
                                 gate.reshape(nb, PEER_TOKEN_BLOCK, e_tok)))
    return out.reshape(b, n, d)


def trunk_layer(x, cond, lp, lb, ctx, rope):
    mod = jax.nn.silu(cond) @ lp['w_mod'] + lp['b_mod']
    sh1, sc1, g1, sh2, sc2, g2 = jnp.split(mod[:, None, :], 6, axis=-1)
    h = rmsnorm(x, lp['norm1_g']) * (1 + sc1) + sh1
    mix, ctx_out = token_mixers(h, lp, lb, ctx, rope)
    x = x + g1 * mix
    h = rmsnorm(x, lp['norm2_g']) * (1 + sc2) + sh2
    x = x + g2 * peer_ffn(h, lp['peer_wq'], lp['peer_keys'], lp['peer_u'], lp['peer_v'])
    return x, ctx_out


def setup_inputs(seed: int = 0) -> dict:
    key = jax.random.key(seed)
    ks = jax.random.split(key, 32)
    nrm = lambda k, shape, s: jax.random.normal(k, shape, jnp.float32) * s
    gain = lambda k, shape: 1.0 + 0.02 * jax.random.normal(k, shape, jnp.float32)
    D = D_MODEL
    return {
        'x_prompt': nrm(ks[0], (BATCH, SEQ, D), 1.0),
        'x_sample': nrm(ks[1], (DEC_BATCH, DEC_SEQ, D), 1.0),
        'c': nrm(ks[2], (DEC_BATCH, D), 1.0),
        'cache_gqa_k': nrm(ks[3], (DEC_BATCH, DEPTH, PAST_LEN, GQA_KV_HEADS, GQA_HEAD_DIM), 1.0),
        'cache_gqa_v': nrm(ks[4], (DEC_BATCH, DEPTH, PAST_LEN, GQA_KV_HEADS, GQA_HEAD_DIM), 1.0),
        'cache_mla_ckv': nrm(ks[5], (DEC_BATCH, DEPTH, PAST_LEN, MLA_KV_RANK), 1.0),
        'cache_mla_krope': nrm(ks[6], (DEC_BATCH, DEPTH, PAST_LEN, MLA_ROPE), 1.0),
        'state_hgrn': nrm(ks[7], (DEC_BATCH, DEPTH, 2, HG_HEADS, HG_KEY_DIM, HG_VAL_DIM), 0.3),
        'c_ctx': nrm(ks[8], (D,), 1.0),
        'w_mod': nrm(ks[9], (DEPTH, D, 6 * D), 0.5 * D ** -0.5),
        'b_mod': nrm(ks[10], (DEPTH, 6 * D), 0.02),
        'norm1_g': gain(ks[11], (DEPTH, D)),
        'norm2_g': gain(ks[12], (DEPTH, D)),
        'w_in': nrm(ks[13], (DEPTH, D, IN_WIDTH), D ** -0.5),
        'gqa_qn': gain(ks[14], (DEPTH, GQA_HEAD_DIM)),
        'gqa_kn': gain(ks[15], (DEPTH, GQA_HEAD_DIM)),
        'mla_qa_n': gain(ks[16], (DEPTH, MLA_Q_RANK)),
        'mla_kva_n': gain(ks[17], (DEPTH, MLA_KV_RANK)),
        'w_mla_qb': nrm(ks[18], (DEPTH, MLA_Q_RANK, MLA_HEADS * MLA_QK), MLA_Q_RANK ** -0.5),
        'w_mla_kvb': nrm(ks[19], (DEPTH, MLA_KV_RANK, MLA_HEADS * (MLA_NOPE + MLA_V)), MLA_KV_RANK ** -0.5),
        'mla_qn': gain(ks[20], (DEPTH, MLA_QK)),
        'mla_kn': gain(ks[21], (DEPTH, MLA_QK)),
        'hg_lb_logits': nrm(ks[22], (DEPTH, 2, HG_WIDTH), 0.5),
        'hg_on': gain(ks[23], (DEPTH, HG_VAL_DIM)),
        'w_out': nrm(ks[24], (DEPTH, MIX_WIDTH, D), MIX_WIDTH ** -0.5),
        'peer_wq': nrm(ks[25], (DEPTH, D, PEER_HEADS * PEER_QDIM), D ** -0.5),
        'peer_keys': nrm(ks[26], (DEPTH, PEER_HEADS, 2, PEER_KEYS, PEER_QDIM // 2), (PEER_QDIM // 2) ** -0.5),
        'peer_u': nrm(ks[27], (DEPTH, PEER_EXPERTS, D), D ** -0.5),
        'peer_v': nrm(ks[28], (DEPTH, PEER_EXPERTS, D), 0.5),
    }


def reference(x_prompt, x_sample, c, cache_gqa_k, cache_gqa_v, cache_mla_ckv, cache_mla_krope, state_hgrn, c_ctx,
              w_mod, b_mod, norm1_g, norm2_g, w_in, gqa_qn, gqa_kn, mla_qa_n, mla_kva_n, w_mla_qb, w_mla_kvb,
              mla_qn, mla_kn, hg_lb_logits, hg_on, w_out, peer_wq, peer_keys, peer_u, peer_v):
    lb_cum = jnp.cumsum(jax.nn.softmax(hg_lb_logits.astype(jnp.float32), axis=0), axis=0)
    lower_bounds = jnp.maximum(lb_cum - lb_cum[:1], 0.0)
    row, col = grid_positions(x_sample.shape[1])
    rope = (rope_tables(row, col, GQA_HEAD_DIM), rope_tables(row, col, MLA_ROPE))
    cond_ctx = c_ctx[None, :]
    xp, xs = x_prompt, x_sample
    new_k, new_v, new_ckv, new_kr, new_s = [], [], [], [], []
    for l in range(DEPTH):
        lp = {'w_mod': w_mod[l], 'b_mod': b_mod[l], 'norm1_g': norm1_g[l], 'norm2_g': norm2_g[l],
              'w_in': w_in[l], 'gqa_qn': gqa_qn[l], 'gqa_kn': gqa_kn[l], 'mla_qa_n': mla_qa_n[l],
              'mla_kva_n': mla_kva_n[l], 'w_mla_qb': w_mla_qb[l], 'w_mla_kvb': w_mla_kvb[l],
              'mla_qn': mla_qn[l], 'mla_kn': mla_kn[l], 'hg_on': hg_on[l], 'w_out': w_out[l],
              'peer_wq': peer_wq[l], 'peer_keys': peer_keys[l], 'peer_u': peer_u[l], 'peer_v': peer_v[l]}
        xp, ctx_out = trunk_layer(xp, cond_ctx, lp, lower_bounds[l], None, None)
        new_k.append(ctx_out[0])
        new_v.append(ctx_out[1])
        new_ckv.append(ctx_out[2])
        new_kr.append(ctx_out[3])
        new_s.append(ctx_out[4])
        ctx_in = (cache_gqa_k[:, l], cache_gqa_v[:, l], cache_mla_ckv[:, l], cache_mla_krope[:, l],
                  state_hgrn[:, l].astype(jnp.float32))
        xs, _ = trunk_layer(xs, c, lp, lower_bounds[l], ctx_in, rope)
    return (xp, xs, jnp.stack(new_k, axis=1), jnp.stack(new_v, axis=1), jnp.stack(new_ckv, axis=1),
            jnp.stack(new_kr, axis=1), jnp.stack(new_s, axis=1))
```

```python
import functools

import numpy as np
import jax
import jax.numpy as jnp
from jax import lax
from jax.experimental import pallas as pl
from jax.experimental.pallas import tpu as pltpu

F32 = jnp.float32
BF16 = jnp.bfloat16

EPS = 1e-6
ROPE_THETA = 10000.0
GRID_W = 64

HEAD_DIM = 128
GQA_HEADS, GQA_KV_HEADS = 6, 2
HG_HEADS = 4
MLA_HEADS, MLA_ROPE, MLA_RANK = 6, 64, 512
MLA_QK = HEAD_DIM + MLA_ROPE
MLA_PAD = 2 * HEAD_DIM
PEER_HEADS, PEER_KEYS, PEER_TOPK = 8, 128, 16
HG_CHUNK = 64

VMEM_LIMIT = 56 * 1024 * 1024


def _cparams(*sem):
    return pltpu.CompilerParams(dimension_semantics=sem, vmem_limit_bytes=VMEM_LIMIT)


def _dot(a, b):
    return jnp.dot(a, b, preferred_element_type=F32)


def _dot_nt(a, b):
    return lax.dot_general(a, b, (((1,), (1,)), ((), ())), preferred_element_type=F32)


def _dot_tn(a, b):
    return lax.dot_general(a, b, (((0,), (0,)), ((), ())), preferred_element_type=F32)


def _split2(a):
    hi = a.astype(BF16)
    lo = (a - hi.astype(F32)).astype(BF16)
    return hi, lo


def _split3(a):
    p1 = a.astype(BF16)
    r = a - p1.astype(F32)
    p2 = r.astype(BF16)
    p3 = (r - p2.astype(F32)).astype(BF16)
    return p1, p2, p3


def _rms(x, n):
    return x * lax.rsqrt(jnp.sum(x * x, axis=-1, keepdims=True) * (1.0 / n) + EPS)


def _mod_kernel(cond_ref, w_ref, b_ref, o_ref):
    c = cond_ref[...]
    s = c * (1.0 / (1.0 + jnp.exp(-c)))
    shi, slo = _split2(s)
    whi, wlo = _split2(w_ref[...])
    o_ref[...] = _dot(shi, whi) + _dot(slo, whi) + _dot(shi, wlo) + b_ref[...]


def _modulation(cond8, w_mod, b_mod):
    L, D, N = w_mod.shape
    tn = min(512, N)
    return pl.pallas_call(
        _mod_kernel,
        grid=(L, N // tn),
        in_specs=[pl.BlockSpec((8, D), lambda l, j: (0, 0)),
                  pl.BlockSpec((None, D, tn), lambda l, j: (l, 0, j)),
                  pl.BlockSpec((None, 1, tn), lambda l, j: (l, 0, j))],
        out_specs=pl.BlockSpec((None, 8, tn), lambda l, j: (l, 0, j)),
        out_shape=jax.ShapeDtypeStruct((L, 8, N), F32),
        compiler_params=_cparams("parallel", "parallel"),
        name="modulation",
    )(cond8, w_mod, b_mod.reshape(L, 1, N))


def _in_kernel(x_ref, g_ref, sh_ref, sc_ref, w_ref, o_ref, h_sc):
    @pl.when(pl.program_id(1) == 0)
    def _():
        x = x_ref[...]
        h = _rms(x, x.shape[-1]) * g_ref[...] * (1.0 + sc_ref[...]) + sh_ref[...]
        h_sc[...] = h.astype(BF16)

    o_ref[...] = _dot(h_sc[...], w_ref[...])


def _in_proj(x, g, modv, w, mod_row_of_tile, tm):
    T, D = x.shape
    N = w.shape[1]
    tn = min(1024, N)

    def mspec(k):
        return pl.BlockSpec((None, 1, D), lambda i, j: (mod_row_of_tile(i) + k, 0, 0))

    return pl.pallas_call(
        _in_kernel,
        grid=(T // tm, N // tn),
        in_specs=[pl.BlockSpec((tm, D), lambda i, j: (i, 0)),
                  pl.BlockSpec((1, D), lambda i, j: (0, 0)),
                  mspec(0), mspec(1),
                  pl.BlockSpec((D, tn), lambda i, j: (0, j))],
        out_specs=pl.BlockSpec((tm, tn), lambda i, j: (i, j)),
        out_shape=jax.ShapeDtypeStruct((T, N), F32),
        scratch_shapes=[pltpu.VMEM((tm, D), BF16)],
        compiler_params=_cparams("parallel", "arbitrary"),
        name="in_proj",
    )(x, g, modv, modv, w)


def _rope(x, tab_ref, blk):
    return (x * tab_ref[0] + pltpu.roll(x, blk, 1) * tab_ref[1]
            + pltpu.roll(x, HEAD_DIM - blk, 1) * tab_ref[2])


def _mla_kv(ckv, krope, wkvb_ref, gkn_ref, gkr_ref, tab_ref, mk_ref, mv_ref):
    kv = _dot(ckv.astype(BF16), wkvb_ref[...])
    nv = MLA_HEADS * HEAD_DIM
    mv_ref[...] = kv[:, nv:].astype(BF16)
    ssr = jnp.sum(krope * krope, axis=-1, keepdims=True)
    for h in range(MLA_HEADS):
        kn = kv[:, h * HEAD_DIM:(h + 1) * HEAD_DIM]
        rs = lax.rsqrt((jnp.sum(kn * kn, axis=-1, keepdims=True) + ssr) * (1.0 / MLA_QK) + EPS)
        kr = krope * rs * gkr_ref[...]
        if tab_ref is not None:
            kr = _rope(kr, tab_ref, MLA_ROPE // 4)
        mk_ref[:, h * MLA_PAD:h * MLA_PAD + HEAD_DIM] = (kn * rs * gkn_ref[...]).astype(BF16)
        mk_ref[:, h * MLA_PAD + HEAD_DIM:(h + 1) * MLA_PAD] = kr.astype(BF16)


def _prep_kernel(gq_ref, gk_ref, gv_ref, mkr_ref, mqa_ref, mkva_ref, tg_ref, tm_ref,
                 gqn_ref, gkn_ref, qan_ref, kvan_ref, wqb_ref, wkvb_ref,
                 mqn_n_ref, mqn_r_ref, mkn_n_ref, mkn_r_ref,
                 gq_o, gkn_o, gkr_o, gvb_o, mq_o, ckv_o, mk_o, mv_o):
    gscale = HEAD_DIM ** -0.5
    for h in range(GQA_HEADS):
        sl = slice(h * HEAD_DIM, (h + 1) * HEAD_DIM)
        q = _rms(gq_ref[:, sl], HEAD_DIM) * gqn_ref[...]
        gq_o[:, sl] = (_rope(q, tg_ref, HEAD_DIM // 4) * gscale).astype(BF16)
    for h in range(GQA_KV_HEADS):
        sl = slice(h * HEAD_DIM, (h + 1) * HEAD_DIM)
        k = _rms(gk_ref[:, sl], HEAD_DIM) * gkn_ref[...]
        gkn_o[:, sl] = k
        gkr_o[:, sl] = _rope(k, tg_ref, HEAD_DIM // 4).astype(BF16)
    gvb_o[...] = gv_ref[...].astype(BF16)

    qa = _rms(mqa_ref[...], MLA_RANK) * qan_ref[...]
    mq = _dot(qa.astype(BF16), wqb_ref[...])
    mscale = MLA_QK ** -0.5
    for h in range(MLA_HEADS):
        qn = mq[:, h * MLA_PAD:h * MLA_PAD + HEAD_DIM]
        qr = mq[:, h * MLA_PAD + HEAD_DIM:(h + 1) * MLA_PAD]
        ss = jnp.sum(qn * qn, axis=-1, keepdims=True) + jnp.sum(qr * qr, axis=-1, keepdims=True)
        rs = lax.rsqrt(ss * (1.0 / MLA_QK) + EPS) * mscale
        mq_o[:, h * MLA_PAD:h * MLA_PAD + HEAD_DIM] = (qn * rs * mqn_n_ref[...]).astype(BF16)
        mq_o[:, h * MLA_PAD + HEAD_DIM:(h + 1) * MLA_PAD] = _rope(qr * rs * mqn_r_ref[...], tm_ref, MLA_ROPE // 4).astype(BF16)

    ckv = _rms(mkva_ref[...], MLA_RANK) * kvan_ref[...]
    ckv_o[...] = ckv
    _mla_kv(ckv, mkr_ref[...], wkvb_ref, mkn_n_ref, mkn_r_ref, tm_ref, mk_o, mv_o)


def _prep(z, tabg, tabm, table_of_tile, tm, gqn, gkn, qan, kvan, wqb, wkvb, mqn_n, mqn_r, mkn_n, mkn_r):
    T = z.shape[0]
    row = lambda w: pl.BlockSpec((1, w), lambda i: (0, 0))
    full = lambda a: pl.BlockSpec(a.shape, lambda i: (0, 0))
    tab = pl.BlockSpec((3, tm, HEAD_DIM), lambda i: (0, table_of_tile(i), 0))
    zspec = lambda w, blk: pl.BlockSpec((tm, w), lambda i: (i, blk))
    ospec = lambda w: pl.BlockSpec((tm, w), lambda i: (i, 0))
    widths = (GQA_HEADS * HEAD_DIM, GQA_KV_HEADS * HEAD_DIM, GQA_KV_HEADS * HEAD_DIM, GQA_KV_HEADS * HEAD_DIM,
              MLA_HEADS * MLA_PAD, MLA_RANK, MLA_HEADS * MLA_PAD, MLA_HEADS * HEAD_DIM)
    dtypes = (BF16, F32, BF16, BF16, BF16, F32, BF16, BF16)
    return pl.pallas_call(
        _prep_kernel,
        grid=(T // tm,),
        in_specs=[zspec(768, 0), zspec(256, 3), zspec(256, 4), zspec(128, 10), zspec(512, 3), zspec(512, 4),
                  tab, tab, row(128), row(128), row(512), row(512), full(wqb), full(wkvb),
                  row(128), row(128), row(128), row(128)],
        out_specs=[ospec(w) for w in widths],
        out_shape=[jax.ShapeDtypeStruct((T, w), d) for w, d in zip(widths, dtypes)],
        compiler_params=_cparams("parallel"),
        name="prep",
    )(z, z, z, z, z, z, tabg, tabm, gqn, gkn, qan, kvan, wqb, wkvb, mqn_n, mqn_r, mkn_n, mkn_r)


def _cache_kernel(ckv_ref, kr_ref, wkvb_ref, gkn_ref, gkr_ref, mk_o, mv_o):
    _mla_kv(ckv_ref[...], kr_ref[...], wkvb_ref, gkn_ref, gkr_ref, None, mk_o, mv_o)


def _cache_expand(ckv, kr, wkvb, mkn_n, mkn_r):
    L, R, _ = ckv.shape
    tm = min(512, R)
    return pl.pallas_call(
        _cache_kernel,
        grid=(L, R // tm),
        in_specs=[pl.BlockSpec((None, tm, MLA_RANK), lambda l, i: (l, i, 0)),
                  pl.BlockSpec((None, tm, HEAD_DIM), lambda l, i: (l, i, 0)),
                  pl.BlockSpec((None,) + wkvb.shape[1:], lambda l, i: (l, 0, 0)),
                  pl.BlockSpec((None, 1, HEAD_DIM), lambda l, i: (l, 0, 0)),
                  pl.BlockSpec((None, 1, HEAD_DIM), lambda l, i: (l, 0, 0))],
        out_specs=[pl.BlockSpec((None, tm, MLA_HEADS * MLA_PAD), lambda l, i: (l, i, 0)),
                   pl.BlockSpec((None, tm, MLA_HEADS * HEAD_DIM), lambda l, i: (l, i, 0))],
        out_shape=[jax.ShapeDtypeStruct((L, R, MLA_HEADS * MLA_PAD), BF16),
                   jax.ShapeDtypeStruct((L, R, MLA_HEADS * HEAD_DIM), BF16)],
        compiler_params=_cparams("parallel", "parallel"),
        name="cache_expand",
    )(ckv, kr, wkvb, mkn_n, mkn_r)


def _attn_kernel(q_ref, k_ref, v_ref, o_ref, m_sc, l_sc, acc_sc):
    ki = pl.program_id(3)

    @pl.when(ki == 0)
    def _():
        m_sc[...] = jnp.full_like(m_sc, -jnp.inf)
        l_sc[...] = jnp.zeros_like(l_sc)
        acc_sc[...] = jnp.zeros_like(acc_sc)

    s = _dot_nt(q_ref[...], k_ref[...])
    m_prev = m_sc[...]
    m_new = jnp.maximum(m_prev, jnp.max(s, axis=-1, keepdims=True))
    p = jnp.exp(s - m_new)
    alpha = jnp.exp(m_prev - m_new)
    l_sc[...] = alpha * l_sc[...] + jnp.sum(p, axis=-1, keepdims=True)
    acc_sc[...] = alpha * acc_sc[...] + _dot(p.astype(BF16), v_ref[...])
    m_sc[...] = m_new

    @pl.when(ki == pl.num_programs(3) - 1)
    def _():
        o_ref[...] = (acc_sc[...] / l_sc[...]).astype(o_ref.dtype)


def _attention(q, k, v, *, batch, nq, nk, q_row0, k_row0, heads, group, dq, dv, tq, tk):
    nqb, nkb = nq // tq, nk // tk
    qb0, kb0 = q_row0 // tq, k_row0 // tk
    return pl.pallas_call(
        _attn_kernel,
        grid=(batch, heads, nqb, nkb),
        in_specs=[pl.BlockSpec((tq, dq), lambda b, h, i, j: (qb0 + b * nqb + i, h)),
                  pl.BlockSpec((tk, dq), lambda b, h, i, j: (kb0 + b * nkb + j, h // group)),
                  pl.BlockSpec((tk, dv), lambda b, h, i, j: (kb0 + b * nkb + j, h // group))],
        out_specs=pl.BlockSpec((tq, dv), lambda b, h, i, j: (b * nqb + i, h)),
        out_shape=jax.ShapeDtypeStruct((batch * nq, heads * dv), BF16),
        scratch_shapes=[pltpu.VMEM((tq, 1), F32), pltpu.VMEM((tq, 1), F32), pltpu.VMEM((tq, dv), F32)],
        compiler_params=_cparams("parallel", "parallel", "parallel", "arbitrary"),
        name="attention",
    )(q, k, v)


def _hgrn_constants(c):
    levels = int(np.log2(c))
    nr = (levels + 2) * c
    L = np.zeros((2, nr, c), np.float32)
    M = np.zeros((2, levels, c, c), np.float32)
    for t in range(c):
        L[0, t, :t + 1] = 1.0
        L[0, c + t, t + 1:] = 1.0
    for lv in range(levels):
        m = c >> (lv + 1)
        for a in range(0, c, 2 * m):
            mid = a + m
            for t in range(mid, a + 2 * m):
                L[0, (2 + lv) * c + t, mid:t + 1] = 1.0
                M[0, lv, t, a:mid] = 1.0
            for s in range(a, mid):
                L[0, (2 + lv) * c + s, s + 1:mid] = 1.0
    Lb = L[0].reshape(levels + 2, c, c)[:, ::-1, ::-1]
    L[1] = Lb.reshape(nr, c)
    M[1] = M[0][:, ::-1, ::-1]
    return L, M


def _hgrn_kernel(q_ref, z_ref, v_ref, lb_ref, s0_ref, L_ref, M_ref, o_ref, sf_ref, s_sc):
    c = q_ref.shape[0]
    levels = M_ref.shape[0]

    @pl.when(pl.program_id(2) == 0)
    def _():
        s_sc[...] = s0_ref[...]

    z = z_ref[...]
    e = jnp.exp(-jnp.abs(z))
    r = 1.0 / (1.0 + e)
    logsig = jnp.minimum(z, 0.0) + jnp.log(r)
    a = lb_ref[0:1, :]
    cc = lb_ref[1:2, :] + logsig
    logf = jnp.maximum(a, cc) + jnp.log(1.0 + jnp.exp(-jnp.abs(a - cc)))
    kk = lb_ref[2:3, :] * jnp.where(z >= 0.0, e * r, r)

    lmat = L_ref[...]
    p1, p2, p3 = _split3(logf)
    ex = jnp.exp(_dot(lmat, p1) + _dot(lmat, p2) + _dot(lmat, p3))
    ones = jnp.ones((c, HEAD_DIM), BF16)
    for h in range(HG_HEADS):
        sl = slice(h * HEAD_DIM, (h + 1) * HEAD_DIM)
        q, k, v = q_ref[:, sl], kk[:, sl], v_ref[:, sl]
        vb = v.astype(BF16)
        amat = jnp.zeros((c, c), F32)
        for lv in range(levels):
            el = ex[(2 + lv) * c:(3 + lv) * c, sl]
            amat = amat + M_ref[lv] * _dot_nt((q * el).astype(BF16), (k * el).astype(BF16))
        s_prev = s_sc[h]
        o = (_dot(amat.astype(BF16), vb) + jnp.sum(q * k, axis=-1, keepdims=True) * v
             + _dot((q * ex[0:c, sl]).astype(BF16), s_prev.astype(BF16)))
        o_ref[:, sl] = o
        dcol = jnp.exp(_dot_tn(p1[:, sl], ones) + _dot_tn(p2[:, sl], ones) + _dot_tn(p3[:, sl], ones))
        s_new = dcol * s_prev + _dot_tn((k * ex[c:2 * c, sl]).astype(BF16), vb)
        s_sc[h] = s_new
        sf_ref[h] = s_new


def _hgrn(z, lbtab, s0, *, batch, n, row0):
    c = min(HG_CHUNK, n)
    nc = n // c
    cb0 = row0 // c
    Lm, Mm = _hgrn_constants(c)
    Lm, Mm = jnp.asarray(Lm, BF16), jnp.asarray(Mm, F32)
    w = HG_HEADS * HEAD_DIM

    def rows(d, b, j):
        return cb0 + b * nc + j + d * (nc - 1 - 2 * j)

    return pl.pallas_call(
        _hgrn_kernel,
        grid=(2, batch, nc),
        in_specs=[pl.BlockSpec((c, w), lambda d, b, j: (rows(d, b, j), 5)),
                  pl.BlockSpec((c, w), lambda d, b, j: (rows(d, b, j), 6 + d)),
                  pl.BlockSpec((c, w), lambda d, b, j: (rows(d, b, j), 8)),
                  pl.BlockSpec((None, 8, w), lambda d, b, j: (d, 0, 0)),
                  pl.BlockSpec((None, None, HG_HEADS, HEAD_DIM, HEAD_DIM), lambda d, b, j: (b, d, 0, 0, 0)),
                  pl.BlockSpec((None,) + Lm.shape[1:], lambda d, b, j: (d, 0, 0)),
                  pl.BlockSpec((None,) + Mm.shape[1:], lambda d, b, j: (d, 0, 0, 0))],
        out_specs=[pl.BlockSpec((None, c, w), lambda d, b, j: (d, rows(d, b, j) - cb0, 0)),
                   pl.BlockSpec((None, None, HG_HEADS, HEAD_DIM, HEAD_DIM), lambda d, b, j: (b, d, 0, 0, 0))],
        out_shape=[jax.ShapeDtypeStruct((2, batch * n, w), F32),
                   jax.ShapeDtypeStruct((batch, 2, HG_HEADS, HEAD_DIM, HEAD_DIM), F32)],
        scratch_shapes=[pltpu.VMEM((HG_HEADS, HEAD_DIM, HEAD_DIM), F32)],
        compiler_params=_cparams("parallel", "parallel", "arbitrary"),
        name="hgrn_scan",
    )(z, z, z, lbtab, s0, Lm, Mm)


def _out_kernel(x_ref, ga_ref, of_ref, ob_ref, hg_ref, ma_ref, w_ref, on_ref, g2_ref,
                g1_ref, sh2_ref, sc2_ref, xo_ref, hhi_ref, hlo_ref):
    o = of_ref[...] + ob_ref[...]
    g = hg_ref[...]
    g = g * (1.0 / (1.0 + jnp.exp(-g)))
    n_g, n_h = GQA_HEADS * HEAD_DIM, HG_HEADS * HEAD_DIM
    mix = _dot(ga_ref[...], w_ref[0:n_g, :]) + _dot(ma_ref[...], w_ref[n_g + n_h:, :])
    for h in range(HG_HEADS):
        sl = slice(h * HEAD_DIM, (h + 1) * HEAD_DIM)
        oh = _rms(o[:, sl], HEAD_DIM) * on_ref[...] * g[:, sl]
        mix = mix + _dot(oh.astype(BF16), w_ref[n_g + h * HEAD_DIM:n_g + (h + 1) * HEAD_DIM, :])
    x = x_ref[...] + g1_ref[...] * mix
    xo_ref[...] = x
    h2 = _rms(x, x.shape[-1]) * g2_ref[...] * (1.0 + sc2_ref[...]) + sh2_ref[...]
    hi, lo = _split2(h2)
    hhi_ref[...] = hi
    hlo_ref[...] = lo


def _mixer_out(x, ga, o2, z, ma, w, on, g2, modv, mod_row_of_tile, tm):
    T, D = x.shape
    wh = HG_HEADS * HEAD_DIM

    def mspec(k):
        return pl.BlockSpec((None, 1, D), lambda i: (mod_row_of_tile(i) + k, 0, 0))

    return pl.pallas_call(
        _out_kernel,
        grid=(T // tm,),
        in_specs=[pl.BlockSpec((tm, D), lambda i: (i, 0)),
                  pl.BlockSpec((tm, ga.shape[1]), lambda i: (i, 0)),
                  pl.BlockSpec((None, tm, wh), lambda i: (0, i, 0)),
                  pl.BlockSpec((None, tm, wh), lambda i: (1, i, 0)),
                  pl.BlockSpec((tm, wh), lambda i: (i, 9)),
                  pl.BlockSpec((tm, ma.shape[1]), lambda i: (i, 0)),
                  pl.BlockSpec(w.shape, lambda i: (0, 0)),
                  pl.BlockSpec((1, HEAD_DIM), lambda i: (0, 0)),
                  pl.BlockSpec((1, D), lambda i: (0, 0)),
                  mspec(2), mspec(3), mspec(4)],
        out_specs=[pl.BlockSpec((tm, D), lambda i: (i, 0))] * 3,
        out_shape=[jax.ShapeDtypeStruct((T, D), F32), jax.ShapeDtypeStruct((T, D), BF16),
                   jax.ShapeDtypeStruct((T, D), BF16)],
        compiler_params=_cparams("parallel"),
        name="mixer_out",
    )(x, ga, o2, o2, z, ma, w, on, g2, modv, modv, modv)


def _ws_kernel(k_ref, wq_ref, hi_ref, lo_ref):
    khi, klo = _split2(k_ref[...])
    whi, wlo = _split2(wq_ref[...])
    acc = _dot_nt(khi, whi) + _dot_nt(klo, whi) + _dot_nt(khi, wlo)
    hi, lo = _split2(acc)
    hi_ref[...] = hi
    lo_ref[...] = lo


def _score_weights(keys, wq):
    L, D, _ = wq.shape
    nb = 2 * PEER_HEADS
    sub = keys.shape[-1]
    kk = keys.reshape(L, nb, PEER_KEYS, sub)
    out = jax.ShapeDtypeStruct((L, nb * PEER_KEYS, D), BF16)
    ospec = pl.BlockSpec((None, PEER_KEYS, D), lambda l, b: (l, (b % 2) * PEER_HEADS + b // 2, 0))
    return pl.pallas_call(
        _ws_kernel,
        grid=(L, nb),
        in_specs=[pl.BlockSpec((None, None, PEER_KEYS, sub), lambda l, b: (l, b, 0, 0)),
                  pl.BlockSpec((None, D, sub), lambda l, b: (l, 0, b))],
        out_specs=[ospec, ospec],
        out_shape=[out, out],
        compiler_params=_cparams("parallel", "parallel"),
        name="score_weights",
    )(kk, wq)


def _score_kernel(whi_ref, wlo_ref, hhi_ref, hlo_ref, o_ref):
    whi, hhi = whi_ref[...], hhi_ref[...]
    o_ref[...] = _dot_nt(whi, hhi) + _dot_nt(whi, hlo_ref[...]) + _dot_nt(wlo_ref[...], hhi)


def _scores(whi, wlo, hhi, hlo, tm):
    R, D = whi.shape
    T = hhi.shape[0]
    tr = min(512, R)
    return pl.pallas_call(
        _score_kernel,
        grid=(R // tr, T // tm),
        in_specs=[pl.BlockSpec((tr, D), lambda r, i: (r, 0)), pl.BlockSpec((tr, D), lambda r, i: (r, 0)),
                  pl.BlockSpec((tm, D), lambda r, i: (i, 0)), pl.BlockSpec((tm, D), lambda r, i: (i, 0))],
        out_specs=pl.BlockSpec((tr, tm), lambda r, i: (r, i)),
        out_shape=jax.ShapeDtypeStruct((R, T), F32),
        compiler_params=_cparams("parallel", "parallel"),
        name="scores",
    )(whi, wlo, hhi, hlo)


def _batcher_pairs(n):
    pairs = []
    p = 1
    while p < n:
        k = p
        while k >= 1:
            for j in range(k % p, n - k, 2 * k):
                for i in range(min(k, n - j - k)):
                    if (i + j) // (2 * p) == (i + j + k) // (2 * p):
                        pairs.append((i + j, i + j + k))
            k //= 2
        p *= 2
    return pairs


_SORT16 = _batcher_pairs(PEER_TOPK)


def _sort_desc(v):
    v = list(v)
    for i, j in _SORT16:
        v[i], v[j] = jnp.maximum(v[i], v[j]), jnp.minimum(v[i], v[j])
    return v


def _merge_top(a, b):
    n = len(a)
    v = [jnp.maximum(a[i], b[n - 1 - i]) for i in range(n)]
    s = n // 2
    while s >= 1:
        for i in range(n):
            if (i & s) == 0:
                v[i], v[i + s] = jnp.maximum(v[i], v[i + s]), jnp.minimum(v[i], v[i + s])
        s //= 2
    return v


def _top16_of_keys(s_ref, row0, lanes):
    k = PEER_TOPK
    groups = [_sort_desc([s_ref[pl.ds(row0 + PEER_HEADS * (k * g + i), PEER_HEADS), lanes] for i in range(k)])
              for g in range(PEER_KEYS // k)]
    while len(groups) > 1:
        groups = [_merge_top(groups[i], groups[i + 1]) for i in range(0, len(groups), 2)]
    return groups[0]


def _topk_kernel(s_ref, thr_ref, c0_ref):
    k = PEER_TOPK
    ninf = jnp.full((PEER_HEADS, 128), -jnp.inf, F32)

    def chunk(ch, carry):
        lanes = pl.ds(pl.multiple_of(ch * 128, 128), 128)
        a = _top16_of_keys(s_ref, 0, lanes)
        b = _top16_of_keys(s_ref, PEER_HEADS * PEER_KEYS, lanes)
        cand = [[a[i] + b[j] for j in range(k // (i + 1))] for i in range(k)]
        g0 = cand[0]
        g1 = _sort_desc(cand[1] + cand[2] + cand[5] + cand[8])
        g2 = _sort_desc(cand[3] + cand[4] + cand[6] + cand[7] + cand[9] + cand[10] + cand[11] + cand[12] + cand[13])
        g3 = cand[14] + cand[15] + [ninf] * (k - 2)
        top = _merge_top(_merge_top(g0, g1), _merge_top(g2, g3))
        zsum = jnp.zeros((PEER_HEADS, 128), F32)
        for t in top:
            zsum = zsum + jnp.exp(t - top[0])
        c0 = top[0] + jnp.log(zsum)
        thr_ref[:, lanes] = top[k - 1] - c0
        c0_ref[:, lanes] = c0
        return carry

    lax.fori_loop(0, s_ref.shape[1] // 128, chunk, 0)


def _topk(st, tm):
    R, T = st.shape
    out = jax.ShapeDtypeStruct((PEER_HEADS, T), F32)
    return pl.pallas_call(
        _topk_kernel,
        grid=(T // tm,),
        in_specs=[pl.BlockSpec((2 * PEER_HEADS * PEER_KEYS, tm), lambda i: (0, i))],
        out_specs=[pl.BlockSpec((PEER_HEADS, tm), lambda i: (0, i))] * 2,
        out_shape=[out, out],
        compiler_params=_cparams("parallel"),
        name="topk",
    )(st)


def _peer_kernel(h_ref, s0_ref, s1_ref, thr_ref, c0_ref, u_ref, vt_ref, x_ref, g2_ref, o_ref,
                 s1n_sc, c_sc, acc_sc):
    e = pl.program_id(1)
    te, tm = c_sc.shape
    nk = PEER_KEYS

    @pl.when(e == 0)
    def _():
        acc_sc[...] = jnp.zeros_like(acc_sc)
        for h in range(PEER_HEADS):
            s1n_sc[h * nk:(h + 1) * nk, :] = s1_ref[h * nk:(h + 1) * nk, :] - c0_ref[h:h + 1, :]

    pre = _dot_nt(u_ref[...], h_ref[...])
    act = 0.5 * pre * (1.0 + lax.erf(pre * (2.0 ** -0.5)))
    for ii in range(te // nk):
        for ch in range(tm // 128):
            lanes = slice(ch * 128, (ch + 1) * 128)
            gate = jnp.zeros((nk, 128), F32)
            for h in range(PEER_HEADS):
                r0 = ii * PEER_HEADS + h
                val = s1n_sc[h * nk:(h + 1) * nk, lanes] + s0_ref[r0:r0 + 1, lanes]
                gate = gate + jnp.where(val >= thr_ref[h:h + 1, lanes], jnp.exp(val), 0.0)
            c_sc[ii * nk:(ii + 1) * nk, lanes] = (gate * act[ii * nk:(ii + 1) * nk, lanes]).astype(BF16)
    acc_sc[...] += _dot(vt_ref[...], c_sc[...])

    @pl.when(e == pl.num_programs(1) - 1)
    def _():
        o_ref[...] = x_ref[...] + g2_ref[...] * acc_sc[...].T


def _peer(hb, st, thr, c0, u, vt, x, modv, mod_row_of_tile, tm):
    T, D = x.shape
    E = u.shape[0]
    te = 512
    half = PEER_HEADS * PEER_KEYS
    return pl.pallas_call(
        _peer_kernel,
        grid=(T // tm, E // te),
        in_specs=[pl.BlockSpec((tm, D), lambda i, e: (i, 0)),
                  pl.BlockSpec((te // PEER_KEYS * PEER_HEADS, tm), lambda i, e: (e, i)),
                  pl.BlockSpec((half, tm), lambda i, e: (2, i)),
                  pl.BlockSpec((PEER_HEADS, tm), lambda i, e: (0, i)),
                  pl.BlockSpec((PEER_HEADS, tm), lambda i, e: (0, i)),
                  pl.BlockSpec((te, D), lambda i, e: (e, 0)),
                  pl.BlockSpec((D, te), lambda i, e: (0, e)),
                  pl.BlockSpec((tm, D), lambda i, e: (i, 0)),
                  pl.BlockSpec((None, 1, D), lambda i, e: (mod_row_of_tile(i) + 5, 0, 0))],
        out_specs=pl.BlockSpec((tm, D), lambda i, e: (i, 0)),
        out_shape=jax.ShapeDtypeStruct((T, D), F32),
        scratch_shapes=[pltpu.VMEM((half, tm), F32), pltpu.VMEM((te, tm), BF16), pltpu.VMEM((D, tm), F32)],
        compiler_params=_cparams("parallel", "arbitrary"),
        name="peer_experts",
    )(hb, st, st, thr, c0, u, vt, x, modv)


def _rope_tables(n, ident_rows):
    pos = jnp.arange(n, dtype=jnp.int32)
    row, col = pos // GRID_W, pos % GRID_W
    lane = jnp.arange(HEAD_DIM)

    def build(nfreq, width):
        inv = ROPE_THETA ** (-jnp.arange(nfreq, dtype=F32) / nfreq)
        ar = row.astype(F32)[:, None] * inv[None, :]
        ac = col.astype(F32)[:, None] * inv[None, :]
        cos = jnp.concatenate([jnp.cos(ar), jnp.cos(ar), jnp.cos(ac), jnp.cos(ac)], axis=-1)
        sin = jnp.concatenate([jnp.sin(ar), jnp.sin(ar), jnp.sin(ac), jnp.sin(ac)], axis=-1)
        pad = HEAD_DIM - width
        cos = jnp.pad(cos, ((0, 0), (0, pad)))
        sin = jnp.pad(sin, ((0, 0), (0, pad)))
        first = (lane % (2 * nfreq)) < nfreq
        tab = jnp.stack([cos, jnp.where(first, 0.0, sin), jnp.where(first, -sin, 0.0)])
        ident = jnp.stack([jnp.ones((ident_rows, HEAD_DIM), F32), jnp.zeros((ident_rows, HEAD_DIM), F32),
                           jnp.zeros((ident_rows, HEAD_DIM), F32)])
        return jnp.concatenate([ident, tab], axis=1)

    return build(HEAD_DIM // 4, HEAD_DIM), build(MLA_ROPE // 4, MLA_ROPE)


def kernel(x_prompt, x_sample, c, cache_gqa_k, cache_gqa_v, cache_mla_ckv, cache_mla_krope, state_hgrn, c_ctx,
           w_mod, b_mod, norm1_g, norm2_g, w_in, gqa_qn, gqa_kn, mla_qa_n, mla_kva_n, w_mla_qb, w_mla_kvb,
           mla_qn, mla_kn, hg_lb_logits, hg_on, w_out, peer_wq, peer_keys, peer_u, peer_v):
    B, S, D = x_prompt.shape
    DB, DS, _ = x_sample.shape
    L = w_mod.shape[0]
    P = cache_gqa_k.shape[2]
    TC, TL = B * S, DB * DS
    T = TC + TL
    tm = 512 if (TC % 512 == 0 and DS % 512 == 0) else 128
    tm_small = min(256, tm)

    ngroups = 1 + DB

    def mod_row_of(t, layer):
        def f(i):
            group = jnp.where(i < TC // t, 0, 1 + (i - TC // t) // (DS // t))
            return (layer * ngroups + group) * 6
        return f

    def table_of(t):
        def f(i):
            return jnp.where(i < TC // t, 0, 1 + (i - TC // t) % (DS // t))
        return f

    cond8 = jnp.zeros((8, D), F32).at[0].set(c_ctx).at[1:1 + DB].set(c)
    modv = _modulation(cond8, w_mod, b_mod)[:, :ngroups].reshape(L * ngroups * 6, 1, D)

    lb_cum = jnp.cumsum(jax.nn.softmax(hg_lb_logits.astype(F32), axis=0), axis=0)
    lbs = jnp.maximum(lb_cum - lb_cum[:1], 0.0)
    lbtab = jnp.stack([jnp.log(lbs), jnp.log1p(-lbs), 1.0 - lbs] + [jnp.zeros_like(lbs)] * 5, axis=2)

    sizes = np.cumsum([0, 768, 256, 256, 512, 512, 512, 512, 512, 512, 512, 64])
    part = lambda a, k: a[..., sizes[k]:sizes[k + 1]]
    order = (0, 1, 2, 10, 8, 9, 3, 4, 5, 6, 7)
    cols = []
    for k in order:
        cols.append(part(w_in, k))
        if k == 10:
            cols.append(jnp.zeros((L, D, 256 - MLA_ROPE), w_in.dtype))
    w_in_b = jnp.concatenate(cols, axis=-1).astype(BF16)

    wqb = w_mla_qb.reshape(L, MLA_RANK, MLA_HEADS, MLA_QK)
    wqb = jnp.pad(wqb, ((0, 0), (0, 0), (0, 0), (0, MLA_PAD - MLA_QK))).reshape(L, MLA_RANK, MLA_HEADS * MLA_PAD).astype(BF16)
    wkvb = w_mla_kvb.reshape(L, MLA_RANK, MLA_HEADS, 2, HEAD_DIM)
    wkvb = jnp.moveaxis(wkvb, 3, 2).reshape(L, MLA_RANK, 2 * MLA_HEADS * HEAD_DIM).astype(BF16)
    rpad = lambda g: jnp.pad(g[:, HEAD_DIM:], ((0, 0), (0, HEAD_DIM - MLA_ROPE)))[:, None, :]
    mqn_n, mqn_r = mla_qn[:, None, :HEAD_DIM], rpad(mla_qn)
    mkn_n, mkn_r = mla_kn[:, None, :HEAD_DIM], rpad(mla_kn)
    w_out_b = w_out.astype(BF16)
    u_b = peer_u.astype(BF16)
    vt_b = jnp.swapaxes(peer_v, 1, 2).astype(BF16)
    def key_major(w):
        half = PEER_HEADS * PEER_KEYS
        km = jnp.swapaxes(w.reshape(L, 2, PEER_HEADS, PEER_KEYS, D), 2, 3).reshape(L, 2 * half, D)
        return jnp.concatenate([km, w[:, half:]], axis=1)

    ws_hi, ws_lo = map(key_major, _score_weights(peer_keys, peer_wq))

    tabg, tabm = _rope_tables(DS, tm_small)

    cck = jnp.moveaxis(cache_mla_ckv, 1, 0).reshape(L, DB * P, MLA_RANK)
    ckr = jnp.pad(jnp.moveaxis(cache_mla_krope, 1, 0).reshape(L, DB * P, MLA_ROPE), ((0, 0), (0, 0), (0, HEAD_DIM - MLA_ROPE)))
    cmk, cmv = _cache_expand(cck, ckr, wkvb, mkn_n, mkn_r)
    cgk = jnp.moveaxis(cache_gqa_k, 1, 0).reshape(L, DB, P, -1).astype(BF16)
    cgv = jnp.moveaxis(cache_gqa_v, 1, 0).reshape(L, DB, P, -1).astype(BF16)
    cmk = cmk.reshape(L, DB, P, -1)
    cmv = cmv.reshape(L, DB, P, -1)
    s0_ctx = jnp.zeros((B, 2, HG_HEADS, HEAD_DIM, HEAD_DIM), F32)

    x = jnp.concatenate([x_prompt.reshape(TC, D), x_sample.reshape(TL, D)], axis=0)
    new_k, new_v, new_ckv, new_kr, new_s = [], [], [], [], []
    tq_c = min(256, S)
    tq_l, tk_l = min(1024, DS), min(512, P, DS)
    for l in range(L):
        z = _in_proj(x, norm1_g[l][None], modv, w_in_b[l], mod_row_of(tm, l), tm)
        gq, gkn, gkr, gvb, mq, ckv, mk, mv = _prep(
            z, tabg, tabm, table_of(tm_small), tm_small, gqa_qn[l][None], gqa_kn[l][None], mla_qa_n[l][None],
            mla_kva_n[l][None], wqb[l], wkvb[l], mqn_n[l], mqn_r[l], mkn_n[l], mkn_r[l])
        new_k.append(gkn[:TC].reshape(B, S, GQA_KV_HEADS, HEAD_DIM))
        new_v.append(z[:TC, 1024:1280].reshape(B, S, GQA_KV_HEADS, HEAD_DIM))
        new_ckv.append(ckv[:TC].reshape(B, S, MLA_RANK))
        new_kr.append(z[:TC, 1280:1280 + MLA_ROPE].reshape(B, S, MLA_ROPE))

        def lat_keys(cache, new):
            return jnp.concatenate([cache, new[TC:].reshape(DB, DS, -1)], axis=1).reshape(DB * (P + DS), -1)

        gqa = dict(heads=GQA_HEADS, group=GQA_HEADS // GQA_KV_HEADS, dq=HEAD_DIM, dv=HEAD_DIM)
        mla = dict(heads=MLA_HEADS, group=1, dq=MLA_PAD, dv=HEAD_DIM)
        ctx_a = dict(batch=B, nq=S, nk=S, q_row0=0, k_row0=0, tq=tq_c, tk=tq_c)
        lat_a = dict(batch=DB, nq=DS, nk=P + DS, q_row0=TC, k_row0=0, tq=tq_l, tk=tk_l)
        ga = jnp.concatenate([_attention(gq, gkr, gvb, **gqa, **ctx_a),
                              _attention(gq, lat_keys(cgk[l], gkr), lat_keys(cgv[l], gvb), **gqa, **lat_a)], axis=0)
        ma = jnp.concatenate([_attention(mq, mk, mv, **mla, **ctx_a),
                              _attention(mq, lat_keys(cmk[l], mk), lat_keys(cmv[l], mv), **mla, **lat_a)], axis=0)

        o_c, s_c = _hgrn(z, lbtab[l], s0_ctx, batch=B, n=S, row0=0)
        o_l, _ = _hgrn(z, lbtab[l], state_hgrn[:, l].astype(F32), batch=DB, n=DS, row0=TC)
        new_s.append(s_c)
        o2 = jnp.concatenate([o_c, o_l], axis=1)

        x, hhi, hlo = _mixer_out(x, ga, o2, z, ma, w_out_b[l], hg_on[l][None], norm2_g[l][None], modv,
                                 mod_row_of(tm_small, l), tm_small)
        st = _scores(ws_hi[l], ws_lo[l], hhi, hlo, tm)
        thr, c0 = _topk(st, tm)
        x = _peer(hhi, st, thr, c0, u_b[l], vt_b[l], x, modv, mod_row_of(tm, l), tm)

    return (x[:TC].reshape(B, S, D), x[TC:].reshape(DB, DS, D),
            jnp.stack(new_k, axis=1), jnp.stack(new_v, axis=1), jnp.stack(new_ckv, axis=1),
            jnp.stack(new_kr, axis=1), jnp.stack(new_s, axis=1))
```

```python
import functools

import numpy as np
import jax
import jax.numpy as jnp
from jax import lax
from jax.experimental import pallas as pl
from jax.experimental.pallas import tpu as pltpu

F32 = jnp.float32
BF16 = jnp.bfloat16

EPS = 1e-6
LOG2E = 1.4426950408889634
ROPE_THETA = 10000.0
GRID_W = 64

HEAD_DIM = 128
GQA_HEADS, GQA_KV_HEADS = 6, 2
HG_HEADS = 4
MLA_HEADS, MLA_ROPE, MLA_RANK = 6, 64, 512
MLA_QK = HEAD_DIM + MLA_ROPE
MLA_PAD = 2 * HEAD_DIM
PEER_HEADS, PEER_KEYS, PEER_TOPK = 8, 128, 16
HG_CHUNK = 64

VMEM_LIMIT = 56 * 1024 * 1024


def _cparams(*sem):
    return pltpu.CompilerParams(dimension_semantics=sem, vmem_limit_bytes=VMEM_LIMIT)


def _dot(a, b):
    return jnp.dot(a, b, preferred_element_type=F32)


def _dot_nt(a, b):
    return lax.dot_general(a, b, (((1,), (1,)), ((), ())), preferred_element_type=F32)


def _dot_tn(a, b):
    return lax.dot_general(a, b, (((0,), (0,)), ((), ())), preferred_element_type=F32)


def _split2(a):
    hi = a.astype(BF16)
    lo = (a - hi.astype(F32)).astype(BF16)
    return hi, lo


def _split3(a):
    p1 = a.astype(BF16)
    r = a - p1.astype(F32)
    p2 = r.astype(BF16)
    p3 = (r - p2.astype(F32)).astype(BF16)
    return p1, p2, p3


def _rms(x, n):
    return x * lax.rsqrt(jnp.sum(x * x, axis=-1, keepdims=True) * (1.0 / n) + EPS)


def _mod_kernel(cond_ref, w_ref, b_ref, o_ref):
    c = cond_ref[...]
    s = c * (1.0 / (1.0 + jnp.exp(-c)))
    shi, slo = _split2(s)
    whi, wlo = _split2(w_ref[...])
    o_ref[...] = _dot(shi, whi) + _dot(slo, whi) + _dot(shi, wlo) + b_ref[...]


def _modulation(cond8, w_mod, b_mod):
    L, D, N = w_mod.shape
    tn = min(512, N)
    return pl.pallas_call(
        _mod_kernel,
        grid=(L, N // tn),
        in_specs=[pl.BlockSpec((8, D), lambda l, j: (0, 0)),
                  pl.BlockSpec((None, D, tn), lambda l, j: (l, 0, j)),
                  pl.BlockSpec((None, 1, tn), lambda l, j: (l, 0, j))],
        out_specs=pl.BlockSpec((None, 8, tn), lambda l, j: (l, 0, j)),
        out_shape=jax.ShapeDtypeStruct((L, 8, N), F32),
        compiler_params=_cparams("parallel", "parallel"),
        name="modulation",
    )(cond8, w_mod, b_mod.reshape(L, 1, N))


def _in_kernel(x_ref, g_ref, sh_ref, sc_ref, w_ref, o_ref, h_sc):
    @pl.when(pl.program_id(1) == 0)
    def _():
        x = x_ref[...]
        h = _rms(x, x.shape[-1]) * g_ref[...] * (1.0 + sc_ref[...]) + sh_ref[...]
        h_sc[...] = h.astype(BF16)

    o_ref[...] = _dot(h_sc[...], w_ref[...])


def _in_proj(x, g, modv, w, mod_row_of_tile, tm):
    T, D = x.shape
    N = w.shape[1]
    tn = min(1024, N)

    def mspec(k):
        return pl.BlockSpec((None, 1, D), lambda i, j: (mod_row_of_tile(i) + k, 0, 0))

    return pl.pallas_call(
        _in_kernel,
        grid=(T // tm, N // tn),
        in_specs=[pl.BlockSpec((tm, D), lambda i, j: (i, 0)),
                  pl.BlockSpec((1, D), lambda i, j: (0, 0)),
                  mspec(0), mspec(1),
                  pl.BlockSpec((D, tn), lambda i, j: (0, j))],
        out_specs=pl.BlockSpec((tm, tn), lambda i, j: (i, j)),
        out_shape=jax.ShapeDtypeStruct((T, N), F32),
        scratch_shapes=[pltpu.VMEM((tm, D), BF16)],
        compiler_params=_cparams("parallel", "arbitrary"),
        name="in_proj",
    )(x, g, modv, modv, w)


def _rope(x, tab_ref, blk):
    return (x * tab_ref[0] + pltpu.roll(x, blk, 1) * tab_ref[1]
            + pltpu.roll(x, HEAD_DIM - blk, 1) * tab_ref[2])


def _mla_kv(ckv, krope, wkvb_ref, gkn_ref, gkr_ref, tab_ref, mk_ref, mv_ref):
    kv = _dot(ckv.astype(BF16), wkvb_ref[...])
    nv = MLA_HEADS * HEAD_DIM
    mv_ref[...] = kv[:, nv:].T.astype(BF16)
    ssr = jnp.sum(krope * krope, axis=-1, keepdims=True)
    for h in range(MLA_HEADS):
        kn = kv[:, h * HEAD_DIM:(h + 1) * HEAD_DIM]
        rs = lax.rsqrt((jnp.sum(kn * kn, axis=-1, keepdims=True) + ssr) * (1.0 / MLA_QK) + EPS)
        kr = krope * rs * gkr_ref[...]
        if tab_ref is not None:
            kr = _rope(kr, tab_ref, MLA_ROPE // 4)
        mk_ref[:, h * MLA_PAD:h * MLA_PAD + HEAD_DIM] = (kn * rs * gkn_ref[...]).astype(BF16)
        mk_ref[:, h * MLA_PAD + HEAD_DIM:(h + 1) * MLA_PAD] = kr.astype(BF16)


def _prep_kernel(gq_ref, gk_ref, gv_ref, mkr_ref, mqa_ref, mkva_ref, tg_ref, tm_ref,
                 gqn_ref, gkn_ref, qan_ref, kvan_ref, wqb_ref, wkvb_ref,
                 mqn_n_ref, mqn_r_ref, mkn_n_ref, mkn_r_ref,
                 gq_o, gkn_o, gkr_o, gvt_o, mq_o, ckv_o, mk_o, mv_o):
    gscale = HEAD_DIM ** -0.5 * LOG2E
    for h in range(GQA_HEADS):
        sl = slice(h * HEAD_DIM, (h + 1) * HEAD_DIM)
        q = _rms(gq_ref[:, sl], HEAD_DIM) * gqn_ref[...]
        gq_o[:, sl] = (_rope(q, tg_ref, HEAD_DIM // 4) * gscale).astype(BF16)
    for h in range(GQA_KV_HEADS):
        sl = slice(h * HEAD_DIM, (h + 1) * HEAD_DIM)
        k = _rms(gk_ref[:, sl], HEAD_DIM) * gkn_ref[...]
        gkn_o[:, sl] = k
        gkr_o[:, sl] = _rope(k, tg_ref, HEAD_DIM // 4).astype(BF16)
    gvt_o[...] = gv_ref[...].T.astype(BF16)

    qa = _rms(mqa_ref[...], MLA_RANK) * qan_ref[...]
    mq = _dot(qa.astype(BF16), wqb_ref[...])
    mscale = MLA_QK ** -0.5 * LOG2E
    for h in range(MLA_HEADS):
        qn = mq[:, h * MLA_PAD:h * MLA_PAD + HEAD_DIM]
        qr = mq[:, h * MLA_PAD + HEAD_DIM:(h + 1) * MLA_PAD]
        ss = jnp.sum(qn * qn, axis=-1, keepdims=True) + jnp.sum(qr * qr, axis=-1, keepdims=True)
        rs = lax.rsqrt(ss * (1.0 / MLA_QK) + EPS) * mscale
        mq_o[:, h * MLA_PAD:h * MLA_PAD + HEAD_DIM] = (qn * rs * mqn_n_ref[...]).astype(BF16)
        mq_o[:, h * MLA_PAD + HEAD_DIM:(h + 1) * MLA_PAD] = _rope(qr * rs * mqn_r_ref[...], tm_ref, MLA_ROPE // 4).astype(BF16)

    ckv = _rms(mkva_ref[...], MLA_RANK) * kvan_ref[...]
    ckv_o[...] = ckv
    _mla_kv(ckv, mkr_ref[...], wkvb_ref, mkn_n_ref, mkn_r_ref, tm_ref, mk_o, mv_o)


def _prep(z, tabg, tabm, table_of_tile, tm, gqn, gkn, qan, kvan, wqb, wkvb, mqn_n, mqn_r, mkn_n, mkn_r):
    T = z.shape[0]
    row = lambda w: pl.BlockSpec((1, w), lambda i: (0, 0))
    full = lambda a: pl.BlockSpec(a.shape, lambda i: (0, 0))
    tab = pl.BlockSpec((3, tm, HEAD_DIM), lambda i: (0, table_of_tile(i), 0))
    zspec = lambda w, blk: pl.BlockSpec((tm, w), lambda i: (i, blk))
    widths = (GQA_HEADS * HEAD_DIM, GQA_KV_HEADS * HEAD_DIM, GQA_KV_HEADS * HEAD_DIM, GQA_KV_HEADS * HEAD_DIM,
              MLA_HEADS * MLA_PAD, MLA_RANK, MLA_HEADS * MLA_PAD, MLA_HEADS * HEAD_DIM)
    dtypes = (BF16, F32, BF16, BF16, BF16, F32, BF16, BF16)
    transposed = (False, False, False, True, False, False, False, True)
    ospec = lambda w, t: pl.BlockSpec((w, tm), lambda i: (0, i)) if t else pl.BlockSpec((tm, w), lambda i: (i, 0))
    return pl.pallas_call(
        _prep_kernel,
        grid=(T // tm,),
        in_specs=[zspec(768, 0), zspec(256, 3), zspec(256, 4), zspec(128, 10), zspec(512, 3), zspec(512, 4),
                  tab, tab, row(128), row(128), row(512), row(512), full(wqb), full(wkvb),
                  row(128), row(128), row(128), row(128)],
        out_specs=[ospec(w, t) for w, t in zip(widths, transposed)],
        out_shape=[jax.ShapeDtypeStruct((w, T) if t else (T, w), d) for w, d, t in zip(widths, dtypes, transposed)],
        compiler_params=_cparams("parallel"),
        name="prep",
    )(z, z, z, z, z, z, tabg, tabm, gqn, gkn, qan, kvan, wqb, wkvb, mqn_n, mqn_r, mkn_n, mkn_r)


def _cache_kernel(ckv_ref, kr_ref, wkvb_ref, gkn_ref, gkr_ref, mk_o, mv_o):
    _mla_kv(ckv_ref[...], kr_ref[...], wkvb_ref, gkn_ref, gkr_ref, None, mk_o, mv_o)


def _cache_expand(ckv, kr, wkvb, mkn_n, mkn_r):
    L, R, _ = ckv.shape
    tm = min(512, R)
    return pl.pallas_call(
        _cache_kernel,
        grid=(L, R // tm),
        in_specs=[pl.BlockSpec((None, tm, MLA_RANK), lambda l, i: (l, i, 0)),
                  pl.BlockSpec((None, tm, HEAD_DIM), lambda l, i: (l, i, 0)),
                  pl.BlockSpec((None,) + wkvb.shape[1:], lambda l, i: (l, 0, 0)),
                  pl.BlockSpec((None, 1, HEAD_DIM), lambda l, i: (l, 0, 0)),
                  pl.BlockSpec((None, 1, HEAD_DIM), lambda l, i: (l, 0, 0))],
        out_specs=[pl.BlockSpec((None, tm, MLA_HEADS * MLA_PAD), lambda l, i: (l, i, 0)),
                   pl.BlockSpec((None, MLA_HEADS * HEAD_DIM, tm), lambda l, i: (l, 0, i))],
        out_shape=[jax.ShapeDtypeStruct((L, R, MLA_HEADS * MLA_PAD), BF16),
                   jax.ShapeDtypeStruct((L, MLA_HEADS * HEAD_DIM, R), BF16)],
        compiler_params=_cparams("parallel", "parallel"),
        name="cache_expand",
    )(ckv, kr, wkvb, mkn_n, mkn_r)


def _attn_kernel(q_ref, k_ref, vt_ref, o_ref, m_sc, l_sc, acc_sc, *, heads, group, dq, dv):
    ki = pl.program_id(3)

    @pl.when(ki == 0)
    def _():
        m_sc[...] = jnp.full_like(m_sc, -jnp.inf)
        l_sc[...] = jnp.zeros_like(l_sc)
        acc_sc[...] = jnp.zeros_like(acc_sc)

    for h in range(heads):
        kv = h // group
        s = _dot_nt(k_ref[:, kv * dq:(kv + 1) * dq], q_ref[:, h * dq:(h + 1) * dq])
        m_prev = m_sc[h]
        m_new = jnp.maximum(m_prev, jnp.max(s, axis=0, keepdims=True))
        p = jnp.exp2(s - m_new)
        alpha = jnp.exp2(m_prev - m_new)
        l_sc[h] = alpha * l_sc[h] + jnp.sum(p, axis=0, keepdims=True)
        acc_sc[h] = alpha * acc_sc[h] + _dot(vt_ref[kv * dv:(kv + 1) * dv, :], p.astype(BF16))
        m_sc[h] = m_new

    @pl.when(ki == pl.num_programs(3) - 1)
    def _():
        for h in range(heads):
            o_ref[:, h * dv:(h + 1) * dv] = (acc_sc[h] / l_sc[h]).T.astype(o_ref.dtype)


def _attention(q, k, vt, *, batch, nq, nk, q_row0, k_row0, heads, group, dq, dv, tq, tk, hb):
    nqb, nkb = nq // tq, nk // tk
    qb0, kb0 = q_row0 // tq, k_row0 // tk
    kvb = max(1, hb // group)
    return pl.pallas_call(
        functools.partial(_attn_kernel, heads=hb, group=group, dq=dq, dv=dv),
        grid=(batch, heads // hb, nqb, nkb),
        in_specs=[pl.BlockSpec((tq, hb * dq), lambda b, h, i, j: (qb0 + b * nqb + i, h)),
                  pl.BlockSpec((tk, kvb * dq), lambda b, h, i, j: (kb0 + b * nkb + j, h)),
                  pl.BlockSpec((kvb * dv, tk), lambda b, h, i, j: (h, kb0 + b * nkb + j))],
        out_specs=pl.BlockSpec((tq, hb * dv), lambda b, h, i, j: (b * nqb + i, h)),
        out_shape=jax.ShapeDtypeStruct((batch * nq, heads * dv), BF16),
        scratch_shapes=[pltpu.VMEM((hb, 1, tq), F32), pltpu.VMEM((hb, 1, tq), F32), pltpu.VMEM((hb, dv, tq), F32)],
        compiler_params=_cparams("parallel", "parallel", "parallel", "arbitrary"),
        name="attention",
    )(q, k, vt)


def _hgrn_constants(c):
    levels = int(np.log2(c))
    nr = (levels + 2) * c
    L = np.zeros((2, nr, c), np.float32)
    M = np.zeros((2, levels, c, c), np.float32)
    for t in range(c):
        L[0, t, :t + 1] = 1.0
        L[0, c + t, t + 1:] = 1.0
    for lv in range(levels):
        m = c >> (lv + 1)
        for a in range(0, c, 2 * m):
            mid = a + m
            for t in range(mid, a + 2 * m):
                L[0, (2 + lv) * c + t, mid:t + 1] = 1.0
                M[0, lv, t, a:mid] = 1.0
            for s in range(a, mid):
                L[0, (2 + lv) * c + s, s + 1:mid] = 1.0
    Lb = L[0].reshape(levels + 2, c, c)[:, ::-1, ::-1]
    L[1] = Lb.reshape(nr, c)
    M[1] = M[0][:, ::-1, ::-1]
    return L, M


def _hgrn_kernel(q_ref, z_ref, v_ref, lb_ref, s0_ref, L_ref, M_ref, o_ref, sf_ref, s_sc):
    c = q_ref.shape[0]
    levels = M_ref.shape[0]

    @pl.when(pl.program_id(2) == 0)
    def _():
        s_sc[...] = s0_ref[...]

    z = z_ref[...]
    e = jnp.exp(-jnp.abs(z))
    r = 1.0 / (1.0 + e)
    logsig = jnp.minimum(z, 0.0) + jnp.log(r)
    a = lb_ref[0:1, :]
    cc = lb_ref[1:2, :] + logsig
    logf = jnp.maximum(a, cc) + jnp.log(1.0 + jnp.exp(-jnp.abs(a - cc)))
    kk = lb_ref[2:3, :] * jnp.where(z >= 0.0, e * r, r)

    lmat = L_ref[...]
    p1, p2, p3 = _split3(logf)
    ex = jnp.exp(_dot(lmat, p1) + _dot(lmat, p2) + _dot(lmat, p3))
    ones = jnp.ones((c, HEAD_DIM), BF16)
    for h in range(HG_HEADS):
        sl = slice(h * HEAD_DIM, (h + 1) * HEAD_DIM)
        q, k, v = q_ref[:, sl], kk[:, sl], v_ref[:, sl]
        vb = v.astype(BF16)
        amat = jnp.zeros((c, c), F32)
        for lv in range(levels):
            el = ex[(2 + lv) * c:(3 + lv) * c, sl]
            amat = amat + M_ref[lv] * _dot_nt((q * el).astype(BF16), (k * el).astype(BF16))
        s_prev = s_sc[h]
        o = (_dot(amat.astype(BF16), vb) + jnp.sum(q * k, axis=-1, keepdims=True) * v
             + _dot((q * ex[0:c, sl]).astype(BF16), s_prev.astype(BF16)))
        o_ref[:, sl] = o
        dcol = jnp.exp(_dot_tn(p1[:, sl], ones) + _dot_tn(p2[:, sl], ones) + _dot_tn(p3[:, sl], ones))
        s_new = dcol * s_prev + _dot_tn((k * ex[c:2 * c, sl]).astype(BF16), vb)
        s_sc[h] = s_new
        sf_ref[h] = s_new


def _hgrn(z, lbtab, s0, *, batch, n, row0):
    c = min(HG_CHUNK, n)
    nc = n // c
    cb0 = row0 // c
    Lm, Mm = _hgrn_constants(c)
    Lm, Mm = jnp.asarray(Lm, BF16), jnp.asarray(Mm, F32)
    w = HG_HEADS * HEAD_DIM

    def rows(d, b, j):
        return cb0 + b * nc + j + d * (nc - 1 - 2 * j)

    return pl.pallas_call(
        _hgrn_kernel,
        grid=(2, batch, nc),
        in_specs=[pl.BlockSpec((c, w), lambda d, b, j: (rows(d, b, j), 5)),
                  pl.BlockSpec((c, w), lambda d, b, j: (rows(d, b, j), 6 + d)),
                  pl.BlockSpec((c, w), lambda d, b, j: (rows(d, b, j), 8)),
                  pl.BlockSpec((None, 8, w), lambda d, b, j: (d, 0, 0)),
                  pl.BlockSpec((None, None, HG_HEADS, HEAD_DIM, HEAD_DIM), lambda d, b, j: (b, d, 0, 0, 0)),
                  pl.BlockSpec((None,) + Lm.shape[1:], lambda d, b, j: (d, 0, 0)),
                  pl.BlockSpec((None,) + Mm.shape[1:], lambda d, b, j: (d, 0, 0, 0))],
        out_specs=[pl.BlockSpec((None, c, w), lambda d, b, j: (d, rows(d, b, j) - cb0, 0)),
                   pl.BlockSpec((None, None, HG_HEADS, HEAD_DIM, HEAD_DIM), lambda d, b, j: (b, d, 0, 0, 0))],
        out_shape=[jax.ShapeDtypeStruct((2, batch * n, w), F32),
                   jax.ShapeDtypeStruct((batch, 2, HG_HEADS, HEAD_DIM, HEAD_DIM), F32)],
        scratch_shapes=[pltpu.VMEM((HG_HEADS, HEAD_DIM, HEAD_DIM), F32)],
        compiler_params=_cparams("parallel", "parallel", "arbitrary"),
        name="hgrn_scan",
    )(z, z, z, lbtab, s0, Lm, Mm)


def _out_kernel(x_ref, ga_ref, of_ref, ob_ref, hg_ref, ma_ref, w_ref, on_ref, g2_ref,
                g1_ref, sh2_ref, sc2_ref, xo_ref, hhi_ref, hlo_ref):
    o = of_ref[...] + ob_ref[...]
    g = hg_ref[...]
    g = g * (1.0 / (1.0 + jnp.exp(-g)))
    n_g, n_h = GQA_HEADS * HEAD_DIM, HG_HEADS * HEAD_DIM
    mix = _dot(ga_ref[...], w_ref[0:n_g, :]) + _dot(ma_ref[...], w_ref[n_g + n_h:, :])
    for h in range(HG_HEADS):
        sl = slice(h * HEAD_DIM, (h + 1) * HEAD_DIM)
        oh = _rms(o[:, sl], HEAD_DIM) * on_ref[...] * g[:, sl]
        mix = mix + _dot(oh.astype(BF16), w_ref[n_g + h * HEAD_DIM:n_g + (h + 1) * HEAD_DIM, :])
    x = x_ref[...] + g1_ref[...] * mix
    xo_ref[...] = x
    h2 = _rms(x, x.shape[-1]) * g2_ref[...] * (1.0 + sc2_ref[...]) + sh2_ref[...]
    hi, lo = _split2(h2)
    hhi_ref[...] = hi
    hlo_ref[...] = lo


def _mixer_out(x, ga, o2, z, ma, w, on, g2, modv, mod_row_of_tile, tm):
    T, D = x.shape
    wh = HG_HEADS * HEAD_DIM

    def mspec(k):
        return pl.BlockSpec((None, 1, D), lambda i: (mod_row_of_tile(i) + k, 0, 0))

    return pl.pallas_call(
        _out_kernel,
        grid=(T // tm,),
        in_specs=[pl.BlockSpec((tm, D), lambda i: (i, 0)),
                  pl.BlockSpec((tm, ga.shape[1]), lambda i: (i, 0)),
                  pl.BlockSpec((None, tm, wh), lambda i: (0, i, 0)),
                  pl.BlockSpec((None, tm, wh), lambda i: (1, i, 0)),
                  pl.BlockSpec((tm, wh), lambda i: (i, 9)),
                  pl.BlockSpec((tm, ma.shape[1]), lambda i: (i, 0)),
                  pl.BlockSpec(w.shape, lambda i: (0, 0)),
                  pl.BlockSpec((1, HEAD_DIM), lambda i: (0, 0)),
                  pl.BlockSpec((1, D), lambda i: (0, 0)),
                  mspec(2), mspec(3), mspec(4)],
        out_specs=[pl.BlockSpec((tm, D), lambda i: (i, 0))] * 3,
        out_shape=[jax.ShapeDtypeStruct((T, D), F32), jax.ShapeDtypeStruct((T, D), BF16),
                   jax.ShapeDtypeStruct((T, D), BF16)],
        compiler_params=_cparams("parallel"),
        name="mixer_out",
    )(x, ga, o2, o2, z, ma, w, on, g2, modv, modv, modv)


def _ws_kernel(k_ref, wq_ref, hi_ref, lo_ref):
    khi, klo = _split2(k_ref[...])
    whi, wlo = _split2(wq_ref[...])
    acc = _dot_nt(khi, whi) + _dot_nt(klo, whi) + _dot_nt(khi, wlo)
    hi, lo = _split2(acc * LOG2E)
    hi_ref[...] = hi
    lo_ref[...] = lo


def _score_weights(keys, wq):
    L, D, _ = wq.shape
    nb = 2 * PEER_HEADS
    sub = keys.shape[-1]
    kk = keys.reshape(L, nb, PEER_KEYS, sub)
    out = jax.ShapeDtypeStruct((L, nb * PEER_KEYS, D), BF16)
    ospec = pl.BlockSpec((None, PEER_KEYS, D), lambda l, b: (l, (b % 2) * PEER_HEADS + b // 2, 0))
    return pl.pallas_call(
        _ws_kernel,
        grid=(L, nb),
        in_specs=[pl.BlockSpec((None, None, PEER_KEYS, sub), lambda l, b: (l, b, 0, 0)),
                  pl.BlockSpec((None, D, sub), lambda l, b: (l, 0, b))],
        out_specs=[ospec, ospec],
        out_shape=[out, out],
        compiler_params=_cparams("parallel", "parallel"),
        name="score_weights",
    )(kk, wq)


def _score_kernel(whi_ref, wlo_ref, hhi_ref, hlo_ref, o_ref):
    whi, hhi = whi_ref[...], hhi_ref[...]
    o_ref[...] = _dot_nt(whi, hhi) + _dot_nt(whi, hlo_ref[...]) + _dot_nt(wlo_ref[...], hhi)


def _scores(whi, wlo, hhi, hlo, tm):
    R, D = whi.shape
    T = hhi.shape[0]
    tr = min(512, R)
    return pl.pallas_call(
        _score_kernel,
        grid=(R // tr, T // tm),
        in_specs=[pl.BlockSpec((tr, D), lambda r, i: (r, 0)), pl.BlockSpec((tr, D), lambda r, i: (r, 0)),
                  pl.BlockSpec((tm, D), lambda r, i: (i, 0)), pl.BlockSpec((tm, D), lambda r, i: (i, 0))],
        out_specs=pl.BlockSpec((tr, tm), lambda r, i: (r, i)),
        out_shape=jax.ShapeDtypeStruct((R, T), F32),
        compiler_params=_cparams("parallel", "parallel"),
        name="scores",
    )(whi, wlo, hhi, hlo)


def _batcher_pairs(n):
    pairs = []
    p = 1
    while p < n:
        k = p
        while k >= 1:
            for j in range(k % p, n - k, 2 * k):
                for i in range(min(k, n - j - k)):
                    if (i + j) // (2 * p) == (i + j + k) // (2 * p):
                        pairs.append((i + j, i + j + k))
            k //= 2
        p *= 2
    return pairs


_SORT16 = _batcher_pairs(PEER_TOPK)


def _sort_desc(v):
    v = list(v)
    for i, j in _SORT16:
        v[i], v[j] = jnp.maximum(v[i], v[j]), jnp.minimum(v[i], v[j])
    return v


def _merge_top(a, b):
    n = len(a)
    v = [jnp.maximum(a[i], b[n - 1 - i]) for i in range(n)]
    s = n // 2
    while s >= 1:
        for i in range(n):
            if (i & s) == 0:
                v[i], v[i + s] = jnp.maximum(v[i], v[i + s]), jnp.minimum(v[i], v[i + s])
        s //= 2
    return v


def _top16_of_keys(s_ref, row0, lanes):
    k = PEER_TOPK
    groups = [_sort_desc([s_ref[pl.ds(row0 + PEER_HEADS * (k * g + i), PEER_HEADS), lanes] for i in range(k)])
              for g in range(PEER_KEYS // k)]
    while len(groups) > 1:
        groups = [_merge_top(groups[i], groups[i + 1]) for i in range(0, len(groups), 2)]
    return groups[0]


def _topk_kernel(s_ref, thr_ref, s1n_ref):
    k = PEER_TOPK
    half = PEER_HEADS * PEER_KEYS
    ninf = jnp.full((PEER_HEADS, 128), -jnp.inf, F32)

    def chunk(ch, carry):
        lanes = pl.ds(pl.multiple_of(ch * 128, 128), 128)
        a = _top16_of_keys(s_ref, 0, lanes)
        b = _top16_of_keys(s_ref, half, lanes)
        cand = [[a[i] + b[j] for j in range(k // (i + 1))] for i in range(k)]
        g0 = cand[0]
        g1 = _sort_desc(cand[1] + cand[2] + cand[5] + cand[8])
        g2 = _sort_desc(cand[3] + cand[4] + cand[6] + cand[7] + cand[9] + cand[10] + cand[11] + cand[12] + cand[13])
        g3 = cand[14] + cand[15] + [ninf] * (k - 2)
        top = _merge_top(_merge_top(g0, g1), _merge_top(g2, g3))
        zsum = jnp.zeros((PEER_HEADS, 128), F32)
        for t in top:
            zsum = zsum + jnp.exp2(t - top[0])
        c0 = top[0] + jnp.log2(zsum)
        thr_ref[:, lanes] = top[k - 1] - c0
        for key in range(PEER_KEYS):
            tile = s_ref[pl.ds(half + PEER_HEADS * key, PEER_HEADS), lanes] - c0
            s1n_ref[ch, pl.ds(key, PEER_HEADS, stride=PEER_KEYS), :] = tile
        return carry

    lax.fori_loop(0, s_ref.shape[1] // 128, chunk, 0)


def _topk(st, tm):
    R, T = st.shape
    return pl.pallas_call(
        _topk_kernel,
        grid=(T // tm,),
        in_specs=[pl.BlockSpec((R, tm), lambda i: (0, i))],
        out_specs=[pl.BlockSpec((PEER_HEADS, tm), lambda i: (0, i)),
                   pl.BlockSpec((tm // 128, R // 2, 128), lambda i: (i, 0, 0))],
        out_shape=[jax.ShapeDtypeStruct((PEER_HEADS, T), F32), jax.ShapeDtypeStruct((T // 128, R // 2, 128), F32)],
        compiler_params=_cparams("parallel"),
        name="topk",
    )(st)


def _peer_kernel(h_ref, s0_ref, s1n_ref, thr_ref, u_ref, vt_ref, x_ref, g2_ref, o_ref,
                 pre_a, pre_b, c_a, c_b, acc_sc):
    s = pl.program_id(1)
    te, tm = c_a.shape
    nk = PEER_KEYS

    @pl.when(s == 0)
    def _():
        acc_sc[...] = jnp.zeros_like(acc_sc)
        pre_b[...] = jnp.zeros_like(pre_b)
        c_a[...] = jnp.zeros_like(c_a)

    d = h_ref.shape[1]
    nblk = (te // nk) * (tm // 128)
    nq = nblk // 4
    th, kq, mq = tm // 2, d // nq, d // nq

    def pre_piece(p, pre_w):
        half, q = divmod(p, nq)
        part = _dot_nt(u_ref[:, q * kq:(q + 1) * kq], h_ref[half * th:(half + 1) * th, q * kq:(q + 1) * kq])
        if q == 0:
            pre_w[:, half * th:(half + 1) * th] = part
        else:
            pre_w[:, half * th:(half + 1) * th] += part

    def value_piece(p, c_r):
        half, q = divmod(p, nq)
        acc_sc[q * mq:(q + 1) * mq, half * th:(half + 1) * th] += _dot(
            vt_ref[q * mq:(q + 1) * mq, :], c_r[:, half * th:(half + 1) * th])

    def stages(pre_w, pre_r, c_w, c_r):
        for blk in range(nblk):
            if blk % 2 == 0:
                pre_piece(blk // 2, pre_w)
            else:
                value_piece(blk // 2, c_r)
            ii, ch = divmod(blk, tm // 128)
            lanes = slice(ch * 128, (ch + 1) * 128)
            gate = jnp.zeros((nk, 128), F32)
            for h in range(PEER_HEADS):
                r0 = ii * PEER_HEADS + h
                val = s1n_ref[ch, h * nk:(h + 1) * nk, :] + s0_ref[r0:r0 + 1, lanes]
                gate = gate + jnp.where(val >= thr_ref[h:h + 1, lanes], jnp.exp2(val), 0.0)
            pre = pre_r[ii * nk:(ii + 1) * nk, lanes]
            act = 0.5 * pre * (1.0 + lax.erf(pre * (2.0 ** -0.5)))
            c_w[ii * nk:(ii + 1) * nk, lanes] = (gate * act).astype(BF16)

    @pl.when(s % 2 == 0)
    def _():
        stages(pre_a, pre_b, c_b, c_a)

    @pl.when(s % 2 == 1)
    def _():
        stages(pre_b, pre_a, c_a, c_b)

    @pl.when(s == pl.num_programs(1) - 1)
    def _():
        o_ref[...] = x_ref[...] + g2_ref[...] * acc_sc[...].T


def _peer(hb, st, thr, s1n, u, vt, x, modv, mod_row_of_tile, tm):
    T, D = x.shape
    E = u.shape[0]
    te = 512
    ne = E // te
    half = PEER_HEADS * PEER_KEYS
    tile = lambda s, lag: jnp.clip(s - lag, 0, ne - 1)
    return pl.pallas_call(
        _peer_kernel,
        grid=(T // tm, ne + 2),
        in_specs=[pl.BlockSpec((tm, D), lambda i, s: (i, 0)),
                  pl.BlockSpec((te // PEER_KEYS * PEER_HEADS, tm), lambda i, s: (tile(s, 1), i)),
                  pl.BlockSpec((tm // 128, half, 128), lambda i, s: (i, 0, 0)),
                  pl.BlockSpec((PEER_HEADS, tm), lambda i, s: (0, i)),
                  pl.BlockSpec((te, D), lambda i, s: (tile(s, 0), 0)),
                  pl.BlockSpec((D, te), lambda i, s: (0, tile(s, 2))),
                  pl.BlockSpec((tm, D), lambda i, s: (i, 0)),
                  pl.BlockSpec((None, 1, D), lambda i, s: (mod_row_of_tile(i) + 5, 0, 0))],
        out_specs=pl.BlockSpec((tm, D), lambda i, s: (i, 0)),
        out_shape=jax.ShapeDtypeStruct((T, D), F32),
        scratch_shapes=[pltpu.VMEM((te, tm), F32), pltpu.VMEM((te, tm), F32), pltpu.VMEM((te, tm), BF16),
                        pltpu.VMEM((te, tm), BF16), pltpu.VMEM((D, tm), F32)],
        compiler_params=_cparams("parallel", "arbitrary"),
        name="peer_experts",
    )(hb, st, s1n, thr, u, vt, x, modv)


def _rope_tables(n, ident_rows):
    pos = jnp.arange(n, dtype=jnp.int32)
    row, col = pos // GRID_W, pos % GRID_W
    lane = jnp.arange(HEAD_DIM)

    def build(nfreq, width):
        inv = ROPE_THETA ** (-jnp.arange(nfreq, dtype=F32) / nfreq)
        ar = row.astype(F32)[:, None] * inv[None, :]
        ac = col.astype(F32)[:, None] * inv[None, :]
        cos = jnp.concatenate([jnp.cos(ar), jnp.cos(ar), jnp.cos(ac), jnp.cos(ac)], axis=-1)
        sin = jnp.concatenate([jnp.sin(ar), jnp.sin(ar), jnp.sin(ac), jnp.sin(ac)], axis=-1)
        pad = HEAD_DIM - width
        cos = jnp.pad(cos, ((0, 0), (0, pad)))
        sin = jnp.pad(sin, ((0, 0), (0, pad)))
        first = (lane % (2 * nfreq)) < nfreq
        tab = jnp.stack([cos, jnp.where(first, 0.0, sin), jnp.where(first, -sin, 0.0)])
        ident = jnp.stack([jnp.ones((ident_rows, HEAD_DIM), F32), jnp.zeros((ident_rows, HEAD_DIM), F32),
                           jnp.zeros((ident_rows, HEAD_DIM), F32)])
        return jnp.concatenate([ident, tab], axis=1)

    return build(HEAD_DIM // 4, HEAD_DIM), build(MLA_ROPE // 4, MLA_ROPE)


def kernel(x_prompt, x_sample, c, cache_gqa_k, cache_gqa_v, cache_mla_ckv, cache_mla_krope, state_hgrn, c_ctx,
           w_mod, b_mod, norm1_g, norm2_g, w_in, gqa_qn, gqa_kn, mla_qa_n, mla_kva_n, w_mla_qb, w_mla_kvb,
           mla_qn, mla_kn, hg_lb_logits, hg_on, w_out, peer_wq, peer_keys, peer_u, peer_v):
    B, S, D = x_prompt.shape
    DB, DS, _ = x_sample.shape
    L = w_mod.shape[0]
    P = cache_gqa_k.shape[2]
    TC, TL = B * S, DB * DS
    T = TC + TL
    tm = 512 if (TC % 512 == 0 and DS % 512 == 0) else 128
    tm_small = min(256, tm)

    ngroups = 1 + DB

    def mod_row_of(t, layer):
        def f(i):
            group = jnp.where(i < TC // t, 0, 1 + (i - TC // t) // (DS // t))
            return (layer * ngroups + group) * 6
        return f

    def table_of(t):
        def f(i):
            return jnp.where(i < TC // t, 0, 1 + (i - TC // t) % (DS // t))
        return f

    cond8 = jnp.zeros((8, D), F32).at[0].set(c_ctx).at[1:1 + DB].set(c)
    modv = _modulation(cond8, w_mod, b_mod)[:, :ngroups].reshape(L * ngroups * 6, 1, D)

    lb_cum = jnp.cumsum(jax.nn.softmax(hg_lb_logits.astype(F32), axis=0), axis=0)
    lbs = jnp.maximum(lb_cum - lb_cum[:1], 0.0)
    lbtab = jnp.stack([jnp.log(lbs), jnp.log1p(-lbs), 1.0 - lbs] + [jnp.zeros_like(lbs)] * 5, axis=2)

    sizes = np.cumsum([0, 768, 256, 256, 512, 512, 512, 512, 512, 512, 512, 64])
    part = lambda a, k: a[..., sizes[k]:sizes[k + 1]]
    order = (0, 1, 2, 10, 8, 9, 3, 4, 5, 6, 7)
    cols = []
    for k in order:
        cols.append(part(w_in, k))
        if k == 10:
            cols.append(jnp.zeros((L, D, 256 - MLA_ROPE), w_in.dtype))
    w_in_b = jnp.concatenate(cols, axis=-1).astype(BF16)

    wqb = w_mla_qb.reshape(L, MLA_RANK, MLA_HEADS, MLA_QK)
    wqb = jnp.pad(wqb, ((0, 0), (0, 0), (0, 0), (0, MLA_PAD - MLA_QK))).reshape(L, MLA_RANK, MLA_HEADS * MLA_PAD).astype(BF16)
    wkvb = w_mla_kvb.reshape(L, MLA_RANK, MLA_HEADS, 2, HEAD_DIM)
    wkvb = jnp.moveaxis(wkvb, 3, 2).reshape(L, MLA_RANK, 2 * MLA_HEADS * HEAD_DIM).astype(BF16)
    rpad = lambda g: jnp.pad(g[:, HEAD_DIM:], ((0, 0), (0, HEAD_DIM - MLA_ROPE)))[:, None, :]
    mqn_n, mqn_r = mla_qn[:, None, :HEAD_DIM], rpad(mla_qn)
    mkn_n, mkn_r = mla_kn[:, None, :HEAD_DIM], rpad(mla_kn)
    w_out_b = w_out.astype(BF16)
    u_b = peer_u.astype(BF16)
    vt_b = jnp.swapaxes(peer_v, 1, 2).astype(BF16)
    def key_major(w):
        return jnp.swapaxes(w.reshape(L, 2, PEER_HEADS, PEER_KEYS, D), 2, 3).reshape(L, 2 * PEER_HEADS * PEER_KEYS, D)

    ws_hi, ws_lo = map(key_major, _score_weights(peer_keys, peer_wq))

    tabg, tabm = _rope_tables(DS, tm_small)

    cck = jnp.moveaxis(cache_mla_ckv, 1, 0).reshape(L, DB * P, MLA_RANK)
    ckr = jnp.pad(jnp.moveaxis(cache_mla_krope, 1, 0).reshape(L, DB * P, MLA_ROPE), ((0, 0), (0, 0), (0, HEAD_DIM - MLA_ROPE)))
    cmk, cmv = _cache_expand(cck, ckr, wkvb, mkn_n, mkn_r)
    cgk = jnp.moveaxis(cache_gqa_k, 1, 0).reshape(L, DB, P, -1).astype(BF16)
    cgv = jnp.transpose(cache_gqa_v.reshape(DB, L, P, -1), (1, 3, 0, 2)).astype(BF16)
    cmk = cmk.reshape(L, DB, P, -1)
    cmv = cmv.reshape(L, -1, DB, P)
    s0_ctx = jnp.zeros((B, 2, HG_HEADS, HEAD_DIM, HEAD_DIM), F32)

    x = jnp.concatenate([x_prompt.reshape(TC, D), x_sample.reshape(TL, D)], axis=0)
    new_k, new_v, new_ckv, new_kr, new_s = [], [], [], [], []
    tq_c = min(256, S)
    tq_l, tk_l = min(1024, DS), min(512, P, DS)
    for l in range(L):
        z = _in_proj(x, norm1_g[l][None], modv, w_in_b[l], mod_row_of(tm, l), tm)
        gq, gkn, gkr, gvt, mq, ckv, mk, mvt = _prep(
            z, tabg, tabm, table_of(tm_small), tm_small, gqa_qn[l][None], gqa_kn[l][None], mla_qa_n[l][None],
            mla_kva_n[l][None], wqb[l], wkvb[l], mqn_n[l], mqn_r[l], mkn_n[l], mkn_r[l])
        new_k.append(gkn[:TC].reshape(B, S, GQA_KV_HEADS, HEAD_DIM))
        new_v.append(z[:TC, 1024:1280].reshape(B, S, GQA_KV_HEADS, HEAD_DIM))
        new_ckv.append(ckv[:TC].reshape(B, S, MLA_RANK))
        new_kr.append(z[:TC, 1280:1280 + MLA_ROPE].reshape(B, S, MLA_ROPE))

        def lat_keys(cache, new):
            return jnp.concatenate([cache, new[TC:].reshape(DB, DS, -1)], axis=1).reshape(DB * (P + DS), -1)

        def lat_values(cache_t, new_t):
            w = new_t.shape[0]
            return jnp.concatenate([cache_t, new_t[:, TC:].reshape(w, DB, DS)], axis=2).reshape(w, DB * (P + DS))

        gqa = dict(heads=GQA_HEADS, group=GQA_HEADS // GQA_KV_HEADS, dq=HEAD_DIM, dv=HEAD_DIM, hb=GQA_HEADS // GQA_KV_HEADS)
        mla = dict(heads=MLA_HEADS, group=1, dq=MLA_PAD, dv=HEAD_DIM, hb=MLA_HEADS // 2)
        ctx_a = dict(batch=B, nq=S, nk=S, q_row0=0, k_row0=0, tq=tq_c, tk=tq_c)
        lat_a = dict(batch=DB, nq=DS, nk=P + DS, q_row0=TC, k_row0=0, tq=tq_l, tk=tk_l)
        ga = jnp.concatenate([_attention(gq, gkr, gvt, **gqa, **ctx_a),
                              _attention(gq, lat_keys(cgk[l], gkr), lat_values(cgv[l], gvt), **gqa, **lat_a)], axis=0)
        ma = jnp.concatenate([_attention(mq, mk, mvt, **mla, **ctx_a),
                              _attention(mq, lat_keys(cmk[l], mk), lat_values(cmv[l], mvt), **mla, **lat_a)], axis=0)

        o_c, s_c = _hgrn(z, lbtab[l], s0_ctx, batch=B, n=S, row0=0)
        o_l, _ = _hgrn(z, lbtab[l], state_hgrn[:, l].astype(F32), batch=DB, n=DS, row0=TC)
        new_s.append(s_c)
        o2 = jnp.concatenate([o_c, o_l], axis=1)

        x, hhi, hlo = _mixer_out(x, ga, o2, z, ma, w_out_b[l], hg_on[l][None], norm2_g[l][None], modv,
                                 mod_row_of(tm_small, l), tm_small)
        st = _scores(ws_hi[l], ws_lo[l], hhi, hlo, tm)
        thr, s1n = _topk(st, tm)
        x = _peer(hhi, st, thr, s1n, u_b[l], vt_b[l], x, modv, mod_row_of(tm, l), tm)

    return (x[:TC].reshape(B, S, D), x[TC:].reshape(DB, DS, D),
            jnp.stack(new_k, axis=1), jnp.stack(new_v, axis=1), jnp.stack(new_ckv, axis=1),
            jnp.stack(new_kr, axis=1), jnp.stack(new_s, axis=1))
```

```python
import functools

import numpy as np
import jax
import jax.numpy as jnp
from jax import lax
from jax.experimental import pallas as pl
from jax.experimental.pallas import tpu as pltpu

F32 = jnp.float32
BF16 = jnp.bfloat16

EPS = 1e-6
LOG2E = 1.4426950408889634
ROPE_THETA = 10000.0
GRID_W = 64

HEAD_DIM = 128
GQA_HEADS, GQA_KV_HEADS = 6, 2
HG_HEADS = 4
MLA_HEADS, MLA_ROPE, MLA_RANK = 6, 64, 512
MLA_QK = HEAD_DIM + MLA_ROPE
MLA_PAD = 2 * HEAD_DIM
PEER_HEADS, PEER_KEYS, PEER_TOPK = 8, 128, 16
HG_CHUNK = 128
PEER_EXPERT_TILE = 1024

VMEM_LIMIT = 60 * 1024 * 1024


def _cparams(*sem):
    return pltpu.CompilerParams(dimension_semantics=sem, vmem_limit_bytes=VMEM_LIMIT)


def _dot(a, b):
    return jnp.dot(a, b, preferred_element_type=F32)


def _dot_nt(a, b):
    return lax.dot_general(a, b, (((1,), (1,)), ((), ())), preferred_element_type=F32)


def _dot_tn(a, b):
    return lax.dot_general(a, b, (((0,), (0,)), ((), ())), preferred_element_type=F32)


def _split2(a):
    hi = a.astype(BF16)
    lo = (a - hi.astype(F32)).astype(BF16)
    return hi, lo


def _split3(a):
    p1 = a.astype(BF16)
    r = a - p1.astype(F32)
    p2 = r.astype(BF16)
    p3 = (r - p2.astype(F32)).astype(BF16)
    return p1, p2, p3


def _rms(x, n):
    return x * lax.rsqrt(jnp.sum(x * x, axis=-1, keepdims=True) * (1.0 / n) + EPS)


def _mod_kernel(cond_ref, w_ref, b_ref, o_ref):
    c = cond_ref[...]
    s = c * (1.0 / (1.0 + jnp.exp(-c)))
    shi, slo = _split2(s)
    whi, wlo = _split2(w_ref[...])
    o_ref[...] = _dot(shi, whi) + _dot(slo, whi) + _dot(shi, wlo) + b_ref[...]


def _modulation(cond8, w_mod, b_mod):
    L, D, N = w_mod.shape
    tn = min(512, N)
    return pl.pallas_call(
        _mod_kernel,
        grid=(L, N // tn),
        in_specs=[pl.BlockSpec((8, D), lambda l, j: (0, 0)),
                  pl.BlockSpec((None, D, tn), lambda l, j: (l, 0, j)),
                  pl.BlockSpec((None, 1, tn), lambda l, j: (l, 0, j))],
        out_specs=pl.BlockSpec((None, 8, tn), lambda l, j: (l, 0, j)),
        out_shape=jax.ShapeDtypeStruct((L, 8, N), F32),
        compiler_params=_cparams("parallel", "parallel"),
        name="modulation",
    )(cond8, w_mod, b_mod.reshape(L, 1, N))


def _in_kernel(x_ref, g_ref, sh_ref, sc_ref, w_ref, o_ref, h_sc):
    @pl.when(pl.program_id(1) == 0)
    def _():
        x = x_ref[...]
        h = _rms(x, x.shape[-1]) * g_ref[...] * (1.0 + sc_ref[...]) + sh_ref[...]
        h_sc[...] = h.astype(BF16)

    o_ref[...] = _dot(h_sc[...], w_ref[...])


def _in_proj(x, g, modv, w, mod_row_of_tile, tm):
    T, D = x.shape
    N = w.shape[1]
    tn = min(1024, N)

    def mspec(k):
        return pl.BlockSpec((None, 1, D), lambda i, j: (mod_row_of_tile(i) + k, 0, 0))

    return pl.pallas_call(
        _in_kernel,
        grid=(T // tm, N // tn),
        in_specs=[pl.BlockSpec((tm, D), lambda i, j: (i, 0)),
                  pl.BlockSpec((1, D), lambda i, j: (0, 0)),
                  mspec(0), mspec(1),
                  pl.BlockSpec((D, tn), lambda i, j: (0, j))],
        out_specs=pl.BlockSpec((tm, tn), lambda i, j: (i, j)),
        out_shape=jax.ShapeDtypeStruct((T, N), F32),
        scratch_shapes=[pltpu.VMEM((tm, D), BF16)],
        compiler_params=_cparams("parallel", "arbitrary"),
        name="in_proj",
    )(x, g, modv, modv, w)


def _rope(x, tab_ref, blk):
    return (x * tab_ref[0] + pltpu.roll(x, blk, 1) * tab_ref[1]
            + pltpu.roll(x, HEAD_DIM - blk, 1) * tab_ref[2])


def _mla_kv(ckv, krope, wkvb_ref, gkn_ref, gkr_ref, tab_ref, mk_ref, mv_ref):
    kv = _dot(ckv.astype(BF16), wkvb_ref[...])
    nv = MLA_HEADS * HEAD_DIM
    mv_ref[...] = kv[:, nv:].T.astype(BF16)
    ssr = jnp.sum(krope * krope, axis=-1, keepdims=True)
    for h in range(MLA_HEADS):
        kn = kv[:, h * HEAD_DIM:(h + 1) * HEAD_DIM]
        rs = lax.rsqrt((jnp.sum(kn * kn, axis=-1, keepdims=True) + ssr) * (1.0 / MLA_QK) + EPS)
        kr = krope * rs * gkr_ref[...]
        if tab_ref is not None:
            kr = _rope(kr, tab_ref, MLA_ROPE // 4)
        mk_ref[:, h * MLA_PAD:h * MLA_PAD + HEAD_DIM] = (kn * rs * gkn_ref[...]).astype(BF16)
        mk_ref[:, h * MLA_PAD + HEAD_DIM:(h + 1) * MLA_PAD] = kr.astype(BF16)


def _prep_kernel(gq_ref, gk_ref, gv_ref, mkr_ref, mqa_ref, mkva_ref, tg_ref, tm_ref,
                 gqn_ref, gkn_ref, qan_ref, kvan_ref, wqb_ref, wkvb_ref,
                 mqn_n_ref, mqn_r_ref, mkn_n_ref, mkn_r_ref,
                 gq_o, gkn_o, gkr_o, gvt_o, mq_o, ckv_o, mk_o, mv_o):
    gscale = HEAD_DIM ** -0.5 * LOG2E
    for h in range(GQA_HEADS):
        sl = slice(h * HEAD_DIM, (h + 1) * HEAD_DIM)
        q = _rms(gq_ref[:, sl], HEAD_DIM) * gqn_ref[...]
        gq_o[:, sl] = (_rope(q, tg_ref, HEAD_DIM // 4) * gscale).astype(BF16)
    for h in range(GQA_KV_HEADS):
        sl = slice(h * HEAD_DIM, (h + 1) * HEAD_DIM)
        k = _rms(gk_ref[:, sl], HEAD_DIM) * gkn_ref[...]
        gkn_o[:, sl] = k
        gkr_o[:, sl] = _rope(k, tg_ref, HEAD_DIM // 4).astype(BF16)
    gvt_o[...] = gv_ref[...].T.astype(BF16)

    qa = _rms(mqa_ref[...], MLA_RANK) * qan_ref[...]
    mq = _dot(qa.astype(BF16), wqb_ref[...])
    mscale = MLA_QK ** -0.5 * LOG2E
    for h in range(MLA_HEADS):
        qn = mq[:, h * MLA_PAD:h * MLA_PAD + HEAD_DIM]
        qr = mq[:, h * MLA_PAD + HEAD_DIM:(h + 1) * MLA_PAD]
        ss = jnp.sum(qn * qn, axis=-1, keepdims=True) + jnp.sum(qr * qr, axis=-1, keepdims=True)
        rs = lax.rsqrt(ss * (1.0 / MLA_QK) + EPS) * mscale
        mq_o[:, h * MLA_PAD:h * MLA_PAD + HEAD_DIM] = (qn * rs * mqn_n_ref[...]).astype(BF16)
        mq_o[:, h * MLA_PAD + HEAD_DIM:(h + 1) * MLA_PAD] = _rope(qr * rs * mqn_r_ref[...], tm_ref, MLA_ROPE // 4).astype(BF16)

    ckv = _rms(mkva_ref[...], MLA_RANK) * kvan_ref[...]
    ckv_o[...] = ckv
    _mla_kv(ckv, mkr_ref[...], wkvb_ref, mkn_n_ref, mkn_r_ref, tm_ref, mk_o, mv_o)


def _prep(z, tabg, tabm, table_of_tile, tm, gqn, gkn, qan, kvan, wqb, wkvb, mqn_n, mqn_r, mkn_n, mkn_r):
    T = z.shape[0]
    row = lambda w: pl.BlockSpec((1, w), lambda i: (0, 0))
    full = lambda a: pl.BlockSpec(a.shape, lambda i: (0, 0))
    tab = pl.BlockSpec((3, tm, HEAD_DIM), lambda i: (0, table_of_tile(i), 0))
    zspec = lambda w, blk: pl.BlockSpec((tm, w), lambda i: (i, blk))
    widths = (GQA_HEADS * HEAD_DIM, GQA_KV_HEADS * HEAD_DIM, GQA_KV_HEADS * HEAD_DIM, GQA_KV_HEADS * HEAD_DIM,
              MLA_HEADS * MLA_PAD, MLA_RANK, MLA_HEADS * MLA_PAD, MLA_HEADS * HEAD_DIM)
    dtypes = (BF16, F32, BF16, BF16, BF16, F32, BF16, BF16)
    transposed = (False, False, False, True, False, False, False, True)
    ospec = lambda w, t: pl.BlockSpec((w, tm), lambda i: (0, i)) if t else pl.BlockSpec((tm, w), lambda i: (i, 0))
    return pl.pallas_call(
        _prep_kernel,
        grid=(T // tm,),
        in_specs=[zspec(768, 0), zspec(256, 3), zspec(256, 4), zspec(128, 10), zspec(512, 3), zspec(512, 4),
                  tab, tab, row(128), row(128), row(512), row(512), full(wqb), full(wkvb),
                  row(128), row(128), row(128), row(128)],
        out_specs=[ospec(w, t) for w, t in zip(widths, transposed)],
        out_shape=[jax.ShapeDtypeStruct((w, T) if t else (T, w), d) for w, d, t in zip(widths, dtypes, transposed)],
        compiler_params=_cparams("parallel"),
        name="prep",
    )(z, z, z, z, z, z, tabg, tabm, gqn, gkn, qan, kvan, wqb, wkvb, mqn_n, mqn_r, mkn_n, mkn_r)


def _cache_kernel(ckv_ref, kr_ref, wkvb_ref, gkn_ref, gkr_ref, mk_o, mv_o):
    _mla_kv(ckv_ref[...], kr_ref[...], wkvb_ref, gkn_ref, gkr_ref, None, mk_o, mv_o)


def _cache_expand(ckv, kr, wkvb, mkn_n, mkn_r):
    L, R, _ = ckv.shape
    tm = min(512, R)
    return pl.pallas_call(
        _cache_kernel,
        grid=(L, R // tm),
        in_specs=[pl.BlockSpec((None, tm, MLA_RANK), lambda l, i: (l, i, 0)),
                  pl.BlockSpec((None, tm, HEAD_DIM), lambda l, i: (l, i, 0)),
                  pl.BlockSpec((None,) + wkvb.shape[1:], lambda l, i: (l, 0, 0)),
                  pl.BlockSpec((None, 1, HEAD_DIM), lambda l, i: (l, 0, 0)),
                  pl.BlockSpec((None, 1, HEAD_DIM), lambda l, i: (l, 0, 0))],
        out_specs=[pl.BlockSpec((None, tm, MLA_HEADS * MLA_PAD), lambda l, i: (l, i, 0)),
                   pl.BlockSpec((None, MLA_HEADS * HEAD_DIM, tm), lambda l, i: (l, 0, i))],
        out_shape=[jax.ShapeDtypeStruct((L, R, MLA_HEADS * MLA_PAD), BF16),
                   jax.ShapeDtypeStruct((L, MLA_HEADS * HEAD_DIM, R), BF16)],
        compiler_params=_cparams("parallel", "parallel"),
        name="cache_expand",
    )(ckv, kr, wkvb, mkn_n, mkn_r)


def _attn_kernel(q_ref, k_ref, vt_ref, o_ref, m_sc, l_sc, acc_sc, *, heads, group, dq, dv):
    ki = pl.program_id(3)

    @pl.when(ki == 0)
    def _():
        m_sc[...] = jnp.full_like(m_sc, -jnp.inf)
        l_sc[...] = jnp.zeros_like(l_sc)
        acc_sc[...] = jnp.zeros_like(acc_sc)

    for h in range(heads):
        kv = h // group
        s = _dot_nt(k_ref[:, kv * dq:(kv + 1) * dq], q_ref[:, h * dq:(h + 1) * dq])
        m_prev = m_sc[h]
        m_new = jnp.maximum(m_prev, jnp.max(s, axis=0, keepdims=True))
        p = jnp.exp2(s - m_new)
        alpha = jnp.exp2(m_prev - m_new)
        l_sc[h] = alpha * l_sc[h] + jnp.sum(p, axis=0, keepdims=True)
        acc_sc[h] = alpha * acc_sc[h] + _dot(vt_ref[kv * dv:(kv + 1) * dv, :], p.astype(BF16))
        m_sc[h] = m_new

    @pl.when(ki == pl.num_programs(3) - 1)
    def _():
        for h in range(heads):
            o_ref[:, h * dv:(h + 1) * dv] = (acc_sc[h] / l_sc[h]).T.astype(o_ref.dtype)


def _attention(q, k, vt, *, batch, nq, nk, q_row0, k_row0, heads, group, dq, dv, tq, tk, hb):
    nqb, nkb = nq // tq, nk // tk
    qb0, kb0 = q_row0 // tq, k_row0 // tk
    kvb = max(1, hb // group)
    return pl.pallas_call(
        functools.partial(_attn_kernel, heads=hb, group=group, dq=dq, dv=dv),
        grid=(batch, heads // hb, nqb, nkb),
        in_specs=[pl.BlockSpec((tq, hb * dq), lambda b, h, i, j: (qb0 + b * nqb + i, h)),
                  pl.BlockSpec((tk, kvb * dq), lambda b, h, i, j: (kb0 + b * nkb + j, h)),
                  pl.BlockSpec((kvb * dv, tk), lambda b, h, i, j: (h, kb0 + b * nkb + j))],
        out_specs=pl.BlockSpec((tq, hb * dv), lambda b, h, i, j: (b * nqb + i, h)),
        out_shape=jax.ShapeDtypeStruct((batch * nq, heads * dv), BF16),
        scratch_shapes=[pltpu.VMEM((hb, 1, tq), F32), pltpu.VMEM((hb, 1, tq), F32), pltpu.VMEM((hb, dv, tq), F32)],
        compiler_params=_cparams("parallel", "parallel", "parallel", "arbitrary"),
        name="attention",
    )(q, k, vt)


def _hgrn_constants(c):
    levels = int(np.log2(c))
    tri = np.zeros((2, c, c), np.float32)
    M = np.zeros((2, levels, c, c), np.float32)
    tri[0] = np.tril(np.ones((c, c), np.float32))
    for lv in range(levels):
        m = c >> (lv + 1)
        for a in range(0, c, 2 * m):
            M[0, lv, a + m:a + 2 * m, a:a + m] = 1.0
    tri[1] = tri[0][::-1, ::-1]
    M[1] = M[0][:, ::-1, ::-1]
    return tri, M


def _level_exponents(bq, bwd, c):
    rowid = lax.broadcasted_iota(jnp.int32, bq.shape, 0)
    out = []
    m = c // 2
    while m >= 1:
        if 2 * m >= 8:
            blocks = [jnp.broadcast_to(jnp.where(bwd, bq[a + m:a + m + 1, :], bq[a + m - 1:a + m, :]), (2 * m, bq.shape[1]))
                      for a in range(0, c, 2 * m)]
            ref = blocks[0] if len(blocks) == 1 else jnp.concatenate(blocks, axis=0)
        else:
            target = m - 1 + bwd.astype(jnp.int32)
            rowmod = rowid % (2 * m)
            ref = bq
            for delta in range(-m, m + 1):
                if delta != 0:
                    ref = jnp.where(rowmod + delta == target, pltpu.roll(bq, (-delta) % c, 0), ref)
        out.append(-jnp.abs(bq - ref))
        m //= 2
    return out


def _hgrn_kernel(q_ref, z_ref, v_ref, lb_ref, s0_ref, L_ref, M_ref, o_ref, sf_ref, s_sc):
    c = q_ref.shape[0]
    levels = M_ref.shape[0]

    @pl.when(pl.program_id(2) == 0)
    def _():
        s_sc[...] = s0_ref[...]

    z = z_ref[...]
    e = jnp.exp(-jnp.abs(z))
    r = 1.0 / (1.0 + e)
    logsig = jnp.minimum(z, 0.0) + jnp.log(r)
    a = lb_ref[0:1, :]
    cc = lb_ref[1:2, :] + logsig
    logf = jnp.maximum(a, cc) + jnp.log(1.0 + jnp.exp(-jnp.abs(a - cc)))
    kk = lb_ref[2:3, :] * jnp.where(z >= 0.0, e * r, r)

    tri = L_ref[...]
    p1, p2, p3 = _split3(logf)
    bq = _dot(tri, p1) + _dot(tri, p2) + _dot(tri, p3)
    tot = jnp.minimum(bq[0:1, :], bq[c - 1:c, :])
    ex_q = jnp.exp(bq)
    ex_k = jnp.exp(tot - bq)
    ex_lv = [jnp.exp(t) for t in _level_exponents(bq, pl.program_id(0) == 1, c)]
    ones = jnp.ones((c, HEAD_DIM), BF16)
    for h in range(HG_HEADS):
        sl = slice(h * HEAD_DIM, (h + 1) * HEAD_DIM)
        q, k, v = q_ref[:, sl], kk[:, sl], v_ref[:, sl]
        vb = v.astype(BF16)
        amat = jnp.zeros((c, c), F32)
        for lv in range(levels):
            el = ex_lv[lv][:, sl]
            amat = amat + M_ref[lv] * _dot_nt((q * el).astype(BF16), (k * el).astype(BF16))
        s_prev = s_sc[h]
        o = (_dot(amat.astype(BF16), vb) + jnp.sum(q * k, axis=-1, keepdims=True) * v
             + _dot((q * ex_q[:, sl]).astype(BF16), s_prev.astype(BF16)))
        o_ref[:, sl] = o
        dcol = jnp.exp(_dot_tn(p1[:, sl], ones) + _dot_tn(p2[:, sl], ones) + _dot_tn(p3[:, sl], ones))
        s_new = dcol * s_prev + _dot_tn((k * ex_k[:, sl]).astype(BF16), vb)
        s_sc[h] = s_new
        sf_ref[h] = s_new


def _hgrn(z, lbtab, s0, *, batch, n, row0):
    c = min(HG_CHUNK, n)
    nc = n // c
    cb0 = row0 // c
    Lm, Mm = _hgrn_constants(c)
    Lm, Mm = jnp.asarray(Lm, BF16), jnp.asarray(Mm, F32)
    w = HG_HEADS * HEAD_DIM

    def rows(d, b, j):
        return cb0 + b * nc + j + d * (nc - 1 - 2 * j)

    return pl.pallas_call(
        _hgrn_kernel,
        grid=(2, batch, nc),
        in_specs=[pl.BlockSpec((c, w), lambda d, b, j: (rows(d, b, j), 5)),
                  pl.BlockSpec((c, w), lambda d, b, j: (rows(d, b, j), 6 + d)),
                  pl.BlockSpec((c, w), lambda d, b, j: (rows(d, b, j), 8)),
                  pl.BlockSpec((None, 8, w), lambda d, b, j: (d, 0, 0)),
                  pl.BlockSpec((None, None, HG_HEADS, HEAD_DIM, HEAD_DIM), lambda d, b, j: (b, d, 0, 0, 0)),
                  pl.BlockSpec((None,) + Lm.shape[1:], lambda d, b, j: (d, 0, 0)),
                  pl.BlockSpec((None,) + Mm.shape[1:], lambda d, b, j: (d, 0, 0, 0))],
        out_specs=[pl.BlockSpec((None, c, w), lambda d, b, j: (d, rows(d, b, j) - cb0, 0)),
                   pl.BlockSpec((None, None, HG_HEADS, HEAD_DIM, HEAD_DIM), lambda d, b, j: (b, d, 0, 0, 0))],
        out_shape=[jax.ShapeDtypeStruct((2, batch * n, w), F32),
                   jax.ShapeDtypeStruct((batch, 2, HG_HEADS, HEAD_DIM, HEAD_DIM), F32)],
        scratch_shapes=[pltpu.VMEM((HG_HEADS, HEAD_DIM, HEAD_DIM), F32)],
        compiler_params=_cparams("parallel", "parallel", "arbitrary"),
        name="hgrn_scan",
    )(z, z, z, lbtab, s0, Lm, Mm)


def _out_kernel(x_ref, ga_ref, of_ref, ob_ref, hg_ref, ma_ref, w_ref, on_ref, g2_ref,
                g1_ref, sh2_ref, sc2_ref, xo_ref, hhi_ref, hlo_ref, ht_ref):
    o = of_ref[...] + ob_ref[...]
    g = hg_ref[...]
    g = g * (1.0 / (1.0 + jnp.exp(-g)))
    n_g, n_h = GQA_HEADS * HEAD_DIM, HG_HEADS * HEAD_DIM
    mix = _dot(ga_ref[...], w_ref[0:n_g, :]) + _dot(ma_ref[...], w_ref[n_g + n_h:, :])
    for h in range(HG_HEADS):
        sl = slice(h * HEAD_DIM, (h + 1) * HEAD_DIM)
        oh = _rms(o[:, sl], HEAD_DIM) * on_ref[...] * g[:, sl]
        mix = mix + _dot(oh.astype(BF16), w_ref[n_g + h * HEAD_DIM:n_g + (h + 1) * HEAD_DIM, :])
    x = x_ref[...] + g1_ref[...] * mix
    xo_ref[...] = x
    h2 = _rms(x, x.shape[-1]) * g2_ref[...] * (1.0 + sc2_ref[...]) + sh2_ref[...]
    hi, lo = _split2(h2)
    hhi_ref[...] = hi
    hlo_ref[...] = lo
    ht_ref[...] = h2.T.astype(BF16)


def _mixer_out(x, ga, o2, z, ma, w, on, g2, modv, mod_row_of_tile, tm):
    T, D = x.shape
    wh = HG_HEADS * HEAD_DIM

    def mspec(k):
        return pl.BlockSpec((None, 1, D), lambda i: (mod_row_of_tile(i) + k, 0, 0))

    return pl.pallas_call(
        _out_kernel,
        grid=(T // tm,),
        in_specs=[pl.BlockSpec((tm, D), lambda i: (i, 0)),
                  pl.BlockSpec((tm, ga.shape[1]), lambda i: (i, 0)),
                  pl.BlockSpec((None, tm, wh), lambda i: (0, i, 0)),
                  pl.BlockSpec((None, tm, wh), lambda i: (1, i, 0)),
                  pl.BlockSpec((tm, wh), lambda i: (i, 9)),
                  pl.BlockSpec((tm, ma.shape[1]), lambda i: (i, 0)),
                  pl.BlockSpec(w.shape, lambda i: (0, 0)),
                  pl.BlockSpec((1, HEAD_DIM), lambda i: (0, 0)),
                  pl.BlockSpec((1, D), lambda i: (0, 0)),
                  mspec(2), mspec(3), mspec(4)],
        out_specs=[pl.BlockSpec((tm, D), lambda i: (i, 0))] * 3 + [pl.BlockSpec((D, tm), lambda i: (0, i))],
        out_shape=[jax.ShapeDtypeStruct((T, D), F32), jax.ShapeDtypeStruct((T, D), BF16),
                   jax.ShapeDtypeStruct((T, D), BF16), jax.ShapeDtypeStruct((D, T), BF16)],
        compiler_params=_cparams("parallel"),
        name="mixer_out",
    )(x, ga, o2, o2, z, ma, w, on, g2, modv, modv, modv)


def _ws_kernel(k_ref, wq_ref, hi_ref, lo_ref):
    khi, klo = _split2(k_ref[...])
    whi, wlo = _split2(wq_ref[...])
    acc = _dot_nt(khi, whi) + _dot_nt(klo, whi) + _dot_nt(khi, wlo)
    hi, lo = _split2(acc * LOG2E)
    hi_ref[...] = hi
    lo_ref[...] = lo


def _score_weights(keys, wq):
    L, D, _ = wq.shape
    nb = 2 * PEER_HEADS
    sub = keys.shape[-1]
    kk = keys.reshape(L, nb, PEER_KEYS, sub)
    out = jax.ShapeDtypeStruct((L, nb * PEER_KEYS, D), BF16)
    ospec = pl.BlockSpec((None, PEER_KEYS, D), lambda l, b: (l, (b % 2) * PEER_HEADS + b // 2, 0))
    return pl.pallas_call(
        _ws_kernel,
        grid=(L, nb),
        in_specs=[pl.BlockSpec((None, None, PEER_KEYS, sub), lambda l, b: (l, b, 0, 0)),
                  pl.BlockSpec((None, D, sub), lambda l, b: (l, 0, b))],
        out_specs=[ospec, ospec],
        out_shape=[out, out],
        compiler_params=_cparams("parallel", "parallel"),
        name="score_weights",
    )(kk, wq)


def _score_kernel(whi_ref, wlo_ref, hhi_ref, hlo_ref, o_ref):
    whi, hhi = whi_ref[...], hhi_ref[...]
    o_ref[...] = _dot_nt(whi, hhi) + _dot_nt(whi, hlo_ref[...]) + _dot_nt(wlo_ref[...], hhi)


def _scores(whi, wlo, hhi, hlo, tm):
    R, D = whi.shape
    T = hhi.shape[0]
    tr = min(512, R)
    return pl.pallas_call(
        _score_kernel,
        grid=(R // tr, T // tm),
        in_specs=[pl.BlockSpec((tr, D), lambda r, i: (r, 0)), pl.BlockSpec((tr, D), lambda r, i: (r, 0)),
                  pl.BlockSpec((tm, D), lambda r, i: (i, 0)), pl.BlockSpec((tm, D), lambda r, i: (i, 0))],
        out_specs=pl.BlockSpec((tr, tm), lambda r, i: (r, i)),
        out_shape=jax.ShapeDtypeStruct((R, T), F32),
        compiler_params=_cparams("parallel", "parallel"),
        name="scores",
    )(whi, wlo, hhi, hlo)


def _batcher_pairs(n):
    pairs = []
    p = 1
    while p < n:
        k = p
        while k >= 1:
            for j in range(k % p, n - k, 2 * k):
                for i in range(min(k, n - j - k)):
                    if (i + j) // (2 * p) == (i + j + k) // (2 * p):
                        pairs.append((i + j, i + j + k))
            k //= 2
        p *= 2
    return pairs


_SORT16 = _batcher_pairs(PEER_TOPK)


def _sort_desc(v):
    v = list(v)
    for i, j in _SORT16:
        v[i], v[j] = jnp.maximum(v[i], v[j]), jnp.minimum(v[i], v[j])
    return v


def _merge_top(a, b):
    n = len(a)
    v = [jnp.maximum(a[i], b[n - 1 - i]) for i in range(n)]
    s = n // 2
    while s >= 1:
        for i in range(n):
            if (i & s) == 0:
                v[i], v[i + s] = jnp.maximum(v[i], v[i + s]), jnp.minimum(v[i], v[i + s])
        s //= 2
    return v


def _top16_of_keys(s_ref, row0, lanes):
    k = PEER_TOPK
    groups = [_sort_desc([s_ref[pl.ds(row0 + PEER_HEADS * (k * g + i), PEER_HEADS), lanes] for i in range(k)])
              for g in range(PEER_KEYS // k)]
    while len(groups) > 1:
        groups = [_merge_top(groups[i], groups[i + 1]) for i in range(0, len(groups), 2)]
    return groups[0]


def _topk_kernel(s_ref, thr_ref, s1n_ref):
    k = PEER_TOPK
    half = PEER_HEADS * PEER_KEYS
    ninf = jnp.full((PEER_HEADS, 128), -jnp.inf, F32)

    def chunk(ch, carry):
        lanes = pl.ds(pl.multiple_of(ch * 128, 128), 128)
        a = _top16_of_keys(s_ref, 0, lanes)
        b = _top16_of_keys(s_ref, half, lanes)
        cand = [[a[i] + b[j] for j in range(k // (i + 1))] for i in range(k)]
        g0 = cand[0]
        g1 = _sort_desc(cand[1] + cand[2] + cand[5] + cand[8])
        g2 = _sort_desc(cand[3] + cand[4] + cand[6] + cand[7] + cand[9] + cand[10] + cand[11] + cand[12] + cand[13])
        g3 = cand[14] + cand[15] + [ninf] * (k - 2)
        top = _merge_top(_merge_top(g0, g1), _merge_top(g2, g3))
        zsum = jnp.zeros((PEER_HEADS, 128), F32)
        for t in top:
            zsum = zsum + jnp.exp2(t - top[0])
        c0 = top[0] + jnp.log2(zsum)
        thr_ref[:, lanes] = top[k - 1] - c0
        for key in range(PEER_KEYS):
            tile = s_ref[pl.ds(half + PEER_HEADS * key, PEER_HEADS), lanes] - c0
            s1n_ref[ch, pl.ds(key, PEER_HEADS, stride=PEER_KEYS), :] = tile
        return carry

    lax.fori_loop(0, s_ref.shape[1] // 128, chunk, 0)


def _topk(st, tm):
    R, T = st.shape
    return pl.pallas_call(
        _topk_kernel,
        grid=(T // tm,),
        in_specs=[pl.BlockSpec((R, tm), lambda i: (0, i))],
        out_specs=[pl.BlockSpec((PEER_HEADS, tm), lambda i: (0, i)),
                   pl.BlockSpec((tm // 128, R // 2, 128), lambda i: (i, 0, 0))],
        out_shape=[jax.ShapeDtypeStruct((PEER_HEADS, T), F32), jax.ShapeDtypeStruct((T // 128, R // 2, 128), F32)],
        compiler_params=_cparams("parallel"),
        name="topk",
    )(st)


def _peer_kernel(ht_ref, s0_ref, s1n_ref, thr_ref, u_ref, vt_ref, x_ref, g2_ref, o_ref,
                 pre_a, pre_b, c_a, c_b, acc_sc):
    s = pl.program_id(1)
    te, tm = c_a.shape
    nk = PEER_KEYS

    @pl.when(s == 0)
    def _():
        acc_sc[...] = jnp.zeros_like(acc_sc)
        pre_b[...] = jnp.zeros_like(pre_b)
        c_a[...] = jnp.zeros_like(c_a)

    d = ht_ref.shape[0]
    nblk = (te // nk) * (tm // 128)
    nh = 2 if tm % 256 == 0 else 1
    th = tm // nh
    ka = max(1, nblk // (2 * nh))
    mc = max(1, nblk // (4 * nh))
    kq, mq = d // ka, d // mc

    def pre_piece(p, pre_w):
        half, q = divmod(p, ka)
        part = _dot(u_ref[:, q * kq:(q + 1) * kq], ht_ref[q * kq:(q + 1) * kq, half * th:(half + 1) * th])
        if q == 0:
            pre_w[:, half * th:(half + 1) * th] = part
        else:
            pre_w[:, half * th:(half + 1) * th] += part

    def value_piece(p, c_r):
        half, q = divmod(p, mc)
        acc_sc[q * mq:(q + 1) * mq, half * th:(half + 1) * th] += _dot(
            vt_ref[q * mq:(q + 1) * mq, :], c_r[:, half * th:(half + 1) * th])

    slots = {}
    for p in range(nh * ka):
        slots.setdefault(p * nblk // (nh * ka), []).append((pre_piece, p))
    for p in range(nh * mc):
        slots.setdefault((p * nblk // (nh * mc) + 1) % nblk, []).append((value_piece, p))

    def stages(pre_w, pre_r, c_w, c_r):
        for blk in range(nblk):
            for piece, p in slots.get(blk, ()):
                piece(p, pre_w if piece is pre_piece else c_r)
            ii, ch = divmod(blk, tm // 128)
            lanes = slice(ch * 128, (ch + 1) * 128)
            gate = jnp.zeros((nk, 128), F32)
            for h in range(PEER_HEADS):
                r0 = ii * PEER_HEADS + h
                val = s1n_ref[ch, h * nk:(h + 1) * nk, :] + s0_ref[r0:r0 + 1, lanes]
                gate = gate + jnp.where(val >= thr_ref[h:h + 1, lanes], jnp.exp2(val), 0.0)
            pre = pre_r[ii * nk:(ii + 1) * nk, lanes]
            act = 0.5 * pre * (1.0 + lax.erf(pre * (2.0 ** -0.5)))
            c_w[ii * nk:(ii + 1) * nk, lanes] = (gate * act).astype(BF16)

    @pl.when(s % 2 == 0)
    def _():
        stages(pre_a, pre_b, c_b, c_a)

    @pl.when(s % 2 == 1)
    def _():
        stages(pre_b, pre_a, c_a, c_b)

    @pl.when(s == pl.num_programs(1) - 1)
    def _():
        o_ref[...] = x_ref[...] + g2_ref[...] * acc_sc[...].T


def _peer(ht, st, thr, s1n, u, vt, x, modv, mod_row_of_tile, tm, te):
    T, D = x.shape
    E = u.shape[0]
    ne = E // te
    half = PEER_HEADS * PEER_KEYS
    tile = lambda s, lag: jnp.clip(s - lag, 0, ne - 1)
    return pl.pallas_call(
        _peer_kernel,
        grid=(T // tm, ne + 2),
        in_specs=[pl.BlockSpec((D, tm), lambda i, s: (0, i)),
                  pl.BlockSpec((te // PEER_KEYS * PEER_HEADS, tm), lambda i, s: (tile(s, 1), i)),
                  pl.BlockSpec((tm // 128, half, 128), lambda i, s: (i, 0, 0)),
                  pl.BlockSpec((PEER_HEADS, tm), lambda i, s: (0, i)),
                  pl.BlockSpec((te, D), lambda i, s: (tile(s, 0), 0)),
                  pl.BlockSpec((D, te), lambda i, s: (0, tile(s, 2))),
                  pl.BlockSpec((tm, D), lambda i, s: (i, 0)),
                  pl.BlockSpec((None, 1, D), lambda i, s: (mod_row_of_tile(i) + 5, 0, 0))],
        out_specs=pl.BlockSpec((tm, D), lambda i, s: (i, 0)),
        out_shape=jax.ShapeDtypeStruct((T, D), F32),
        scratch_shapes=[pltpu.VMEM((te, tm), F32), pltpu.VMEM((te, tm), F32), pltpu.VMEM((te, tm), BF16),
                        pltpu.VMEM((te, tm), BF16), pltpu.VMEM((D, tm), F32)],
        compiler_params=_cparams("parallel", "arbitrary"),
        name="peer_experts",
    )(ht, st, s1n, thr, u, vt, x, modv)


def _rope_tables(n, ident_rows):
    pos = jnp.arange(n, dtype=jnp.int32)
    row, col = pos // GRID_W, pos % GRID_W
    lane = jnp.arange(HEAD_DIM)

    def build(nfreq, width):
        inv = ROPE_THETA ** (-jnp.arange(nfreq, dtype=F32) / nfreq)
        ar = row.astype(F32)[:, None] * inv[None, :]
        ac = col.astype(F32)[:, None] * inv[None, :]
        cos = jnp.concatenate([jnp.cos(ar), jnp.cos(ar), jnp.cos(ac), jnp.cos(ac)], axis=-1)
        sin = jnp.concatenate([jnp.sin(ar), jnp.sin(ar), jnp.sin(ac), jnp.sin(ac)], axis=-1)
        pad = HEAD_DIM - width
        cos = jnp.pad(cos, ((0, 0), (0, pad)))
        sin = jnp.pad(sin, ((0, 0), (0, pad)))
        first = (lane % (2 * nfreq)) < nfreq
        tab = jnp.stack([cos, jnp.where(first, 0.0, sin), jnp.where(first, -sin, 0.0)])
        ident = jnp.stack([jnp.ones((ident_rows, HEAD_DIM), F32), jnp.zeros((ident_rows, HEAD_DIM), F32),
                           jnp.zeros((ident_rows, HEAD_DIM), F32)])
        return jnp.concatenate([ident, tab], axis=1)

    return build(HEAD_DIM // 4, HEAD_DIM), build(MLA_ROPE // 4, MLA_ROPE)


def kernel(x_prompt, x_sample, c, cache_gqa_k, cache_gqa_v, cache_mla_ckv, cache_mla_krope, state_hgrn, c_ctx,
           w_mod, b_mod, norm1_g, norm2_g, w_in, gqa_qn, gqa_kn, mla_qa_n, mla_kva_n, w_mla_qb, w_mla_kvb,
           mla_qn, mla_kn, hg_lb_logits, hg_on, w_out, peer_wq, peer_keys, peer_u, peer_v):
    B, S, D = x_prompt.shape
    DB, DS, _ = x_sample.shape
    L = w_mod.shape[0]
    P = cache_gqa_k.shape[2]
    TC, TL = B * S, DB * DS
    T = TC + TL
    tm = 512 if (TC % 512 == 0 and DS % 512 == 0) else 128
    tm_small = min(256, tm)

    ngroups = 1 + DB

    def mod_row_of(t, layer):
        def f(i):
            group = jnp.where(i < TC // t, 0, 1 + (i - TC // t) // (DS // t))
            return (layer * ngroups + group) * 6
        return f

    def table_of(t):
        def f(i):
            return jnp.where(i < TC // t, 0, 1 + (i - TC // t) % (DS // t))
        return f

    cond8 = jnp.zeros((8, D), F32).at[0].set(c_ctx).at[1:1 + DB].set(c)
    modv = _modulation(cond8, w_mod, b_mod)[:, :ngroups].reshape(L * ngroups * 6, 1, D)

    lb_cum = jnp.cumsum(jax.nn.softmax(hg_lb_logits.astype(F32), axis=0), axis=0)
    lbs = jnp.maximum(lb_cum - lb_cum[:1], 0.0)
    lbtab = jnp.stack([jnp.log(lbs), jnp.log1p(-lbs), 1.0 - lbs] + [jnp.zeros_like(lbs)] * 5, axis=2)

    sizes = np.cumsum([0, 768, 256, 256, 512, 512, 512, 512, 512, 512, 512, 64])
    part = lambda a, k: a[..., sizes[k]:sizes[k + 1]]
    order = (0, 1, 2, 10, 8, 9, 3, 4, 5, 6, 7)
    cols = []
    for k in order:
        cols.append(part(w_in, k))
        if k == 10:
            cols.append(jnp.zeros((L, D, 256 - MLA_ROPE), w_in.dtype))
    w_in_b = jnp.concatenate(cols, axis=-1).astype(BF16)

    wqb = w_mla_qb.reshape(L, MLA_RANK, MLA_HEADS, MLA_QK)
    wqb = jnp.pad(wqb, ((0, 0), (0, 0), (0, 0), (0, MLA_PAD - MLA_QK))).reshape(L, MLA_RANK, MLA_HEADS * MLA_PAD).astype(BF16)
    wkvb = w_mla_kvb.reshape(L, MLA_RANK, MLA_HEADS, 2, HEAD_DIM)
    wkvb = jnp.moveaxis(wkvb, 3, 2).reshape(L, MLA_RANK, 2 * MLA_HEADS * HEAD_DIM).astype(BF16)
    rpad = lambda g: jnp.pad(g[:, HEAD_DIM:], ((0, 0), (0, HEAD_DIM - MLA_ROPE)))[:, None, :]
    mqn_n, mqn_r = mla_qn[:, None, :HEAD_DIM], rpad(mla_qn)
    mkn_n, mkn_r = mla_kn[:, None, :HEAD_DIM], rpad(mla_kn)
    w_out_b = w_out.astype(BF16)
    u_b = peer_u.astype(BF16)
    vt_b = jnp.swapaxes(peer_v, 1, 2).astype(BF16)
    def key_major(w):
        return jnp.swapaxes(w.reshape(L, 2, PEER_HEADS, PEER_KEYS, D), 2, 3).reshape(L, 2 * PEER_HEADS * PEER_KEYS, D)

    ws_hi, ws_lo = map(key_major, _score_weights(peer_keys, peer_wq))

    tabg, tabm = _rope_tables(DS, tm_small)

    cck = jnp.moveaxis(cache_mla_ckv, 1, 0).reshape(L, DB * P, MLA_RANK)
    ckr = jnp.pad(jnp.moveaxis(cache_mla_krope, 1, 0).reshape(L, DB * P, MLA_ROPE), ((0, 0), (0, 0), (0, HEAD_DIM - MLA_ROPE)))
    cmk, cmv = _cache_expand(cck, ckr, wkvb, mkn_n, mkn_r)
    cgk = jnp.moveaxis(cache_gqa_k, 1, 0).reshape(L, DB, P, -1).astype(BF16)
    cgv = jnp.transpose(cache_gqa_v.reshape(DB, L, P, -1), (1, 3, 0, 2)).astype(BF16)
    cmk = cmk.reshape(L, DB, P, -1)
    cmv = cmv.reshape(L, -1, DB, P)
    s0_ctx = jnp.zeros((B, 2, HG_HEADS, HEAD_DIM, HEAD_DIM), F32)

    x = jnp.concatenate([x_prompt.reshape(TC, D), x_sample.reshape(TL, D)], axis=0)
    new_k, new_v, new_ckv, new_kr, new_s = [], [], [], [], []
    tq_c = min(256, S)
    tq_l, tk_l = min(1024, DS), min(512, P, DS)
    for l in range(L):
        z = _in_proj(x, norm1_g[l][None], modv, w_in_b[l], mod_row_of(tm, l), tm)
        gq, gkn, gkr, gvt, mq, ckv, mk, mvt = _prep(
            z, tabg, tabm, table_of(tm_small), tm_small, gqa_qn[l][None], gqa_kn[l][None], mla_qa_n[l][None],
            mla_kva_n[l][None], wqb[l], wkvb[l], mqn_n[l], mqn_r[l], mkn_n[l], mkn_r[l])
        new_k.append(gkn[:TC].reshape(B, S, GQA_KV_HEADS, HEAD_DIM))
        new_v.append(z[:TC, 1024:1280].reshape(B, S, GQA_KV_HEADS, HEAD_DIM))
        new_ckv.append(ckv[:TC].reshape(B, S, MLA_RANK))
        new_kr.append(z[:TC, 1280:1280 + MLA_ROPE].reshape(B, S, MLA_ROPE))

        def lat_keys(cache, new):
            return jnp.concatenate([cache, new[TC:].reshape(DB, DS, -1)], axis=1).reshape(DB * (P + DS), -1)

        def lat_values(cache_t, new_t):
            w = new_t.shape[0]
            return jnp.concatenate([cache_t, new_t[:, TC:].reshape(w, DB, DS)], axis=2).reshape(w, DB * (P + DS))

        gqa = dict(heads=GQA_HEADS, group=GQA_HEADS // GQA_KV_HEADS, dq=HEAD_DIM, dv=HEAD_DIM, hb=GQA_HEADS // GQA_KV_HEADS)
        mla = dict(heads=MLA_HEADS, group=1, dq=MLA_PAD, dv=HEAD_DIM, hb=MLA_HEADS // 2)
        ctx_a = dict(batch=B, nq=S, nk=S, q_row0=0, k_row0=0, tq=tq_c, tk=tq_c)
        lat_a = dict(batch=DB, nq=DS, nk=P + DS, q_row0=TC, k_row0=0, tq=tq_l, tk=tk_l)
        ga = jnp.concatenate([_attention(gq, gkr, gvt, **gqa, **ctx_a),
                              _attention(gq, lat_keys(cgk[l], gkr), lat_values(cgv[l], gvt), **gqa, **lat_a)], axis=0)
        ma = jnp.concatenate([_attention(mq, mk, mvt, **mla, **ctx_a),
                              _attention(mq, lat_keys(cmk[l], mk), lat_values(cmv[l], mvt), **mla, **lat_a)], axis=0)

        o_c, s_c = _hgrn(z, lbtab[l], s0_ctx, batch=B, n=S, row0=0)
        o_l, _ = _hgrn(z, lbtab[l], state_hgrn[:, l].astype(F32), batch=DB, n=DS, row0=TC)
        new_s.append(s_c)
        o2 = jnp.concatenate([o_c, o_l], axis=1)

        x, hhi, hlo, ht = _mixer_out(x, ga, o2, z, ma, w_out_b[l], hg_on[l][None], norm2_g[l][None], modv,
                                 mod_row_of(tm_small, l), tm_small)
        st = _scores(ws_hi[l], ws_lo[l], hhi, hlo, tm)
        thr, s1n = _topk(st, tm)
        x = _peer(ht, st, thr, s1n, u_b[l], vt_b[l], x, modv, mod_row_of(tm, l), tm, PEER_EXPERT_TILE)

    return (x[:TC].reshape(B, S, D), x[TC:].reshape(DB, DS, D),
            jnp.stack(new_k, axis=1), jnp.stack(new_v, axis=1), jnp.stack(new_ckv, axis=1),
            jnp.stack(new_kr, axis=1), jnp.stack(new_s, axis=1))
```

```python
import functools

import numpy as np
import jax
import jax.numpy as jnp
from jax import lax
from jax.experimental import pallas as pl
from jax.experimental.pallas import tpu as pltpu

F32 = jnp.float32
BF16 = jnp.bfloat16

EPS = 1e-6
LOG2E = 1.4426950408889634
ROPE_THETA = 10000.0
GRID_W = 64

HEAD_DIM = 128
GQA_HEADS, GQA_KV_HEADS = 6, 2
HG_HEADS = 4
MLA_HEADS, MLA_ROPE, MLA_RANK = 6, 64, 512
MLA_QK = HEAD_DIM + MLA_ROPE
MLA_PAD = 2 * HEAD_DIM
PEER_HEADS, PEER_KEYS, PEER_TOPK = 8, 128, 16
HG_CHUNK = 128
PEER_EXPERT_TILE = 512
IN_TILE = 1024

VMEM_LIMIT = 60 * 1024 * 1024


def _cparams(*sem):
    return pltpu.CompilerParams(dimension_semantics=sem, vmem_limit_bytes=VMEM_LIMIT)


def _dot(a, b):
    return jnp.dot(a, b, preferred_element_type=F32)


def _dot_nt(a, b):
    return lax.dot_general(a, b, (((1,), (1,)), ((), ())), preferred_element_type=F32)


def _dot_tn(a, b):
    return lax.dot_general(a, b, (((0,), (0,)), ((), ())), preferred_element_type=F32)


def _split2(a):
    hi = a.astype(BF16)
    lo = (a - hi.astype(F32)).astype(BF16)
    return hi, lo


def _split3(a):
    p1 = a.astype(BF16)
    r = a - p1.astype(F32)
    p2 = r.astype(BF16)
    p3 = (r - p2.astype(F32)).astype(BF16)
    return p1, p2, p3


def _rms(x, n):
    return x * lax.rsqrt(jnp.sum(x * x, axis=-1, keepdims=True) * (1.0 / n) + EPS)


def _mod_kernel(cond_ref, w_ref, b_ref, o_ref):
    c = cond_ref[...]
    s = c * (1.0 / (1.0 + jnp.exp(-c)))
    shi, slo = _split2(s)
    whi, wlo = _split2(w_ref[...])
    o_ref[...] = _dot(shi, whi) + _dot(slo, whi) + _dot(shi, wlo) + b_ref[...]


def _modulation(cond8, w_mod, b_mod):
    L, D, N = w_mod.shape
    tn = min(512, N)
    return pl.pallas_call(
        _mod_kernel,
        grid=(L, N // tn),
        in_specs=[pl.BlockSpec((8, D), lambda l, j: (0, 0)),
                  pl.BlockSpec((None, D, tn), lambda l, j: (l, 0, j)),
                  pl.BlockSpec((None, 1, tn), lambda l, j: (l, 0, j))],
        out_specs=pl.BlockSpec((None, 8, tn), lambda l, j: (l, 0, j)),
        out_shape=jax.ShapeDtypeStruct((L, 8, N), F32),
        compiler_params=_cparams("parallel", "parallel"),
        name="modulation",
    )(cond8, w_mod, b_mod.reshape(L, 1, N))


def _in_kernel(x_ref, g_ref, sh_ref, sc_ref, w_ref, o_ref, h_sc):
    @pl.when(pl.program_id(1) == 0)
    def _():
        x = x_ref[...]
        h = _rms(x, x.shape[-1]) * g_ref[...] * (1.0 + sc_ref[...]) + sh_ref[...]
        h_sc[...] = h.astype(BF16)

    o_ref[...] = _dot(h_sc[...], w_ref[...])


def _in_proj(x, g, modv, w, mod_row_of_tile, tm):
    T, D = x.shape
    nj, _, tn = w.shape
    N = nj * tn

    def mspec(k):
        return pl.BlockSpec((None, 1, D), lambda i, j: (mod_row_of_tile(i) + k, 0, 0))

    return pl.pallas_call(
        _in_kernel,
        grid=(T // tm, nj),
        in_specs=[pl.BlockSpec((tm, D), lambda i, j: (i, 0)),
                  pl.BlockSpec((1, D), lambda i, j: (0, 0)),
                  mspec(0), mspec(1),
                  pl.BlockSpec((None, D, tn), lambda i, j: (j, 0, 0))],
        out_specs=pl.BlockSpec((tm, tn), lambda i, j: (i, j)),
        out_shape=jax.ShapeDtypeStruct((T, N), F32),
        scratch_shapes=[pltpu.VMEM((tm, D), BF16)],
        compiler_params=_cparams("parallel", "arbitrary"),
        name="in_proj",
    )(x, g, modv, modv, w)


def _rope(x, tab_ref, blk):
    return (x * tab_ref[0] + pltpu.roll(x, blk, 1) * tab_ref[1]
            + pltpu.roll(x, HEAD_DIM - blk, 1) * tab_ref[2])


def _mla_kv(ckv, krope, wkvb_ref, gkn_ref, gkr_ref, tab_ref, mk_ref, mv_ref):
    kv = _dot(ckv.astype(BF16), wkvb_ref[...])
    nv = MLA_HEADS * HEAD_DIM
    mv_ref[...] = kv[:, nv:].T.astype(BF16)
    ssr = jnp.sum(krope * krope, axis=-1, keepdims=True)
    for h in range(MLA_HEADS):
        kn = kv[:, h * HEAD_DIM:(h + 1) * HEAD_DIM]
        rs = lax.rsqrt((jnp.sum(kn * kn, axis=-1, keepdims=True) + ssr) * (1.0 / MLA_QK) + EPS)
        kr = krope * rs * gkr_ref[...]
        if tab_ref is not None:
            kr = _rope(kr, tab_ref, MLA_ROPE // 4)
        mk_ref[:, h * MLA_PAD:h * MLA_PAD + HEAD_DIM] = (kn * rs * gkn_ref[...]).astype(BF16)
        mk_ref[:, h * MLA_PAD + HEAD_DIM:(h + 1) * MLA_PAD] = kr.astype(BF16)


def _prep_kernel(gq_ref, gk_ref, gv_ref, mkr_ref, mqa_ref, mkva_ref, tg_ref, tm_ref,
                 gqn_ref, gkn_ref, qan_ref, kvan_ref, wqb_ref, wkvb_ref,
                 mqn_n_ref, mqn_r_ref, mkn_n_ref, mkn_r_ref,
                 gq_o, gkn_o, gkr_o, gvt_o, mq_o, ckv_o, mk_o, mv_o):
    gscale = HEAD_DIM ** -0.5 * LOG2E
    for h in range(GQA_HEADS):
        sl = slice(h * HEAD_DIM, (h + 1) * HEAD_DIM)
        q = _rms(gq_ref[:, sl], HEAD_DIM) * gqn_ref[...]
        gq_o[:, sl] = (_rope(q, tg_ref, HEAD_DIM // 4) * gscale).astype(BF16)
    for h in range(GQA_KV_HEADS):
        sl = slice(h * HEAD_DIM, (h + 1) * HEAD_DIM)
        k = _rms(gk_ref[:, sl], HEAD_DIM) * gkn_ref[...]
        gkn_o[:, sl] = k
        gkr_o[:, sl] = _rope(k, tg_ref, HEAD_DIM // 4).astype(BF16)
    gvt_o[...] = gv_ref[...].T.astype(BF16)

    qa = _rms(mqa_ref[...], MLA_RANK) * qan_ref[...]
    mq = _dot(qa.astype(BF16), wqb_ref[...])
    mscale = MLA_QK ** -0.5 * LOG2E
    for h in range(MLA_HEADS):
        qn = mq[:, h * MLA_PAD:h * MLA_PAD + HEAD_DIM]
        qr = mq[:, h * MLA_PAD + HEAD_DIM:(h + 1) * MLA_PAD]
        ss = jnp.sum(qn * qn, axis=-1, keepdims=True) + jnp.sum(qr * qr, axis=-1, keepdims=True)
        rs = lax.rsqrt(ss * (1.0 / MLA_QK) + EPS) * mscale
        mq_o[:, h * MLA_PAD:h * MLA_PAD + HEAD_DIM] = (qn * rs * mqn_n_ref[...]).astype(BF16)
        mq_o[:, h * MLA_PAD + HEAD_DIM:(h + 1) * MLA_PAD] = _rope(qr * rs * mqn_r_ref[...], tm_ref, MLA_ROPE // 4).astype(BF16)

    ckv = _rms(mkva_ref[...], MLA_RANK) * kvan_ref[...]
    ckv_o[...] = ckv
    _mla_kv(ckv, mkr_ref[...], wkvb_ref, mkn_n_ref, mkn_r_ref, tm_ref, mk_o, mv_o)


def _prep(z, tabg, tabm, table_of_tile, tm, gqn, gkn, qan, kvan, wqb, wkvb, mqn_n, mqn_r, mkn_n, mkn_r):
    T = z.shape[0]
    row = lambda w: pl.BlockSpec((1, w), lambda i: (0, 0))
    full = lambda a: pl.BlockSpec(a.shape, lambda i: (0, 0))
    tab = pl.BlockSpec((3, tm, HEAD_DIM), lambda i: (0, table_of_tile(i), 0))
    zspec = lambda w, blk: pl.BlockSpec((tm, w), lambda i: (i, blk))
    widths = (GQA_HEADS * HEAD_DIM, GQA_KV_HEADS * HEAD_DIM, GQA_KV_HEADS * HEAD_DIM, GQA_KV_HEADS * HEAD_DIM,
              MLA_HEADS * MLA_PAD, MLA_RANK, MLA_HEADS * MLA_PAD, MLA_HEADS * HEAD_DIM)
    dtypes = (BF16, F32, BF16, BF16, BF16, F32, BF16, BF16)
    transposed = (False, False, False, True, False, False, False, True)
    ospec = lambda w, t: pl.BlockSpec((w, tm), lambda i: (0, i)) if t else pl.BlockSpec((tm, w), lambda i: (i, 0))
    return pl.pallas_call(
        _prep_kernel,
        grid=(T // tm,),
        in_specs=[zspec(768, 0), zspec(256, 3), zspec(256, 4), zspec(128, 10), zspec(512, 3), zspec(512, 4),
                  tab, tab, row(128), row(128), row(512), row(512), full(wqb), full(wkvb),
                  row(128), row(128), row(128), row(128)],
        out_specs=[ospec(w, t) for w, t in zip(widths, transposed)],
        out_shape=[jax.ShapeDtypeStruct((w, T) if t else (T, w), d) for w, d, t in zip(widths, dtypes, transposed)],
        compiler_params=_cparams("parallel"),
        name="prep",
    )(z, z, z, z, z, z, tabg, tabm, gqn, gkn, qan, kvan, wqb, wkvb, mqn_n, mqn_r, mkn_n, mkn_r)


def _cache_kernel(ckv_ref, kr_ref, wkvb_ref, gkn_ref, gkr_ref, mk_o, mv_o):
    _mla_kv(ckv_ref[...], kr_ref[...], wkvb_ref, gkn_ref, gkr_ref, None, mk_o, mv_o)


def _cache_expand(ckv, kr, wkvb, mkn_n, mkn_r):
    L, R, _ = ckv.shape
    tm = min(512, R)
    return pl.pallas_call(
        _cache_kernel,
        grid=(L, R // tm),
        in_specs=[pl.BlockSpec((None, tm, MLA_RANK), lambda l, i: (l, i, 0)),
                  pl.BlockSpec((None, tm, HEAD_DIM), lambda l, i: (l, i, 0)),
                  pl.BlockSpec((None,) + wkvb.shape[1:], lambda l, i: (l, 0, 0)),
                  pl.BlockSpec((None, 1, HEAD_DIM), lambda l, i: (l, 0, 0)),
                  pl.BlockSpec((None, 1, HEAD_DIM), lambda l, i: (l, 0, 0))],
        out_specs=[pl.BlockSpec((None, tm, MLA_HEADS * MLA_PAD), lambda l, i: (l, i, 0)),
                   pl.BlockSpec((None, MLA_HEADS * HEAD_DIM, tm), lambda l, i: (l, 0, i))],
        out_shape=[jax.ShapeDtypeStruct((L, R, MLA_HEADS * MLA_PAD), BF16),
                   jax.ShapeDtypeStruct((L, MLA_HEADS * HEAD_DIM, R), BF16)],
        compiler_params=_cparams("parallel", "parallel"),
        name="cache_expand",
    )(ckv, kr, wkvb, mkn_n, mkn_r)


def _attn_kernel(q_ref, k_ref, vt_ref, o_ref, m_sc, l_sc, acc_sc, *, heads, group, dq, dv):
    ki = pl.program_id(3)

    @pl.when(ki == 0)
    def _():
        m_sc[...] = jnp.full_like(m_sc, -jnp.inf)
        l_sc[...] = jnp.zeros_like(l_sc)
        acc_sc[...] = jnp.zeros_like(acc_sc)

    for h in range(heads):
        kv = h // group
        s = _dot_nt(k_ref[:, kv * dq:(kv + 1) * dq], q_ref[:, h * dq:(h + 1) * dq])
        m_prev = m_sc[h]
        m_new = jnp.maximum(m_prev, jnp.max(s, axis=0, keepdims=True))
        p = jnp.exp2(s - m_new)
        alpha = jnp.exp2(m_prev - m_new)
        l_sc[h] = alpha * l_sc[h] + jnp.sum(p, axis=0, keepdims=True)
        acc_sc[h] = alpha * acc_sc[h] + _dot(vt_ref[kv * dv:(kv + 1) * dv, :], p.astype(BF16))
        m_sc[h] = m_new

    @pl.when(ki == pl.num_programs(3) - 1)
    def _():
        for h in range(heads):
            o_ref[:, h * dv:(h + 1) * dv] = (acc_sc[h] / l_sc[h]).T.astype(o_ref.dtype)


def _attention(q, k, vt, *, batch, nq, nk, q_row0, k_row0, heads, group, dq, dv, tq, tk, hb):
    nqb, nkb = nq // tq, nk // tk
    qb0, kb0 = q_row0 // tq, k_row0 // tk
    kvb = max(1, hb // group)
    return pl.pallas_call(
        functools.partial(_attn_kernel, heads=hb, group=group, dq=dq, dv=dv),
        grid=(batch, heads // hb, nqb, nkb),
        in_specs=[pl.BlockSpec((tq, hb * dq), lambda b, h, i, j: (qb0 + b * nqb + i, h)),
                  pl.BlockSpec((tk, kvb * dq), lambda b, h, i, j: (kb0 + b * nkb + j, h)),
                  pl.BlockSpec((kvb * dv, tk), lambda b, h, i, j: (h, kb0 + b * nkb + j))],
        out_specs=pl.BlockSpec((tq, hb * dv), lambda b, h, i, j: (b * nqb + i, h)),
        out_shape=jax.ShapeDtypeStruct((batch * nq, heads * dv), BF16),
        scratch_shapes=[pltpu.VMEM((hb, 1, tq), F32), pltpu.VMEM((hb, 1, tq), F32), pltpu.VMEM((hb, dv, tq), F32)],
        compiler_params=_cparams("parallel", "parallel", "parallel", "arbitrary"),
        name="attention",
    )(q, k, vt)


def _hgrn_constants(c):
    levels = int(np.log2(c))
    tri = np.zeros((2, c, c), np.float32)
    M = np.zeros((2, levels, c, c), np.float32)
    tri[0] = np.tril(np.ones((c, c), np.float32))
    for lv in range(levels):
        m = c >> (lv + 1)
        for a in range(0, c, 2 * m):
            M[0, lv, a + m:a + 2 * m, a:a + m] = 1.0
    tri[1] = tri[0][::-1, ::-1]
    M[1] = M[0][:, ::-1, ::-1]
    return tri, M


def _level_exponents(bq, bwd, c):
    rowid = lax.broadcasted_iota(jnp.int32, bq.shape, 0)
    out = []
    m = c // 2
    while m >= 1:
        if 2 * m >= 8:
            blocks = [jnp.broadcast_to(jnp.where(bwd, bq[a + m:a + m + 1, :], bq[a + m - 1:a + m, :]), (2 * m, bq.shape[1]))
                      for a in range(0, c, 2 * m)]
            ref = blocks[0] if len(blocks) == 1 else jnp.concatenate(blocks, axis=0)
        else:
            target = m - 1 + bwd.astype(jnp.int32)
            rowmod = rowid % (2 * m)
            ref = bq
            for delta in range(-m, m + 1):
                if delta != 0:
                    ref = jnp.where(rowmod + delta == target, pltpu.roll(bq, (-delta) % c, 0), ref)
        out.append(-jnp.abs(bq - ref))
        m //= 2
    return out


def _hgrn_kernel(q_ref, z_ref, v_ref, lb_ref, s0_ref, L_ref, M_ref, o_ref, sf_ref, s_sc):
    c = q_ref.shape[0]
    levels = M_ref.shape[0]

    @pl.when(pl.program_id(2) == 0)
    def _():
        s_sc[...] = s0_ref[...]

    z = z_ref[...]
    e = jnp.exp(-jnp.abs(z))
    r = 1.0 / (1.0 + e)
    logsig = jnp.minimum(z, 0.0) + jnp.log(r)
    a = lb_ref[0:1, :]
    cc = lb_ref[1:2, :] + logsig
    logf = jnp.maximum(a, cc) + jnp.log(1.0 + jnp.exp(-jnp.abs(a - cc)))
    kk = lb_ref[2:3, :] * jnp.where(z >= 0.0, e * r, r)

    tri = L_ref[...]
    p1, p2, p3 = _split3(logf)
    bq = _dot(tri, p1) + _dot(tri, p2) + _dot(tri, p3)
    tot = jnp.minimum(bq[0:1, :], bq[c - 1:c, :])
    ex_q = jnp.exp(bq)
    ex_k = jnp.exp(tot - bq)
    ex_lv = [jnp.exp(t) for t in _level_exponents(bq, pl.program_id(0) == 1, c)]
    ones = jnp.ones((c, HEAD_DIM), BF16)
    for h in range(HG_HEADS):
        sl = slice(h * HEAD_DIM, (h + 1) * HEAD_DIM)
        q, k, v = q_ref[:, sl], kk[:, sl], v_ref[:, sl]
        vb = v.astype(BF16)
        amat = jnp.zeros((c, c), F32)
        for lv in range(levels):
            el = ex_lv[lv][:, sl]
            amat = amat + M_ref[lv] * _dot_nt((q * el).astype(BF16), (k * el).astype(BF16))
        s_prev = s_sc[h]
        o = (_dot(amat.astype(BF16), vb) + jnp.sum(q * k, axis=-1, keepdims=True) * v
             + _dot((q * ex_q[:, sl]).astype(BF16), s_prev.astype(BF16)))
        o_ref[:, sl] = o
        dcol = jnp.exp(_dot_tn(p1[:, sl], ones) + _dot_tn(p2[:, sl], ones) + _dot_tn(p3[:, sl], ones))
        s_new = dcol * s_prev + _dot_tn((k * ex_k[:, sl]).astype(BF16), vb)
        s_sc[h] = s_new
        sf_ref[h] = s_new


def _hgrn(z, lbtab, s0, *, batch, n, row0):
    c = min(HG_CHUNK, n)
    nc = n // c
    cb0 = row0 // c
    Lm, Mm = _hgrn_constants(c)
    Lm, Mm = jnp.asarray(Lm, BF16), jnp.asarray(Mm, F32)
    w = HG_HEADS * HEAD_DIM

    def rows(d, b, j):
        return cb0 + b * nc + j + d * (nc - 1 - 2 * j)

    return pl.pallas_call(
        _hgrn_kernel,
        grid=(2, batch, nc),
        in_specs=[pl.BlockSpec((c, w), lambda d, b, j: (rows(d, b, j), 5)),
                  pl.BlockSpec((c, w), lambda d, b, j: (rows(d, b, j), 6 + d)),
                  pl.BlockSpec((c, w), lambda d, b, j: (rows(d, b, j), 8)),
                  pl.BlockSpec((None, 8, w), lambda d, b, j: (d, 0, 0)),
                  pl.BlockSpec((None, None, HG_HEADS, HEAD_DIM, HEAD_DIM), lambda d, b, j: (b, d, 0, 0, 0)),
                  pl.BlockSpec((None,) + Lm.shape[1:], lambda d, b, j: (d, 0, 0)),
                  pl.BlockSpec((None,) + Mm.shape[1:], lambda d, b, j: (d, 0, 0, 0))],
        out_specs=[pl.BlockSpec((None, c, w), lambda d, b, j: (d, rows(d, b, j) - cb0, 0)),
                   pl.BlockSpec((None, None, HG_HEADS, HEAD_DIM, HEAD_DIM), lambda d, b, j: (b, d, 0, 0, 0))],
        out_shape=[jax.ShapeDtypeStruct((2, batch * n, w), F32),
                   jax.ShapeDtypeStruct((batch, 2, HG_HEADS, HEAD_DIM, HEAD_DIM), F32)],
        scratch_shapes=[pltpu.VMEM((HG_HEADS, HEAD_DIM, HEAD_DIM), F32)],
        compiler_params=_cparams("parallel", "parallel", "arbitrary"),
        name="hgrn_scan",
    )(z, z, z, lbtab, s0, Lm, Mm)


def _out_kernel(x_ref, ga_ref, of_ref, ob_ref, hg_ref, ma_ref, w_ref, on_ref, g2_ref,
                g1_ref, sh2_ref, sc2_ref, xo_ref, hhi_ref, hlo_ref, ht_ref):
    o = of_ref[...] + ob_ref[...]
    g = hg_ref[...]
    g = g * (1.0 / (1.0 + jnp.exp(-g)))
    n_g, n_h = GQA_HEADS * HEAD_DIM, HG_HEADS * HEAD_DIM
    mix = _dot(ga_ref[...], w_ref[0:n_g, :]) + _dot(ma_ref[...], w_ref[n_g + n_h:, :])
    for h in range(HG_HEADS):
        sl = slice(h * HEAD_DIM, (h + 1) * HEAD_DIM)
        oh = _rms(o[:, sl], HEAD_DIM) * on_ref[...] * g[:, sl]
        mix = mix + _dot(oh.astype(BF16), w_ref[n_g + h * HEAD_DIM:n_g + (h + 1) * HEAD_DIM, :])
    x = x_ref[...] + g1_ref[...] * mix
    xo_ref[...] = x
    h2 = _rms(x, x.shape[-1]) * g2_ref[...] * (1.0 + sc2_ref[...]) + sh2_ref[...]
    hi, lo = _split2(h2)
    hhi_ref[...] = hi
    hlo_ref[...] = lo
    ht_ref[...] = h2.T.astype(BF16)


def _mixer_out(x, ga, o2, z, ma, w, on, g2, modv, mod_row_of_tile, tm):
    T, D = x.shape
    wh = HG_HEADS * HEAD_DIM

    def mspec(k):
        return pl.BlockSpec((None, 1, D), lambda i: (mod_row_of_tile(i) + k, 0, 0))

    return pl.pallas_call(
        _out_kernel,
        grid=(T // tm,),
        in_specs=[pl.BlockSpec((tm, D), lambda i: (i, 0)),
                  pl.BlockSpec((tm, ga.shape[1]), lambda i: (i, 0)),
                  pl.BlockSpec((None, tm, wh), lambda i: (0, i, 0)),
                  pl.BlockSpec((None, tm, wh), lambda i: (1, i, 0)),
                  pl.BlockSpec((tm, wh), lambda i: (i, 9)),
                  pl.BlockSpec((tm, ma.shape[1]), lambda i: (i, 0)),
                  pl.BlockSpec(w.shape, lambda i: (0, 0)),
                  pl.BlockSpec((1, HEAD_DIM), lambda i: (0, 0)),
                  pl.BlockSpec((1, D), lambda i: (0, 0)),
                  mspec(2), mspec(3), mspec(4)],
        out_specs=[pl.BlockSpec((tm, D), lambda i: (i, 0))] * 3 + [pl.BlockSpec((D, tm), lambda i: (0, i))],
        out_shape=[jax.ShapeDtypeStruct((T, D), F32), jax.ShapeDtypeStruct((T, D), BF16),
                   jax.ShapeDtypeStruct((T, D), BF16), jax.ShapeDtypeStruct((D, T), BF16)],
        compiler_params=_cparams("parallel"),
        name="mixer_out",
    )(x, ga, o2, o2, z, ma, w, on, g2, modv, modv, modv)


def _ws_kernel(k_ref, wq_ref, hi_ref, lo_ref):
    khi, klo = _split2(k_ref[...])
    whi, wlo = _split2(wq_ref[...])
    acc = _dot_nt(khi, whi) + _dot_nt(klo, whi) + _dot_nt(khi, wlo)
    hi, lo = _split2(acc * LOG2E)
    hi_ref[...] = hi
    lo_ref[...] = lo


def _score_weights(keys, wq):
    L, D, _ = wq.shape
    nb = 2 * PEER_HEADS
    sub = keys.shape[-1]
    kk = keys.reshape(L, nb, PEER_KEYS, sub)
    out = jax.ShapeDtypeStruct((L, nb * PEER_KEYS, D), BF16)
    ospec = pl.BlockSpec((None, PEER_KEYS, D), lambda l, b: (l, (b % 2) * PEER_HEADS + b // 2, 0))
    return pl.pallas_call(
        _ws_kernel,
        grid=(L, nb),
        in_specs=[pl.BlockSpec((None, None, PEER_KEYS, sub), lambda l, b: (l, b, 0, 0)),
                  pl.BlockSpec((None, D, sub), lambda l, b: (l, 0, b))],
        out_specs=[ospec, ospec],
        out_shape=[out, out],
        compiler_params=_cparams("parallel", "parallel"),
        name="score_weights",
    )(kk, wq)


def _score_kernel(whi_ref, wlo_ref, hhi_ref, hlo_ref, o_ref):
    whi, hhi = whi_ref[...], hhi_ref[...]
    o_ref[...] = _dot_nt(whi, hhi) + _dot_nt(whi, hlo_ref[...]) + _dot_nt(wlo_ref[...], hhi)


def _scores(whi, wlo, hhi, hlo, tm):
    R, D = whi.shape
    T = hhi.shape[0]
    tr = min(512, R)
    return pl.pallas_call(
        _score_kernel,
        grid=(R // tr, T // tm),
        in_specs=[pl.BlockSpec((tr, D), lambda r, i: (r, 0)), pl.BlockSpec((tr, D), lambda r, i: (r, 0)),
                  pl.BlockSpec((tm, D), lambda r, i: (i, 0)), pl.BlockSpec((tm, D), lambda r, i: (i, 0))],
        out_specs=pl.BlockSpec((tr, tm), lambda r, i: (r, i)),
        out_shape=jax.ShapeDtypeStruct((R, T), F32),
        compiler_params=_cparams("parallel", "parallel"),
        name="scores",
    )(whi, wlo, hhi, hlo)


def _batcher_pairs(n):
    pairs = []
    p = 1
    while p < n:
        k = p
        while k >= 1:
            for j in range(k % p, n - k, 2 * k):
                for i in range(min(k, n - j - k)):
                    if (i + j) // (2 * p) == (i + j + k) // (2 * p):
                        pairs.append((i + j, i + j + k))
            k //= 2
        p *= 2
    return pairs


_SORT16 = _batcher_pairs(PEER_TOPK)


def _sort_desc(v):
    v = list(v)
    for i, j in _SORT16:
        v[i], v[j] = jnp.maximum(v[i], v[j]), jnp.minimum(v[i], v[j])
    return v


def _merge_top(a, b):
    n = len(a)
    v = [jnp.maximum(a[i], b[n - 1 - i]) for i in range(n)]
    s = n // 2
    while s >= 1:
        for i in range(n):
            if (i & s) == 0:
                v[i], v[i + s] = jnp.maximum(v[i], v[i + s]), jnp.minimum(v[i], v[i + s])
        s //= 2
    return v


def _top16_of_keys(s_ref, row0, lanes):
    k = PEER_TOPK
    groups = [_sort_desc([s_ref[pl.ds(row0 + PEER_HEADS * (k * g + i), PEER_HEADS), lanes] for i in range(k)])
              for g in range(PEER_KEYS // k)]
    while len(groups) > 1:
        groups = [_merge_top(groups[i], groups[i + 1]) for i in range(0, len(groups), 2)]
    return groups[0]


def _topk_kernel(s_ref, thr_ref, s1n_ref):
    k = PEER_TOPK
    half = PEER_HEADS * PEER_KEYS
    ninf = jnp.full((PEER_HEADS, 128), -jnp.inf, F32)

    def chunk(ch, carry):
        lanes = pl.ds(pl.multiple_of(ch * 128, 128), 128)
        a = _top16_of_keys(s_ref, 0, lanes)
        b = _top16_of_keys(s_ref, half, lanes)
        cand = [[a[i] + b[j] for j in range(k // (i + 1))] for i in range(k)]
        g0 = cand[0]
        g1 = _sort_desc(cand[1] + cand[2] + cand[5] + cand[8])
        g2 = _sort_desc(cand[3] + cand[4] + cand[6] + cand[7] + cand[9] + cand[10] + cand[11] + cand[12] + cand[13])
        g3 = cand[14] + cand[15] + [ninf] * (k - 2)
        top = _merge_top(_merge_top(g0, g1), _merge_top(g2, g3))
        zsum = jnp.zeros((PEER_HEADS, 128), F32)
        for t in top:
            zsum = zsum + jnp.exp2(t - top[0])
        c0 = top[0] + jnp.log2(zsum)
        thr_ref[:, lanes] = top[k - 1] - c0
        for key in range(PEER_KEYS):
            tile = s_ref[pl.ds(half + PEER_HEADS * key, PEER_HEADS), lanes] - c0
            s1n_ref[ch, pl.ds(key, PEER_HEADS, stride=PEER_KEYS), :] = tile
        return carry

    lax.fori_loop(0, s_ref.shape[1] // 128, chunk, 0)


def _topk(st, tm):
    R, T = st.shape
    return pl.pallas_call(
        _topk_kernel,
        grid=(T // tm,),
        in_specs=[pl.BlockSpec((R, tm), lambda i: (0, i))],
        out_specs=[pl.BlockSpec((PEER_HEADS, tm), lambda i: (0, i)),
                   pl.BlockSpec((tm // 128, R // 2, 128), lambda i: (i, 0, 0))],
        out_shape=[jax.ShapeDtypeStruct((PEER_HEADS, T), F32), jax.ShapeDtypeStruct((T // 128, R // 2, 128), F32)],
        compiler_params=_cparams("parallel"),
        name="topk",
    )(st)


def _peer_kernel(ht_ref, s0_ref, s1n_ref, thr_ref, u_ref, vt_ref, x_ref, g2_ref, o_ref,
                 pre_a, pre_b, c_a, c_b, acc_sc):
    s = pl.program_id(1)
    te, tm = c_a.shape
    nk = PEER_KEYS

    @pl.when(s == 0)
    def _():
        acc_sc[...] = jnp.zeros_like(acc_sc)
        pre_b[...] = jnp.zeros_like(pre_b)
        c_a[...] = jnp.zeros_like(c_a)

    d = ht_ref.shape[0]
    nblk = (te // nk) * (tm // 128)
    nh = 2 if tm % 256 == 0 else 1
    th = tm // nh
    ka = max(1, nblk // (2 * nh))
    mc = max(1, nblk // (4 * nh))
    kq, mq = d // ka, d // mc

    def pre_piece(p, pre_w):
        half, q = divmod(p, ka)
        part = _dot(u_ref[:, q * kq:(q + 1) * kq], ht_ref[q * kq:(q + 1) * kq, half * th:(half + 1) * th])
        if q == 0:
            pre_w[:, half * th:(half + 1) * th] = part
        else:
            pre_w[:, half * th:(half + 1) * th] += part

    def value_piece(p, c_r):
        half, q = divmod(p, mc)
        acc_sc[q * mq:(q + 1) * mq, half * th:(half + 1) * th] += _dot(
            vt_ref[q * mq:(q + 1) * mq, :], c_r[:, half * th:(half + 1) * th])

    slots = {}
    for p in range(nh * ka):
        slots.setdefault(p * nblk // (nh * ka), []).append((pre_piece, p))
    for p in range(nh * mc):
        slots.setdefault((p * nblk // (nh * mc) + 1) % nblk, []).append((value_piece, p))

    def stages(pre_w, pre_r, c_w, c_r):
        for blk in range(nblk):
            for piece, p in slots.get(blk, ()):
                piece(p, pre_w if piece is pre_piece else c_r)
            ii, ch = divmod(blk, tm // 128)
            lanes = slice(ch * 128, (ch + 1) * 128)
            gate = jnp.zeros((nk, 128), F32)
            for h in range(PEER_HEADS):
                r0 = ii * PEER_HEADS + h
                val = s1n_ref[ch, h * nk:(h + 1) * nk, :] + s0_ref[r0:r0 + 1, lanes]
                gate = gate + jnp.where(val >= thr_ref[h:h + 1, lanes], jnp.exp2(val), 0.0)
            pre = pre_r[ii * nk:(ii + 1) * nk, lanes]
            act = 0.5 * pre * (1.0 + lax.erf(pre * (2.0 ** -0.5)))
            c_w[ii * nk:(ii + 1) * nk, lanes] = (gate * act).astype(BF16)

    @pl.when(s % 2 == 0)
    def _():
        stages(pre_a, pre_b, c_b, c_a)

    @pl.when(s % 2 == 1)
    def _():
        stages(pre_b, pre_a, c_a, c_b)

    @pl.when(s == pl.num_programs(1) - 1)
    def _():
        o_ref[...] = x_ref[...] + g2_ref[...] * acc_sc[...].T


def _peer(ht, st, thr, s1n, u, vt, x, modv, mod_row_of_tile, tm):
    T, D = x.shape
    ne, _, te = vt.shape
    half = PEER_HEADS * PEER_KEYS
    tile = lambda s, lag: jnp.clip(s - lag, 0, ne - 1)
    return pl.pallas_call(
        _peer_kernel,
        grid=(T // tm, ne + 2),
        in_specs=[pl.BlockSpec((D, tm), lambda i, s: (0, i)),
                  pl.BlockSpec((te // PEER_KEYS * PEER_HEADS, tm), lambda i, s: (tile(s, 1), i)),
                  pl.BlockSpec((tm // 128, half, 128), lambda i, s: (i, 0, 0)),
                  pl.BlockSpec((PEER_HEADS, tm), lambda i, s: (0, i)),
                  pl.BlockSpec((te, D), lambda i, s: (tile(s, 0), 0)),
                  pl.BlockSpec((None, D, te), lambda i, s: (tile(s, 2), 0, 0)),
                  pl.BlockSpec((tm, D), lambda i, s: (i, 0)),
                  pl.BlockSpec((None, 1, D), lambda i, s: (mod_row_of_tile(i) + 5, 0, 0))],
        out_specs=pl.BlockSpec((tm, D), lambda i, s: (i, 0)),
        out_shape=jax.ShapeDtypeStruct((T, D), F32),
        scratch_shapes=[pltpu.VMEM((te, tm), F32), pltpu.VMEM((te, tm), F32), pltpu.VMEM((te, tm), BF16),
                        pltpu.VMEM((te, tm), BF16), pltpu.VMEM((D, tm), F32)],
        compiler_params=_cparams("parallel", "arbitrary"),
        name="peer_experts",
    )(ht, st, s1n, thr, u, vt, x, modv)


def _rope_tables(n, ident_rows):
    pos = jnp.arange(n, dtype=jnp.int32)
    row, col = pos // GRID_W, pos % GRID_W
    lane = jnp.arange(HEAD_DIM)

    def build(nfreq, width):
        inv = ROPE_THETA ** (-jnp.arange(nfreq, dtype=F32) / nfreq)
        ar = row.astype(F32)[:, None] * inv[None, :]
        ac = col.astype(F32)[:, None] * inv[None, :]
        cos = jnp.concatenate([jnp.cos(ar), jnp.cos(ar), jnp.cos(ac), jnp.cos(ac)], axis=-1)
        sin = jnp.concatenate([jnp.sin(ar), jnp.sin(ar), jnp.sin(ac), jnp.sin(ac)], axis=-1)
        pad = HEAD_DIM - width
        cos = jnp.pad(cos, ((0, 0), (0, pad)))
        sin = jnp.pad(sin, ((0, 0), (0, pad)))
        first = (lane % (2 * nfreq)) < nfreq
        tab = jnp.stack([cos, jnp.where(first, 0.0, sin), jnp.where(first, -sin, 0.0)])
        ident = jnp.stack([jnp.ones((ident_rows, HEAD_DIM), F32), jnp.zeros((ident_rows, HEAD_DIM), F32),
                           jnp.zeros((ident_rows, HEAD_DIM), F32)])
        return jnp.concatenate([ident, tab], axis=1)

    return build(HEAD_DIM // 4, HEAD_DIM), build(MLA_ROPE // 4, MLA_ROPE)


def kernel(x_prompt, x_sample, c, cache_gqa_k, cache_gqa_v, cache_mla_ckv, cache_mla_krope, state_hgrn, c_ctx,
           w_mod, b_mod, norm1_g, norm2_g, w_in, gqa_qn, gqa_kn, mla_qa_n, mla_kva_n, w_mla_qb, w_mla_kvb,
           mla_qn, mla_kn, hg_lb_logits, hg_on, w_out, peer_wq, peer_keys, peer_u, peer_v):
    B, S, D = x_prompt.shape
    DB, DS, _ = x_sample.shape
    L = w_mod.shape[0]
    P = cache_gqa_k.shape[2]
    TC, TL = B * S, DB * DS
    T = TC + TL
    tm = 512 if (TC % 512 == 0 and DS % 512 == 0) else 128
    tm_small = min(256, tm)

    ngroups = 1 + DB

    def mod_row_of(t, layer):
        def f(i):
            group = jnp.where(i < TC // t, 0, 1 + (i - TC // t) // (DS // t))
            return (layer * ngroups + group) * 6
        return f

    def table_of(t):
        def f(i):
            return jnp.where(i < TC // t, 0, 1 + (i - TC // t) % (DS // t))
        return f

    cond8 = jnp.zeros((8, D), F32).at[0].set(c_ctx).at[1:1 + DB].set(c)
    modv = _modulation(cond8, w_mod, b_mod)[:, :ngroups].reshape(L * ngroups * 6, 1, D)

    lb_cum = jnp.cumsum(jax.nn.softmax(hg_lb_logits.astype(F32), axis=0), axis=0)
    lbs = jnp.maximum(lb_cum - lb_cum[:1], 0.0)
    lbtab = jnp.stack([jnp.log(lbs), jnp.log1p(-lbs), 1.0 - lbs] + [jnp.zeros_like(lbs)] * 5, axis=2)

    sizes = np.cumsum([0, 768, 256, 256, 512, 512, 512, 512, 512, 512, 512, 64])
    part = lambda a, k: a[..., sizes[k]:sizes[k + 1]]
    order = (0, 1, 2, 10, 8, 9, 3, 4, 5, 6, 7)
    cols = []
    for k in order:
        cols.append(part(w_in, k))
        if k == 10:
            cols.append(jnp.zeros((L, D, 256 - MLA_ROPE), w_in.dtype))
    w_in_b = jnp.concatenate(cols, axis=-1).astype(BF16)
    w_in_b = jnp.swapaxes(w_in_b.reshape(L, D, -1, IN_TILE), 1, 2)

    wqb = w_mla_qb.reshape(L, MLA_RANK, MLA_HEADS, MLA_QK)
    wqb = jnp.pad(wqb, ((0, 0), (0, 0), (0, 0), (0, MLA_PAD - MLA_QK))).reshape(L, MLA_RANK, MLA_HEADS * MLA_PAD).astype(BF16)
    wkvb = w_mla_kvb.reshape(L, MLA_RANK, MLA_HEADS, 2, HEAD_DIM)
    wkvb = jnp.moveaxis(wkvb, 3, 2).reshape(L, MLA_RANK, 2 * MLA_HEADS * HEAD_DIM).astype(BF16)
    rpad = lambda g: jnp.pad(g[:, HEAD_DIM:], ((0, 0), (0, HEAD_DIM - MLA_ROPE)))[:, None, :]
    mqn_n, mqn_r = mla_qn[:, None, :HEAD_DIM], rpad(mla_qn)
    mkn_n, mkn_r = mla_kn[:, None, :HEAD_DIM], rpad(mla_kn)
    w_out_b = w_out.astype(BF16)
    u_b = peer_u.astype(BF16)
    vt_b = jnp.swapaxes(peer_v.reshape(L, -1, PEER_EXPERT_TILE, D), 2, 3).astype(BF16)
    def key_major(w):
        return jnp.swapaxes(w.reshape(L, 2, PEER_HEADS, PEER_KEYS, D), 2, 3).reshape(L, 2 * PEER_HEADS * PEER_KEYS, D)

    ws_hi, ws_lo = map(key_major, _score_weights(peer_keys, peer_wq))

    tabg, tabm = _rope_tables(DS, tm_small)

    cck = jnp.moveaxis(cache_mla_ckv, 1, 0).reshape(L, DB * P, MLA_RANK)
    ckr = jnp.pad(jnp.moveaxis(cache_mla_krope, 1, 0).reshape(L, DB * P, MLA_ROPE), ((0, 0), (0, 0), (0, HEAD_DIM - MLA_ROPE)))
    cmk, cmv = _cache_expand(cck, ckr, wkvb, mkn_n, mkn_r)
    cgk = jnp.moveaxis(cache_gqa_k, 1, 0).reshape(L, DB, P, -1).astype(BF16)
    cgv = jnp.transpose(cache_gqa_v.reshape(DB, L, P, -1), (1, 3, 0, 2)).astype(BF16)
    cmk = cmk.reshape(L, DB, P, -1)
    cmv = cmv.reshape(L, -1, DB, P)
    s0_ctx = jnp.zeros((B, 2, HG_HEADS, HEAD_DIM, HEAD_DIM), F32)

    x = jnp.concatenate([x_prompt.reshape(TC, D), x_sample.reshape(TL, D)], axis=0)
    new_k, new_v, new_ckv, new_kr, new_s = [], [], [], [], []
    tq_c = min(256, S)
    tq_l, tk_l = min(1024, DS), min(512, P, DS)
    for l in range(L):
        z = _in_proj(x, norm1_g[l][None], modv, w_in_b[l], mod_row_of(tm, l), tm)
        gq, gkn, gkr, gvt, mq, ckv, mk, mvt = _prep(
            z, tabg, tabm, table_of(tm_small), tm_small, gqa_qn[l][None], gqa_kn[l][None], mla_qa_n[l][None],
            mla_kva_n[l][None], wqb[l], wkvb[l], mqn_n[l], mqn_r[l], mkn_n[l], mkn_r[l])
        new_k.append(gkn[:TC].reshape(B, S, GQA_KV_HEADS, HEAD_DIM))
        new_v.append(z[:TC, 1024:1280].reshape(B, S, GQA_KV_HEADS, HEAD_DIM))
        new_ckv.append(ckv[:TC].reshape(B, S, MLA_RANK))
        new_kr.append(z[:TC, 1280:1280 + MLA_ROPE].reshape(B, S, MLA_ROPE))

        def lat_keys(cache, new):
            return jnp.concatenate([cache, new[TC:].reshape(DB, DS, -1)], axis=1).reshape(DB * (P + DS), -1)

        def lat_values(cache_t, new_t):
            w = new_t.shape[0]
            return jnp.concatenate([cache_t, new_t[:, TC:].reshape(w, DB, DS)], axis=2).reshape(w, DB * (P + DS))

        gqa = dict(heads=GQA_HEADS, group=GQA_HEADS // GQA_KV_HEADS, dq=HEAD_DIM, dv=HEAD_DIM, hb=GQA_HEADS // GQA_KV_HEADS)
        mla = dict(heads=MLA_HEADS, group=1, dq=MLA_PAD, dv=HEAD_DIM, hb=MLA_HEADS // 2)
        ctx_a = dict(batch=B, nq=S, nk=S, q_row0=0, k_row0=0, tq=tq_c, tk=tq_c)
        lat_a = dict(batch=DB, nq=DS, nk=P + DS, q_row0=TC, k_row0=0, tq=tq_l, tk=tk_l)
        ga = jnp.concatenate([_attention(gq, gkr, gvt, **gqa, **ctx_a),
                              _attention(gq, lat_keys(cgk[l], gkr), lat_values(cgv[l], gvt), **gqa, **lat_a)], axis=0)
        ma = jnp.concatenate([_attention(mq, mk, mvt, **mla, **ctx_a),
                              _attention(mq, lat_keys(cmk[l], mk), lat_values(cmv[l], mvt), **mla, **lat_a)], axis=0)

        o_c, s_c = _hgrn(z, lbtab[l], s0_ctx, batch=B, n=S, row0=0)
        o_l, _ = _hgrn(z, lbtab[l], state_hgrn[:, l].astype(F32), batch=DB, n=DS, row0=TC)
        new_s.append(s_c)
        o2 = jnp.concatenate([o_c, o_l], axis=1)

        x, hhi, hlo, ht = _mixer_out(x, ga, o2, z, ma, w_out_b[l], hg_on[l][None], norm2_g[l][None], modv,
                                 mod_row_of(tm_small, l), tm_small)
        st = _scores(ws_hi[l], ws_lo[l], hhi, hlo, tm)
        thr, s1n = _topk(st, tm)
        x = _peer(ht, st, thr, s1n, u_b[l], vt_b[l], x, modv, mod_row_of(tm, l), tm)

    return (x[:TC].reshape(B, S, D), x[TC:].reshape(DB, DS, D),
            jnp.stack(new_k, axis=1), jnp.stack(new_v, axis=1), jnp.stack(new_ckv, axis=1),
            jnp.stack(new_kr, axis=1), jnp.stack(new_s, axis=1))
```

```python
import functools

import numpy as np
import jax
import jax.numpy as jnp
from jax import lax
from jax.experimental import pallas as pl
from jax.experimental.pallas import tpu as pltpu

F32 = jnp.float32
BF16 = jnp.bfloat16

EPS = 1e-6
LOG2E = 1.4426950408889634
ROPE_THETA = 10000.0
GRID_W = 64

HEAD_DIM = 128
GQA_HEADS, GQA_KV_HEADS = 6, 2
HG_HEADS = 4
MLA_HEADS, MLA_ROPE, MLA_RANK = 6, 64, 512
MLA_QK = HEAD_DIM + MLA_ROPE
MLA_PAD = 2 * HEAD_DIM
PEER_HEADS, PEER_KEYS, PEER_TOPK = 8, 128, 16
HG_CHUNK = 128
PEER_EXPERT_TILE = 512
IN_TILE = 1024

VMEM_LIMIT = 60 * 1024 * 1024


def _cparams(*sem):
    return pltpu.CompilerParams(dimension_semantics=sem, vmem_limit_bytes=VMEM_LIMIT)


def _dot(a, b):
    return jnp.dot(a, b, preferred_element_type=F32)


def _dot_nt(a, b):
    return lax.dot_general(a, b, (((1,), (1,)), ((), ())), preferred_element_type=F32)


def _dot_tn(a, b):
    return lax.dot_general(a, b, (((0,), (0,)), ((), ())), preferred_element_type=F32)


def _split2(a):
    hi = a.astype(BF16)
    lo = (a - hi.astype(F32)).astype(BF16)
    return hi, lo


def _split3(a):
    p1 = a.astype(BF16)
    r = a - p1.astype(F32)
    p2 = r.astype(BF16)
    p3 = (r - p2.astype(F32)).astype(BF16)
    return p1, p2, p3


def _rms(x, n):
    return x * lax.rsqrt(jnp.sum(x * x, axis=-1, keepdims=True) * (1.0 / n) + EPS)


def _mod_kernel(cond_ref, w_ref, b_ref, o_ref):
    c = cond_ref[...]
    s = c * (1.0 / (1.0 + jnp.exp(-c)))
    shi, slo = _split2(s)
    whi, wlo = _split2(w_ref[...])
    o_ref[...] = _dot(shi, whi) + _dot(slo, whi) + _dot(shi, wlo) + b_ref[...]


def _modulation(cond8, w_mod, b_mod):
    L, D, N = w_mod.shape
    tn = min(512, N)
    return pl.pallas_call(
        _mod_kernel,
        grid=(L, N // tn),
        in_specs=[pl.BlockSpec((8, D), lambda l, j: (0, 0)),
                  pl.BlockSpec((None, D, tn), lambda l, j: (l, 0, j)),
                  pl.BlockSpec((None, 1, tn), lambda l, j: (l, 0, j))],
        out_specs=pl.BlockSpec((None, 8, tn), lambda l, j: (l, 0, j)),
        out_shape=jax.ShapeDtypeStruct((L, 8, N), F32),
        compiler_params=_cparams("parallel", "parallel"),
        name="modulation",
    )(cond8, w_mod, b_mod.reshape(L, 1, N))


def _in_kernel(x_ref, g_ref, sh_ref, sc_ref, w_ref, o_ref, h_sc):
    @pl.when(pl.program_id(1) == 0)
    def _():
        x = x_ref[...]
        h = _rms(x, x.shape[-1]) * g_ref[...] * (1.0 + sc_ref[...]) + sh_ref[...]
        h_sc[...] = h.astype(BF16)

    o_ref[...] = _dot(h_sc[...], w_ref[...])


def _in_proj(x, g, modv, w, mod_row_of_tile, tm):
    T, D = x.shape
    nj, _, tn = w.shape
    N = nj * tn

    def mspec(k):
        return pl.BlockSpec((None, 1, D), lambda i, j: (mod_row_of_tile(i) + k, 0, 0))

    return pl.pallas_call(
        _in_kernel,
        grid=(T // tm, nj),
        in_specs=[pl.BlockSpec((tm, D), lambda i, j: (i, 0)),
                  pl.BlockSpec((1, D), lambda i, j: (0, 0)),
                  mspec(0), mspec(1),
                  pl.BlockSpec((None, D, tn), lambda i, j: (j, 0, 0))],
        out_specs=pl.BlockSpec((tm, tn), lambda i, j: (i, j)),
        out_shape=jax.ShapeDtypeStruct((T, N), F32),
        scratch_shapes=[pltpu.VMEM((tm, D), BF16)],
        compiler_params=_cparams("parallel", "arbitrary"),
        name="in_proj",
    )(x, g, modv, modv, w)


def _rope(x, tab_ref, blk):
    return (x * tab_ref[0] + pltpu.roll(x, blk, 1) * tab_ref[1]
            + pltpu.roll(x, HEAD_DIM - blk, 1) * tab_ref[2])


def _mla_kv(ckv, krope, wkvb_ref, gkn_ref, gkr_ref, tab_ref, mk_ref, mv_ref):
    kv = _dot(ckv.astype(BF16), wkvb_ref[...])
    nv = MLA_HEADS * HEAD_DIM
    mv_ref[...] = kv[:, nv:].T.astype(BF16)
    ssr = jnp.sum(krope * krope, axis=-1, keepdims=True)
    for h in range(MLA_HEADS):
        kn = kv[:, h * HEAD_DIM:(h + 1) * HEAD_DIM]
        rs = lax.rsqrt((jnp.sum(kn * kn, axis=-1, keepdims=True) + ssr) * (1.0 / MLA_QK) + EPS)
        kr = krope * rs * gkr_ref[...]
        if tab_ref is not None:
            kr = _rope(kr, tab_ref, MLA_ROPE // 4)
        mk_ref[:, h * MLA_PAD:h * MLA_PAD + HEAD_DIM] = (kn * rs * gkn_ref[...]).astype(BF16)
        mk_ref[:, h * MLA_PAD + HEAD_DIM:(h + 1) * MLA_PAD] = kr.astype(BF16)


def _prep_kernel(gq_ref, gk_ref, gv_ref, mkr_ref, mqa_ref, mkva_ref, tg_ref, tm_ref,
                 gqn_ref, gkn_ref, qan_ref, kvan_ref, wqb_ref, wkvb_ref,
                 mqn_n_ref, mqn_r_ref, mkn_n_ref, mkn_r_ref,
                 gq_o, gkn_o, gkr_o, gvt_o, mq_o, ckv_o, mk_o, mv_o):
    gscale = HEAD_DIM ** -0.5 * LOG2E
    for h in range(GQA_HEADS):
        sl = slice(h * HEAD_DIM, (h + 1) * HEAD_DIM)
        q = _rms(gq_ref[:, sl], HEAD_DIM) * gqn_ref[...]
        gq_o[:, sl] = (_rope(q, tg_ref, HEAD_DIM // 4) * gscale).astype(BF16)
    for h in range(GQA_KV_HEADS):
        sl = slice(h * HEAD_DIM, (h + 1) * HEAD_DIM)
        k = _rms(gk_ref[:, sl], HEAD_DIM) * gkn_ref[...]
        gkn_o[:, sl] = k
        gkr_o[:, sl] = _rope(k, tg_ref, HEAD_DIM // 4).astype(BF16)
    gvt_o[...] = gv_ref[...].T.astype(BF16)

    qa = _rms(mqa_ref[...], MLA_RANK) * qan_ref[...]
    mq = _dot(qa.astype(BF16), wqb_ref[...])
    mscale = MLA_QK ** -0.5 * LOG2E
    for h in range(MLA_HEADS):
        qn = mq[:, h * MLA_PAD:h * MLA_PAD + HEAD_DIM]
        qr = mq[:, h * MLA_PAD + HEAD_DIM:(h + 1) * MLA_PAD]
        ss = jnp.sum(qn * qn, axis=-1, keepdims=True) + jnp.sum(qr * qr, axis=-1, keepdims=True)
        rs = lax.rsqrt(ss * (1.0 / MLA_QK) + EPS) * mscale
        mq_o[:, h * MLA_PAD:h * MLA_PAD + HEAD_DIM] = (qn * rs * mqn_n_ref[...]).astype(BF16)
        mq_o[:, h * MLA_PAD + HEAD_DIM:(h + 1) * MLA_PAD] = _rope(qr * rs * mqn_r_ref[...], tm_ref, MLA_ROPE // 4).astype(BF16)

    ckv = _rms(mkva_ref[...], MLA_RANK) * kvan_ref[...]
    ckv_o[...] = ckv
    _mla_kv(ckv, mkr_ref[...], wkvb_ref, mkn_n_ref, mkn_r_ref, tm_ref, mk_o, mv_o)


def _prep(z, tabg, tabm, table_of_tile, tm, gqn, gkn, qan, kvan, wqb, wkvb, mqn_n, mqn_r, mkn_n, mkn_r):
    T = z.shape[0]
    row = lambda w: pl.BlockSpec((1, w), lambda i: (0, 0))
    full = lambda a: pl.BlockSpec(a.shape, lambda i: (0, 0))
    tab = pl.BlockSpec((3, tm, HEAD_DIM), lambda i: (0, table_of_tile(i), 0))
    zspec = lambda w, blk: pl.BlockSpec((tm, w), lambda i: (i, blk))
    widths = (GQA_HEADS * HEAD_DIM, GQA_KV_HEADS * HEAD_DIM, GQA_KV_HEADS * HEAD_DIM, GQA_KV_HEADS * HEAD_DIM,
              MLA_HEADS * MLA_PAD, MLA_RANK, MLA_HEADS * MLA_PAD, MLA_HEADS * HEAD_DIM)
    dtypes = (BF16, F32, BF16, BF16, BF16, F32, BF16, BF16)
    transposed = (False, False, False, True, False, False, False, True)
    ospec = lambda w, t: pl.BlockSpec((w, tm), lambda i: (0, i)) if t else pl.BlockSpec((tm, w), lambda i: (i, 0))
    return pl.pallas_call(
        _prep_kernel,
        grid=(T // tm,),
        in_specs=[zspec(768, 0), zspec(256, 3), zspec(256, 4), zspec(128, 10), zspec(512, 3), zspec(512, 4),
                  tab, tab, row(128), row(128), row(512), row(512), full(wqb), full(wkvb),
                  row(128), row(128), row(128), row(128)],
        out_specs=[ospec(w, t) for w, t in zip(widths, transposed)],
        out_shape=[jax.ShapeDtypeStruct((w, T) if t else (T, w), d) for w, d, t in zip(widths, dtypes, transposed)],
        compiler_params=_cparams("parallel"),
        name="prep",
    )(z, z, z, z, z, z, tabg, tabm, gqn, gkn, qan, kvan, wqb, wkvb, mqn_n, mqn_r, mkn_n, mkn_r)


def _cache_kernel(ckv_ref, kr_ref, wkvb_ref, gkn_ref, gkr_ref, mk_o, mv_o):
    _mla_kv(ckv_ref[...], kr_ref[...], wkvb_ref, gkn_ref, gkr_ref, None, mk_o, mv_o)


def _cache_expand(ckv, kr, wkvb, mkn_n, mkn_r):
    L, R, _ = ckv.shape
    tm = min(512, R)
    return pl.pallas_call(
        _cache_kernel,
        grid=(L, R // tm),
        in_specs=[pl.BlockSpec((None, tm, MLA_RANK), lambda l, i: (l, i, 0)),
                  pl.BlockSpec((None, tm, HEAD_DIM), lambda l, i: (l, i, 0)),
                  pl.BlockSpec((None,) + wkvb.shape[1:], lambda l, i: (l, 0, 0)),
                  pl.BlockSpec((None, 1, HEAD_DIM), lambda l, i: (l, 0, 0)),
                  pl.BlockSpec((None, 1, HEAD_DIM), lambda l, i: (l, 0, 0))],
        out_specs=[pl.BlockSpec((None, tm, MLA_HEADS * MLA_PAD), lambda l, i: (l, i, 0)),
                   pl.BlockSpec((None, MLA_HEADS * HEAD_DIM, tm), lambda l, i: (l, 0, i))],
        out_shape=[jax.ShapeDtypeStruct((L, R, MLA_HEADS * MLA_PAD), BF16),
                   jax.ShapeDtypeStruct((L, MLA_HEADS * HEAD_DIM, R), BF16)],
        compiler_params=_cparams("parallel", "parallel"),
        name="cache_expand",
    )(ckv, kr, wkvb, mkn_n, mkn_r)


def _attn_kernel(*refs, heads, group, dq, dv, ncb, has_prev):
    n_in = (5 if ncb else 3) + (1 if has_prev else 0)
    q_ref = refs[0]
    o_ref, m_sc, l_sc, acc_sc = refs[n_in:]
    ki = pl.program_id(3)

    @pl.when(ki == 0)
    def _():
        m_sc[...] = jnp.full_like(m_sc, -jnp.inf)
        l_sc[...] = jnp.zeros_like(l_sc)
        acc_sc[...] = jnp.zeros_like(acc_sc)

    def step(k_ref, vt_ref):
        def scores(h):
            kv = h // group
            return _dot_nt(k_ref[:, kv * dq:(kv + 1) * dq], q_ref[:, h * dq:(h + 1) * dq])

        s_next = scores(0)
        for h in range(heads):
            kv = h // group
            s = s_next
            if h + 1 < heads:
                s_next = scores(h + 1)
            m_prev = m_sc[h]
            m_new = jnp.maximum(m_prev, jnp.max(s, axis=0, keepdims=True))
            p = jnp.exp2(s - m_new)
            alpha = jnp.exp2(m_prev - m_new)
            l_sc[h] = alpha * l_sc[h] + jnp.sum(p, axis=0, keepdims=True)
            acc_sc[h] = alpha * acc_sc[h] + _dot(vt_ref[kv * dv:(kv + 1) * dv, :], p.astype(BF16))
            m_sc[h] = m_new

    if ncb:
        pl.when(ki < ncb)(lambda: step(refs[1], refs[2]))
        pl.when(ki >= ncb)(lambda: step(refs[3], refs[4]))
    else:
        step(refs[1], refs[2])

    @pl.when(ki == pl.num_programs(3) - 1)
    def _():
        for h in range(heads):
            o_ref[:, h * dv:(h + 1) * dv] = (acc_sc[h] / l_sc[h]).T.astype(o_ref.dtype)


def _attention(q, k, vt, *, batch, nq, nk, row0, heads, group, dq, dv, tq, tk, hb, cache=None, prev=None):
    T = q.shape[0]
    nqb, nnb = nq // tq, nk // tk
    ncb = 0 if cache is None else cache[0].shape[0] // batch // tk
    qb0, kb0 = row0 // tq, row0 // tk
    kvb = max(1, hb // group)
    new_blk = lambda b, j: kb0 + b * nnb + jnp.maximum(j - ncb, 0)
    in_specs = [pl.BlockSpec((tq, hb * dq), lambda b, h, i, j: (qb0 + b * nqb + i, h))]
    operands = [q]
    if ncb:
        cache_blk = lambda b, j: b * ncb + jnp.minimum(j, ncb - 1)
        in_specs += [pl.BlockSpec((tk, kvb * dq), lambda b, h, i, j: (cache_blk(b, j), h)),
                     pl.BlockSpec((kvb * dv, tk), lambda b, h, i, j: (h, cache_blk(b, j)))]
        operands += list(cache)
    in_specs += [pl.BlockSpec((tk, kvb * dq), lambda b, h, i, j: (new_blk(b, j), h)),
                 pl.BlockSpec((kvb * dv, tk), lambda b, h, i, j: (h, new_blk(b, j)))]
    operands += [k, vt]
    aliases = {}
    if prev is not None:
        in_specs.append(pl.BlockSpec(memory_space=pl.ANY))
        aliases = {len(operands): 0}
        operands.append(prev)
    return pl.pallas_call(
        functools.partial(_attn_kernel, heads=hb, group=group, dq=dq, dv=dv, ncb=ncb, has_prev=prev is not None),
        grid=(batch, heads // hb, nqb, ncb + nnb),
        in_specs=in_specs,
        out_specs=pl.BlockSpec((tq, hb * dv), lambda b, h, i, j: (qb0 + b * nqb + i, h)),
        out_shape=jax.ShapeDtypeStruct((T, heads * dv), BF16),
        scratch_shapes=[pltpu.VMEM((hb, 1, tq), F32), pltpu.VMEM((hb, 1, tq), F32), pltpu.VMEM((hb, dv, tq), F32)],
        input_output_aliases=aliases,
        compiler_params=_cparams("parallel", "parallel", "parallel", "arbitrary"),
        name="attention",
    )(*operands)


def _hgrn_constants(c):
    levels = int(np.log2(c))
    tri = np.zeros((2, c, c), np.float32)
    M = np.zeros((2, levels, c, c), np.float32)
    tri[0] = np.tril(np.ones((c, c), np.float32))
    for lv in range(levels):
        m = c >> (lv + 1)
        for a in range(0, c, 2 * m):
            M[0, lv, a + m:a + 2 * m, a:a + m] = 1.0
    tri[1] = tri[0][::-1, ::-1]
    M[1] = M[0][:, ::-1, ::-1]
    return tri, M


def _level_exponents(bq, bwd, c):
    rowid = lax.broadcasted_iota(jnp.int32, bq.shape, 0)
    out = []
    m = c // 2
    while m >= 1:
        if 2 * m >= 8:
            blocks = [jnp.broadcast_to(jnp.where(bwd, bq[a + m:a + m + 1, :], bq[a + m - 1:a + m, :]), (2 * m, bq.shape[1]))
                      for a in range(0, c, 2 * m)]
            ref = blocks[0] if len(blocks) == 1 else jnp.concatenate(blocks, axis=0)
        else:
            target = m - 1 + bwd.astype(jnp.int32)
            rowmod = rowid % (2 * m)
            ref = bq
            for delta in range(-m, m + 1):
                if delta != 0:
                    ref = jnp.where(rowmod + delta == target, pltpu.roll(bq, (-delta) % c, 0), ref)
        out.append(-jnp.abs(bq - ref))
        m //= 2
    return out


def _hgrn_kernel(q_ref, z_ref, v_ref, lb_ref, s0_ref, L_ref, M_ref, o_ref, sf_ref, s_sc):
    c = q_ref.shape[0]
    levels = M_ref.shape[0]

    @pl.when(pl.program_id(2) == 0)
    def _():
        s_sc[...] = s0_ref[...]

    z = z_ref[...]
    e = jnp.exp(-jnp.abs(z))
    r = 1.0 / (1.0 + e)
    logsig = jnp.minimum(z, 0.0) + jnp.log(r)
    a = lb_ref[0:1, :]
    cc = lb_ref[1:2, :] + logsig
    logf = jnp.maximum(a, cc) + jnp.log(1.0 + jnp.exp(-jnp.abs(a - cc)))
    kk = lb_ref[2:3, :] * jnp.where(z >= 0.0, e * r, r)

    tri = L_ref[...]
    p1, p2, p3 = _split3(logf)
    bq = _dot(tri, p1) + _dot(tri, p2) + _dot(tri, p3)
    tot = jnp.minimum(bq[0:1, :], bq[c - 1:c, :])
    ex_q = jnp.exp(bq)
    ex_k = jnp.exp(tot - bq)
    ex_lv = [jnp.exp(t) for t in _level_exponents(bq, pl.program_id(0) == 1, c)]
    ones = jnp.ones((c, HEAD_DIM), BF16)
    for h in range(HG_HEADS):
        sl = slice(h * HEAD_DIM, (h + 1) * HEAD_DIM)
        q, k, v = q_ref[:, sl], kk[:, sl], v_ref[:, sl]
        vb = v.astype(BF16)
        amat = jnp.zeros((c, c), F32)
        for lv in range(levels):
            el = ex_lv[lv][:, sl]
            amat = amat + M_ref[lv] * _dot_nt((q * el).astype(BF16), (k * el).astype(BF16))
        s_prev = s_sc[h]
        o = (_dot(amat.astype(BF16), vb) + jnp.sum(q * k, axis=-1, keepdims=True) * v
             + _dot((q * ex_q[:, sl]).astype(BF16), s_prev.astype(BF16)))
        o_ref[:, sl] = o
        dcol = jnp.exp(_dot_tn(p1[:, sl], ones) + _dot_tn(p2[:, sl], ones) + _dot_tn(p3[:, sl], ones))
        s_new = dcol * s_prev + _dot_tn((k * ex_k[:, sl]).astype(BF16), vb)
        s_sc[h] = s_new
        sf_ref[h] = s_new


def _hgrn(z, lbtab, s0, *, batch, n, row0, prev=None):
    c = min(HG_CHUNK, n)
    nc = n // c
    cb0 = row0 // c
    Lm, Mm = _hgrn_constants(c)
    Lm, Mm = jnp.asarray(Lm, BF16), jnp.asarray(Mm, F32)
    w = HG_HEADS * HEAD_DIM

    def rows(d, b, j):
        return cb0 + b * nc + j + d * (nc - 1 - 2 * j)

    kernel, extra_specs, extra, aliases = _hgrn_kernel, [], [], {}
    if prev is not None:
        kernel = lambda *refs: _hgrn_kernel(*refs[:7], *refs[8:])
        extra_specs, extra, aliases = [pl.BlockSpec(memory_space=pl.ANY)], [prev], {7: 0}
    return pl.pallas_call(
        kernel,
        grid=(2, batch, nc),
        in_specs=[pl.BlockSpec((c, w), lambda d, b, j: (rows(d, b, j), 5)),
                  pl.BlockSpec((c, w), lambda d, b, j: (rows(d, b, j), 6 + d)),
                  pl.BlockSpec((c, w), lambda d, b, j: (rows(d, b, j), 8)),
                  pl.BlockSpec((None, 8, w), lambda d, b, j: (d, 0, 0)),
                  pl.BlockSpec((None, None, HG_HEADS, HEAD_DIM, HEAD_DIM), lambda d, b, j: (b, d, 0, 0, 0)),
                  pl.BlockSpec((None,) + Lm.shape[1:], lambda d, b, j: (d, 0, 0)),
                  pl.BlockSpec((None,) + Mm.shape[1:], lambda d, b, j: (d, 0, 0, 0))] + extra_specs,
        out_specs=[pl.BlockSpec((None, c, w), lambda d, b, j: (d, rows(d, b, j), 0)),
                   pl.BlockSpec((None, None, HG_HEADS, HEAD_DIM, HEAD_DIM), lambda d, b, j: (b, d, 0, 0, 0))],
        out_shape=[jax.ShapeDtypeStruct((2, z.shape[0], w), F32),
                   jax.ShapeDtypeStruct((batch, 2, HG_HEADS, HEAD_DIM, HEAD_DIM), F32)],
        scratch_shapes=[pltpu.VMEM((HG_HEADS, HEAD_DIM, HEAD_DIM), F32)],
        input_output_aliases=aliases,
        compiler_params=_cparams("parallel", "parallel", "arbitrary"),
        name="hgrn_scan",
    )(z, z, z, lbtab, s0, Lm, Mm, *extra)


def _out_kernel(x_ref, ga_ref, of_ref, ob_ref, hg_ref, ma_ref, w_ref, on_ref, g2_ref,
                g1_ref, sh2_ref, sc2_ref, xo_ref, hhi_ref, hlo_ref, ht_ref):
    o = of_ref[...] + ob_ref[...]
    g = hg_ref[...]
    g = g * (1.0 / (1.0 + jnp.exp(-g)))
    n_g, n_h = GQA_HEADS * HEAD_DIM, HG_HEADS * HEAD_DIM
    mix = _dot(ga_ref[...], w_ref[0:n_g, :]) + _dot(ma_ref[...], w_ref[n_g + n_h:, :])
    for h in range(HG_HEADS):
        sl = slice(h * HEAD_DIM, (h + 1) * HEAD_DIM)
        oh = _rms(o[:, sl], HEAD_DIM) * on_ref[...] * g[:, sl]
        mix = mix + _dot(oh.astype(BF16), w_ref[n_g + h * HEAD_DIM:n_g + (h + 1) * HEAD_DIM, :])
    x = x_ref[...] + g1_ref[...] * mix
    xo_ref[...] = x
    h2 = _rms(x, x.shape[-1]) * g2_ref[...] * (1.0 + sc2_ref[...]) + sh2_ref[...]
    hi, lo = _split2(h2)
    hhi_ref[...] = hi
    hlo_ref[...] = lo
    ht_ref[...] = h2.T.astype(BF16)


def _mixer_out(x, ga, o2, z, ma, w, on, g2, modv, mod_row_of_tile, tm):
    T, D = x.shape
    wh = HG_HEADS * HEAD_DIM

    def mspec(k):
        return pl.BlockSpec((None, 1, D), lambda i: (mod_row_of_tile(i) + k, 0, 0))

    return pl.pallas_call(
        _out_kernel,
        grid=(T // tm,),
        in_specs=[pl.BlockSpec((tm, D), lambda i: (i, 0)),
                  pl.BlockSpec((tm, ga.shape[1]), lambda i: (i, 0)),
                  pl.BlockSpec((None, tm, wh), lambda i: (0, i, 0)),
                  pl.BlockSpec((None, tm, wh), lambda i: (1, i, 0)),
                  pl.BlockSpec((tm, wh), lambda i: (i, 9)),
                  pl.BlockSpec((tm, ma.shape[1]), lambda i: (i, 0)),
                  pl.BlockSpec(w.shape, lambda i: (0, 0)),
                  pl.BlockSpec((1, HEAD_DIM), lambda i: (0, 0)),
                  pl.BlockSpec((1, D), lambda i: (0, 0)),
                  mspec(2), mspec(3), mspec(4)],
        out_specs=[pl.BlockSpec((tm, D), lambda i: (i, 0))] * 3 + [pl.BlockSpec((D, tm), lambda i: (0, i))],
        out_shape=[jax.ShapeDtypeStruct((T, D), F32), jax.ShapeDtypeStruct((T, D), BF16),
                   jax.ShapeDtypeStruct((T, D), BF16), jax.ShapeDtypeStruct((D, T), BF16)],
        compiler_params=_cparams("parallel"),
        name="mixer_out",
    )(x, ga, o2, o2, z, ma, w, on, g2, modv, modv, modv)


def _ws_kernel(k_ref, wq_ref, hi_ref, lo_ref):
    khi, klo = _split2(k_ref[...])
    whi, wlo = _split2(wq_ref[...])
    acc = _dot_nt(khi, whi) + _dot_nt(klo, whi) + _dot_nt(khi, wlo)
    hi, lo = _split2(acc * LOG2E)
    hi_ref[...] = hi
    lo_ref[...] = lo


def _score_weights(keys, wq):
    L, D, _ = wq.shape
    nb = 2 * PEER_HEADS
    sub = keys.shape[-1]
    kk = keys.reshape(L, nb, PEER_KEYS, sub)
    out = jax.ShapeDtypeStruct((L, nb * PEER_KEYS, D), BF16)
    ospec = pl.BlockSpec((None, PEER_KEYS, D), lambda l, b: (l, (b % 2) * PEER_HEADS + b // 2, 0))
    return pl.pallas_call(
        _ws_kernel,
        grid=(L, nb),
        in_specs=[pl.BlockSpec((None, None, PEER_KEYS, sub), lambda l, b: (l, b, 0, 0)),
                  pl.BlockSpec((None, D, sub), lambda l, b: (l, 0, b))],
        out_specs=[ospec, ospec],
        out_shape=[out, out],
        compiler_params=_cparams("parallel", "parallel"),
        name="score_weights",
    )(kk, wq)


def _score_kernel(whi_ref, wlo_ref, hhi_ref, hlo_ref, o_ref):
    whi, hhi = whi_ref[...], hhi_ref[...]
    o_ref[...] = _dot_nt(whi, hhi) + _dot_nt(whi, hlo_ref[...]) + _dot_nt(wlo_ref[...], hhi)


def _scores(whi, wlo, hhi, hlo, tm):
    R, D = whi.shape
    T = hhi.shape[0]
    tr = min(512, R)
    return pl.pallas_call(
        _score_kernel,
        grid=(R // tr, T // tm),
        in_specs=[pl.BlockSpec((tr, D), lambda r, i: (r, 0)), pl.BlockSpec((tr, D), lambda r, i: (r, 0)),
                  pl.BlockSpec((tm, D), lambda r, i: (i, 0)), pl.BlockSpec((tm, D), lambda r, i: (i, 0))],
        out_specs=pl.BlockSpec((tr, tm), lambda r, i: (r, i)),
        out_shape=jax.ShapeDtypeStruct((R, T), F32),
        compiler_params=_cparams("parallel", "parallel"),
        name="scores",
    )(whi, wlo, hhi, hlo)


def _batcher_pairs(n):
    pairs = []
    p = 1
    while p < n:
        k = p
        while k >= 1:
            for j in range(k % p, n - k, 2 * k):
                for i in range(min(k, n - j - k)):
                    if (i + j) // (2 * p) == (i + j + k) // (2 * p):
                        pairs.append((i + j, i + j + k))
            k //= 2
        p *= 2
    return pairs


_SORT16 = _batcher_pairs(PEER_TOPK)


def _sort_desc(v):
    v = list(v)
    for i, j in _SORT16:
        v[i], v[j] = jnp.maximum(v[i], v[j]), jnp.minimum(v[i], v[j])
    return v


def _merge_top(a, b):
    n = len(a)
    v = [jnp.maximum(a[i], b[n - 1 - i]) for i in range(n)]
    s = n // 2
    while s >= 1:
        for i in range(n):
            if (i & s) == 0:
                v[i], v[i + s] = jnp.maximum(v[i], v[i + s]), jnp.minimum(v[i], v[i + s])
        s //= 2
    return v


def _top16_of_keys(s_ref, row0, lanes):
    k = PEER_TOPK
    groups = [_sort_desc([s_ref[pl.ds(row0 + PEER_HEADS * (k * g + i), PEER_HEADS), lanes] for i in range(k)])
              for g in range(PEER_KEYS // k)]
    while len(groups) > 1:
        groups = [_merge_top(groups[i], groups[i + 1]) for i in range(0, len(groups), 2)]
    return groups[0]


def _topk_kernel(s_ref, thr_ref, s1n_ref):
    k = PEER_TOPK
    half = PEER_HEADS * PEER_KEYS
    ninf = jnp.full((PEER_HEADS, 128), -jnp.inf, F32)

    def chunk(ch, carry):
        lanes = pl.ds(pl.multiple_of(ch * 128, 128), 128)
        a = _top16_of_keys(s_ref, 0, lanes)
        b = _top16_of_keys(s_ref, half, lanes)
        cand = [[a[i] + b[j] for j in range(k // (i + 1))] for i in range(k)]
        g0 = cand[0]
        g1 = _sort_desc(cand[1] + cand[2] + cand[5] + cand[8])
        g2 = _sort_desc(cand[3] + cand[4] + cand[6] + cand[7] + cand[9] + cand[10] + cand[11] + cand[12] + cand[13])
        g3 = cand[14] + cand[15] + [ninf] * (k - 2)
        top = _merge_top(_merge_top(g0, g1), _merge_top(g2, g3))
        zsum = jnp.zeros((PEER_HEADS, 128), F32)
        for t in top:
            zsum = zsum + jnp.exp2(t - top[0])
        c0 = top[0] + jnp.log2(zsum)
        thr_ref[:, lanes] = top[k - 1] - c0
        for key in range(PEER_KEYS):
            tile = s_ref[pl.ds(half + PEER_HEADS * key, PEER_HEADS), lanes] - c0
            s1n_ref[ch, pl.ds(key, PEER_HEADS, stride=PEER_KEYS), :] = tile
        return carry

    lax.fori_loop(0, s_ref.shape[1] // 128, chunk, 0)


def _topk(st, tm):
    R, T = st.shape
    return pl.pallas_call(
        _topk_kernel,
        grid=(T // tm,),
        in_specs=[pl.BlockSpec((R, tm), lambda i: (0, i))],
        out_specs=[pl.BlockSpec((PEER_HEADS, tm), lambda i: (0, i)),
                   pl.BlockSpec((tm // 128, R // 2, 128), lambda i: (i, 0, 0))],
        out_shape=[jax.ShapeDtypeStruct((PEER_HEADS, T), F32), jax.ShapeDtypeStruct((T // 128, R // 2, 128), F32)],
        compiler_params=_cparams("parallel"),
        name="topk",
    )(st)


def _peer_kernel(ht_ref, s0_ref, s1n_ref, thr_ref, u_ref, vt_ref, x_ref, g2_ref, o_ref,
                 pre_a, pre_b, c_a, c_b, acc_sc):
    s = pl.program_id(1)
    te, tm = c_a.shape
    nk = PEER_KEYS

    @pl.when(s == 0)
    def _():
        acc_sc[...] = jnp.zeros_like(acc_sc)
        pre_b[...] = jnp.zeros_like(pre_b)
        c_a[...] = jnp.zeros_like(c_a)

    d = ht_ref.shape[0]
    nblk = (te // nk) * (tm // 128)
    nh = 2 if tm % 256 == 0 else 1
    th = tm // nh
    ka = max(1, nblk // (2 * nh))
    mc = max(1, nblk // (4 * nh))
    kq, mq = d // ka, d // mc

    def pre_piece(p, pre_w):
        half, q = divmod(p, ka)
        part = _dot(u_ref[:, q * kq:(q + 1) * kq], ht_ref[q * kq:(q + 1) * kq, half * th:(half + 1) * th])
        if q == 0:
            pre_w[:, half * th:(half + 1) * th] = part
        else:
            pre_w[:, half * th:(half + 1) * th] += part

    def value_piece(p, c_r):
        half, q = divmod(p, mc)
        acc_sc[q * mq:(q + 1) * mq, half * th:(half + 1) * th] += _dot(
            vt_ref[q * mq:(q + 1) * mq, :], c_r[:, half * th:(half + 1) * th])

    slots = {}
    for p in range(nh * ka):
        slots.setdefault(p * nblk // (nh * ka), []).append((pre_piece, p))
    for p in range(nh * mc):
        slots.setdefault((p * nblk // (nh * mc) + 1) % nblk, []).append((value_piece, p))

    def stages(pre_w, pre_r, c_w, c_r):
        for blk in range(nblk):
            for piece, p in slots.get(blk, ()):
                piece(p, pre_w if piece is pre_piece else c_r)
            ii, ch = divmod(blk, tm // 128)
            lanes = slice(ch * 128, (ch + 1) * 128)
            gate = jnp.zeros((nk, 128), F32)
            for h in range(PEER_HEADS):
                r0 = ii * PEER_HEADS + h
                val = s1n_ref[ch, h * nk:(h + 1) * nk, :] + s0_ref[r0:r0 + 1, lanes]
                gate = gate + jnp.where(val >= thr_ref[h:h + 1, lanes], jnp.exp2(val), 0.0)
            pre = pre_r[ii * nk:(ii + 1) * nk, lanes]
            act = 0.5 * pre * (1.0 + lax.erf(pre * (2.0 ** -0.5)))
            c_w[ii * nk:(ii + 1) * nk, lanes] = (gate * act).astype(BF16)

    @pl.when(s % 2 == 0)
    def _():
        stages(pre_a, pre_b, c_b, c_a)

    @pl.when(s % 2 == 1)
    def _():
        stages(pre_b, pre_a, c_a, c_b)

    @pl.when(s == pl.num_programs(1) - 1)
    def _():
        o_ref[...] = x_ref[...] + g2_ref[...] * acc_sc[...].T


def _peer(ht, st, thr, s1n, u, vt, x, modv, mod_row_of_tile, tm):
    T, D = x.shape
    ne, _, te = vt.shape
    half = PEER_HEADS * PEER_KEYS
    tile = lambda s, lag: jnp.clip(s - lag, 0, ne - 1)
    return pl.pallas_call(
        _peer_kernel,
        grid=(T // tm, ne + 2),
        in_specs=[pl.BlockSpec((D, tm), lambda i, s: (0, i)),
                  pl.BlockSpec((te // PEER_KEYS * PEER_HEADS, tm), lambda i, s: (tile(s, 1), i)),
                  pl.BlockSpec((tm // 128, half, 128), lambda i, s: (i, 0, 0)),
                  pl.BlockSpec((PEER_HEADS, tm), lambda i, s: (0, i)),
                  pl.BlockSpec((te, D), lambda i, s: (tile(s, 0), 0)),
                  pl.BlockSpec((None, D, te), lambda i, s: (tile(s, 2), 0, 0)),
                  pl.BlockSpec((tm, D), lambda i, s: (i, 0)),
                  pl.BlockSpec((None, 1, D), lambda i, s: (mod_row_of_tile(i) + 5, 0, 0))],
        out_specs=pl.BlockSpec((tm, D), lambda i, s: (i, 0)),
        out_shape=jax.ShapeDtypeStruct((T, D), F32),
        scratch_shapes=[pltpu.VMEM((te, tm), F32), pltpu.VMEM((te, tm), F32), pltpu.VMEM((te, tm), BF16),
                        pltpu.VMEM((te, tm), BF16), pltpu.VMEM((D, tm), F32)],
        compiler_params=_cparams("parallel", "arbitrary"),
        name="peer_experts",
    )(ht, st, s1n, thr, u, vt, x, modv)


def _rope_tables(n, ident_rows):
    pos = jnp.arange(n, dtype=jnp.int32)
    row, col = pos // GRID_W, pos % GRID_W
    lane = jnp.arange(HEAD_DIM)

    def build(nfreq, width):
        inv = ROPE_THETA ** (-jnp.arange(nfreq, dtype=F32) / nfreq)
        ar = row.astype(F32)[:, None] * inv[None, :]
        ac = col.astype(F32)[:, None] * inv[None, :]
        cos = jnp.concatenate([jnp.cos(ar), jnp.cos(ar), jnp.cos(ac), jnp.cos(ac)], axis=-1)
        sin = jnp.concatenate([jnp.sin(ar), jnp.sin(ar), jnp.sin(ac), jnp.sin(ac)], axis=-1)
        pad = HEAD_DIM - width
        cos = jnp.pad(cos, ((0, 0), (0, pad)))
        sin = jnp.pad(sin, ((0, 0), (0, pad)))
        first = (lane % (2 * nfreq)) < nfreq
        tab = jnp.stack([cos, jnp.where(first, 0.0, sin), jnp.where(first, -sin, 0.0)])
        ident = jnp.stack([jnp.ones((ident_rows, HEAD_DIM), F32), jnp.zeros((ident_rows, HEAD_DIM), F32),
                           jnp.zeros((ident_rows, HEAD_DIM), F32)])
        return jnp.concatenate([ident, tab], axis=1)

    return build(HEAD_DIM // 4, HEAD_DIM), build(MLA_ROPE // 4, MLA_ROPE)


def kernel(x_prompt, x_sample, c, cache_gqa_k, cache_gqa_v, cache_mla_ckv, cache_mla_krope, state_hgrn, c_ctx,
           w_mod, b_mod, norm1_g, norm2_g, w_in, gqa_qn, gqa_kn, mla_qa_n, mla_kva_n, w_mla_qb, w_mla_kvb,
           mla_qn, mla_kn, hg_lb_logits, hg_on, w_out, peer_wq, peer_keys, peer_u, peer_v):
    B, S, D = x_prompt.shape
    DB, DS, _ = x_sample.shape
    L = w_mod.shape[0]
    P = cache_gqa_k.shape[2]
    TC, TL = B * S, DB * DS
    T = TC + TL
    tm = 512 if (TC % 512 == 0 and DS % 512 == 0) else 128
    tm_small = min(256, tm)

    ngroups = 1 + DB

    def mod_row_of(t, layer):
        def f(i):
            group = jnp.where(i < TC // t, 0, 1 + (i - TC // t) // (DS // t))
            return (layer * ngroups + group) * 6
        return f

    def table_of(t):
        def f(i):
            return jnp.where(i < TC // t, 0, 1 + (i - TC // t) % (DS // t))
        return f

    cond8 = jnp.zeros((8, D), F32).at[0].set(c_ctx).at[1:1 + DB].set(c)
    modv = _modulation(cond8, w_mod, b_mod)[:, :ngroups].reshape(L * ngroups * 6, 1, D)

    lb_cum = jnp.cumsum(jax.nn.softmax(hg_lb_logits.astype(F32), axis=0), axis=0)
    lbs = jnp.maximum(lb_cum - lb_cum[:1], 0.0)
    lbtab = jnp.stack([jnp.log(lbs), jnp.log1p(-lbs), 1.0 - lbs] + [jnp.zeros_like(lbs)] * 5, axis=2)

    sizes = np.cumsum([0, 768, 256, 256, 512, 512, 512, 512, 512, 512, 512, 64])
    part = lambda a, k: a[..., sizes[k]:sizes[k + 1]]
    order = (0, 1, 2, 10, 8, 9, 3, 4, 5, 6, 7)
    cols = []
    for k in order:
        cols.append(part(w_in, k))
        if k == 10:
            cols.append(jnp.zeros((L, D, 256 - MLA_ROPE), w_in.dtype))
    w_in_b = jnp.concatenate(cols, axis=-1).astype(BF16)
    w_in_b = jnp.swapaxes(w_in_b.reshape(L, D, -1, IN_TILE), 1, 2)

    wqb = w_mla_qb.reshape(L, MLA_RANK, MLA_HEADS, MLA_QK)
    wqb = jnp.pad(wqb, ((0, 0), (0, 0), (0, 0), (0, MLA_PAD - MLA_QK))).reshape(L, MLA_RANK, MLA_HEADS * MLA_PAD).astype(BF16)
    wkvb = w_mla_kvb.reshape(L, MLA_RANK, MLA_HEADS, 2, HEAD_DIM)
    wkvb = jnp.moveaxis(wkvb, 3, 2).reshape(L, MLA_RANK, 2 * MLA_HEADS * HEAD_DIM).astype(BF16)
    rpad = lambda g: jnp.pad(g[:, HEAD_DIM:], ((0, 0), (0, HEAD_DIM - MLA_ROPE)))[:, None, :]
    mqn_n, mqn_r = mla_qn[:, None, :HEAD_DIM], rpad(mla_qn)
    mkn_n, mkn_r = mla_kn[:, None, :HEAD_DIM], rpad(mla_kn)
    w_out_b = w_out.astype(BF16)
    u_b = peer_u.astype(BF16)
    vt_b = jnp.swapaxes(peer_v.reshape(L, -1, PEER_EXPERT_TILE, D), 2, 3).astype(BF16)
    def key_major(w):
        return jnp.swapaxes(w.reshape(L, 2, PEER_HEADS, PEER_KEYS, D), 2, 3).reshape(L, 2 * PEER_HEADS * PEER_KEYS, D)

    ws_hi, ws_lo = map(key_major, _score_weights(peer_keys, peer_wq))

    tabg, tabm = _rope_tables(DS, tm_small)

    cck = jnp.moveaxis(cache_mla_ckv, 1, 0).reshape(L, DB * P, MLA_RANK)
    ckr = jnp.pad(jnp.moveaxis(cache_mla_krope, 1, 0).reshape(L, DB * P, MLA_ROPE), ((0, 0), (0, 0), (0, HEAD_DIM - MLA_ROPE)))
    cmk, cmv = _cache_expand(cck, ckr, wkvb, mkn_n, mkn_r)
    cgk = jnp.moveaxis(cache_gqa_k, 1, 0).reshape(L, DB * P, -1).astype(BF16)
    cgv = jnp.transpose(cache_gqa_v.reshape(DB, L, P, -1), (1, 3, 0, 2)).reshape(L, -1, DB * P).astype(BF16)
    s0_ctx = jnp.zeros((B, 2, HG_HEADS, HEAD_DIM, HEAD_DIM), F32)

    x = jnp.concatenate([x_prompt.reshape(TC, D), x_sample.reshape(TL, D)], axis=0)
    new_k, new_v, new_ckv, new_kr, new_s = [], [], [], [], []
    tq_c = min(256, S)
    tq_l, tk_l = min(1024, DS), min(512, P, DS)
    for l in range(L):
        z = _in_proj(x, norm1_g[l][None], modv, w_in_b[l], mod_row_of(tm, l), tm)
        gq, gkn, gkr, gvt, mq, ckv, mk, mvt = _prep(
            z, tabg, tabm, table_of(tm_small), tm_small, gqa_qn[l][None], gqa_kn[l][None], mla_qa_n[l][None],
            mla_kva_n[l][None], wqb[l], wkvb[l], mqn_n[l], mqn_r[l], mkn_n[l], mkn_r[l])
        new_k.append(gkn[:TC].reshape(B, S, GQA_KV_HEADS, HEAD_DIM))
        new_v.append(z[:TC, 1024:1280].reshape(B, S, GQA_KV_HEADS, HEAD_DIM))
        new_ckv.append(ckv[:TC].reshape(B, S, MLA_RANK))
        new_kr.append(z[:TC, 1280:1280 + MLA_ROPE].reshape(B, S, MLA_ROPE))

        gqa = dict(heads=GQA_HEADS, group=GQA_HEADS // GQA_KV_HEADS, dq=HEAD_DIM, dv=HEAD_DIM, hb=GQA_HEADS // GQA_KV_HEADS)
        mla = dict(heads=MLA_HEADS, group=1, dq=MLA_PAD, dv=HEAD_DIM, hb=MLA_HEADS // 2)
        ctx_a = dict(batch=B, nq=S, nk=S, row0=0, tq=tq_c, tk=tq_c)
        lat_a = dict(batch=DB, nq=DS, nk=DS, row0=TC, tq=tq_l, tk=tk_l)
        ga = _attention(gq, gkr, gvt, **gqa, **ctx_a)
        ga = _attention(gq, gkr, gvt, **gqa, **lat_a, cache=(cgk[l], cgv[l]), prev=ga)
        ma = _attention(mq, mk, mvt, **mla, **ctx_a)
        ma = _attention(mq, mk, mvt, **mla, **lat_a, cache=(cmk[l], cmv[l]), prev=ma)

        o2, s_c = _hgrn(z, lbtab[l], s0_ctx, batch=B, n=S, row0=0)
        o2, _ = _hgrn(z, lbtab[l], state_hgrn[:, l].astype(F32), batch=DB, n=DS, row0=TC, prev=o2)
        new_s.append(s_c)

        x, hhi, hlo, ht = _mixer_out(x, ga, o2, z, ma, w_out_b[l], hg_on[l][None], norm2_g[l][None], modv,
                                 mod_row_of(tm_small, l), tm_small)
        st = _scores(ws_hi[l], ws_lo[l], hhi, hlo, tm)
        thr, s1n = _topk(st, tm)
        x = _peer(ht, st, thr, s1n, u_b[l], vt_b[l], x, modv, mod_row_of(tm, l), tm)

    return (x[:TC].reshape(B, S, D), x[TC:].reshape(DB, DS, D),
            jnp.stack(new_k, axis=1), jnp.stack(new_v, axis=1), jnp.stack(new_ckv, axis=1),
            jnp.stack(new_kr, axis=1), jnp.stack(new_s, axis=1))
```

```python
import functools

import numpy as np
import jax
import jax.numpy as jnp
from jax import lax
from jax.experimental import pallas as pl
from jax.experimental.pallas import tpu as pltpu

F32 = jnp.float32
BF16 = jnp.bfloat16

EPS = 1e-6
LOG2E = 1.4426950408889634
ROPE_THETA = 10000.0
GRID_W = 64

HEAD_DIM = 128
GQA_HEADS, GQA_KV_HEADS = 6, 2
HG_HEADS = 4
MLA_HEADS, MLA_ROPE, MLA_RANK = 6, 64, 512
MLA_QK = HEAD_DIM + MLA_ROPE
MLA_PAD = 2 * HEAD_DIM
PEER_HEADS, PEER_KEYS, PEER_TOPK = 8, 128, 16
HG_CHUNK = 128
PEER_EXPERT_TILE = 512
IN_TILE = 1024

VMEM_LIMIT = 60 * 1024 * 1024


def _cparams(*sem):
    return pltpu.CompilerParams(dimension_semantics=sem, vmem_limit_bytes=VMEM_LIMIT)


def _dot(a, b):
    return jnp.dot(a, b, preferred_element_type=F32)


def _dot_nt(a, b):
    return lax.dot_general(a, b, (((1,), (1,)), ((), ())), preferred_element_type=F32)


def _dot_tn(a, b):
    return lax.dot_general(a, b, (((0,), (0,)), ((), ())), preferred_element_type=F32)


def _split2(a):
    hi = a.astype(BF16)
    lo = (a - hi.astype(F32)).astype(BF16)
    return hi, lo


def _split3(a):
    p1 = a.astype(BF16)
    r = a - p1.astype(F32)
    p2 = r.astype(BF16)
    p3 = (r - p2.astype(F32)).astype(BF16)
    return p1, p2, p3


def _rms(x, n):
    return x * lax.rsqrt(jnp.sum(x * x, axis=-1, keepdims=True) * (1.0 / n) + EPS)


def _mod_kernel(cond_ref, w_ref, b_ref, o_ref):
    c = cond_ref[...]
    s = c * (1.0 / (1.0 + jnp.exp(-c)))
    shi, slo = _split2(s)
    whi, wlo = _split2(w_ref[...])
    o_ref[...] = _dot(shi, whi) + _dot(slo, whi) + _dot(shi, wlo) + b_ref[...]


def _modulation(cond8, w_mod, b_mod):
    L, D, N = w_mod.shape
    tn = min(512, N)
    return pl.pallas_call(
        _mod_kernel,
        grid=(L, N // tn),
        in_specs=[pl.BlockSpec((8, D), lambda l, j: (0, 0)),
                  pl.BlockSpec((None, D, tn), lambda l, j: (l, 0, j)),
                  pl.BlockSpec((None, 1, tn), lambda l, j: (l, 0, j))],
        out_specs=pl.BlockSpec((None, 8, tn), lambda l, j: (l, 0, j)),
        out_shape=jax.ShapeDtypeStruct((L, 8, N), F32),
        compiler_params=_cparams("parallel", "parallel"),
        name="modulation",
    )(cond8, w_mod, b_mod.reshape(L, 1, N))


def _in_kernel(x_ref, g_ref, sh_ref, sc_ref, w_ref, o_ref, h_sc):
    @pl.when(pl.program_id(1) == 0)
    def _():
        x = x_ref[...]
        h = _rms(x, x.shape[-1]) * g_ref[...] * (1.0 + sc_ref[...]) + sh_ref[...]
        h_sc[...] = h.astype(BF16)

    o_ref[...] = _dot(h_sc[...], w_ref[...])


def _in_proj(x, g, modv, w, mod_row_of_tile, tm):
    T, D = x.shape
    nj, _, tn = w.shape
    N = nj * tn

    def mspec(k):
        return pl.BlockSpec((None, 1, D), lambda i, j: (mod_row_of_tile(i) + k, 0, 0))

    return pl.pallas_call(
        _in_kernel,
        grid=(T // tm, nj),
        in_specs=[pl.BlockSpec((tm, D), lambda i, j: (i, 0)),
                  pl.BlockSpec((1, D), lambda i, j: (0, 0)),
                  mspec(0), mspec(1),
                  pl.BlockSpec((None, D, tn), lambda i, j: (j, 0, 0))],
        out_specs=pl.BlockSpec((tm, tn), lambda i, j: (i, j)),
        out_shape=jax.ShapeDtypeStruct((T, N), F32),
        scratch_shapes=[pltpu.VMEM((tm, D), BF16)],
        compiler_params=_cparams("parallel", "arbitrary"),
        name="in_proj",
    )(x, g, modv, modv, w)


def _rope(x, tab_ref, blk):
    return (x * tab_ref[0] + pltpu.roll(x, blk, 1) * tab_ref[1]
            + pltpu.roll(x, HEAD_DIM - blk, 1) * tab_ref[2])


def _mla_kv(ckv, krope, wkvb_ref, gkn_ref, gkr_ref, tab_ref, mk_ref, mv_ref):
    kv = _dot(ckv.astype(BF16), wkvb_ref[...])
    nv = MLA_HEADS * HEAD_DIM
    mv_ref[...] = kv[:, nv:].T.astype(BF16)
    ssr = jnp.sum(krope * krope, axis=-1, keepdims=True)
    for h in range(MLA_HEADS):
        kn = kv[:, h * HEAD_DIM:(h + 1) * HEAD_DIM]
        rs = lax.rsqrt((jnp.sum(kn * kn, axis=-1, keepdims=True) + ssr) * (1.0 / MLA_QK) + EPS)
        kr = krope * rs * gkr_ref[...]
        if tab_ref is not None:
            kr = _rope(kr, tab_ref, MLA_ROPE // 4)
        mk_ref[:, h * MLA_PAD:h * MLA_PAD + HEAD_DIM] = (kn * rs * gkn_ref[...]).astype(BF16)
        mk_ref[:, h * MLA_PAD + HEAD_DIM:(h + 1) * MLA_PAD] = kr.astype(BF16)


def _prep_kernel(gq_ref, gk_ref, gv_ref, mkr_ref, mqa_ref, mkva_ref, tg_ref, tm_ref,
                 gqn_ref, gkn_ref, qan_ref, kvan_ref, wqb_ref, wkvb_ref,
                 mqn_n_ref, mqn_r_ref, mkn_n_ref, mkn_r_ref,
                 gq_o, gkn_o, gkr_o, gvt_o, mq_o, ckv_o, mk_o, mv_o):
    gscale = HEAD_DIM ** -0.5 * LOG2E
    for h in range(GQA_HEADS):
        sl = slice(h * HEAD_DIM, (h + 1) * HEAD_DIM)
        q = _rms(gq_ref[:, sl], HEAD_DIM) * gqn_ref[...]
        gq_o[:, sl] = (_rope(q, tg_ref, HEAD_DIM // 4) * gscale).astype(BF16)
    for h in range(GQA_KV_HEADS):
        sl = slice(h * HEAD_DIM, (h + 1) * HEAD_DIM)
        k = _rms(gk_ref[:, sl], HEAD_DIM) * gkn_ref[...]
        gkn_o[:, sl] = k
        gkr_o[:, sl] = _rope(k, tg_ref, HEAD_DIM // 4).astype(BF16)
    gvt_o[...] = gv_ref[...].T.astype(BF16)

    qa = _rms(mqa_ref[...], MLA_RANK) * qan_ref[...]
    mq = _dot(qa.astype(BF16), wqb_ref[...])
    mscale = MLA_QK ** -0.5 * LOG2E
    for h in range(MLA_HEADS):
        qn = mq[:, h * MLA_PAD:h * MLA_PAD + HEAD_DIM]
        qr = mq[:, h * MLA_PAD + HEAD_DIM:(h + 1) * MLA_PAD]
        ss = jnp.sum(qn * qn, axis=-1, keepdims=True) + jnp.sum(qr * qr, axis=-1, keepdims=True)
        rs = lax.rsqrt(ss * (1.0 / MLA_QK) + EPS) * mscale
        mq_o[:, h * MLA_PAD:h * MLA_PAD + HEAD_DIM] = (qn * rs * mqn_n_ref[...]).astype(BF16)
        mq_o[:, h * MLA_PAD + HEAD_DIM:(h + 1) * MLA_PAD] = _rope(qr * rs * mqn_r_ref[...], tm_ref, MLA_ROPE // 4).astype(BF16)

    ckv = _rms(mkva_ref[...], MLA_RANK) * kvan_ref[...]
    ckv_o[...] = ckv
    _mla_kv(ckv, mkr_ref[...], wkvb_ref, mkn_n_ref, mkn_r_ref, tm_ref, mk_o, mv_o)


def _prep(z, tabg, tabm, table_of_tile, tm, gqn, gkn, qan, kvan, wqb, wkvb, mqn_n, mqn_r, mkn_n, mkn_r):
    T = z.shape[0]
    row = lambda w: pl.BlockSpec((1, w), lambda i: (0, 0))
    full = lambda a: pl.BlockSpec(a.shape, lambda i: (0, 0))
    tab = pl.BlockSpec((3, tm, HEAD_DIM), lambda i: (0, table_of_tile(i), 0))
    zspec = lambda w, blk: pl.BlockSpec((tm, w), lambda i: (i, blk))
    widths = (GQA_HEADS * HEAD_DIM, GQA_KV_HEADS * HEAD_DIM, GQA_KV_HEADS * HEAD_DIM, GQA_KV_HEADS * HEAD_DIM,
              MLA_HEADS * MLA_PAD, MLA_RANK, MLA_HEADS * MLA_PAD, MLA_HEADS * HEAD_DIM)
    dtypes = (BF16, F32, BF16, BF16, BF16, F32, BF16, BF16)
    transposed = (False, False, False, True, False, False, False, True)
    ospec = lambda w, t: pl.BlockSpec((w, tm), lambda i: (0, i)) if t else pl.BlockSpec((tm, w), lambda i: (i, 0))
    return pl.pallas_call(
        _prep_kernel,
        grid=(T // tm,),
        in_specs=[zspec(768, 0), zspec(256, 3), zspec(256, 4), zspec(128, 10), zspec(512, 3), zspec(512, 4),
                  tab, tab, row(128), row(128), row(512), row(512), full(wqb), full(wkvb),
                  row(128), row(128), row(128), row(128)],
        out_specs=[ospec(w, t) for w, t in zip(widths, transposed)],
        out_shape=[jax.ShapeDtypeStruct((w, T) if t else (T, w), d) for w, d, t in zip(widths, dtypes, transposed)],
        compiler_params=_cparams("parallel"),
        name="prep",
    )(z, z, z, z, z, z, tabg, tabm, gqn, gkn, qan, kvan, wqb, wkvb, mqn_n, mqn_r, mkn_n, mkn_r)


def _cache_kernel(ckv_ref, kr_ref, wkvb_ref, gkn_ref, gkr_ref, mk_o, mv_o):
    _mla_kv(ckv_ref[...], kr_ref[...], wkvb_ref, gkn_ref, gkr_ref, None, mk_o, mv_o)


def _cache_expand(ckv, kr, wkvb, mkn_n, mkn_r):
    L, R, _ = ckv.shape
    tm = min(512, R)
    return pl.pallas_call(
        _cache_kernel,
        grid=(L, R // tm),
        in_specs=[pl.BlockSpec((None, tm, MLA_RANK), lambda l, i: (l, i, 0)),
                  pl.BlockSpec((None, tm, HEAD_DIM), lambda l, i: (l, i, 0)),
                  pl.BlockSpec((None,) + wkvb.shape[1:], lambda l, i: (l, 0, 0)),
                  pl.BlockSpec((None, 1, HEAD_DIM), lambda l, i: (l, 0, 0)),
                  pl.BlockSpec((None, 1, HEAD_DIM), lambda l, i: (l, 0, 0))],
        out_specs=[pl.BlockSpec((None, tm, MLA_HEADS * MLA_PAD), lambda l, i: (l, i, 0)),
                   pl.BlockSpec((None, MLA_HEADS * HEAD_DIM, tm), lambda l, i: (l, 0, i))],
        out_shape=[jax.ShapeDtypeStruct((L, R, MLA_HEADS * MLA_PAD), BF16),
                   jax.ShapeDtypeStruct((L, MLA_HEADS * HEAD_DIM, R), BF16)],
        compiler_params=_cparams("parallel", "parallel"),
        name="cache_expand",
    )(ckv, kr, wkvb, mkn_n, mkn_r)


def _attn_kernel(*refs, heads, group, dq, dv, ncb, has_prev):
    n_in = (5 if ncb else 3) + (1 if has_prev else 0)
    q_ref = refs[0]
    o_ref, m_sc, l_sc, acc_sc = refs[n_in:]
    ki = pl.program_id(3)

    @pl.when(ki == 0)
    def _():
        m_sc[...] = jnp.full_like(m_sc, -jnp.inf)
        l_sc[...] = jnp.zeros_like(l_sc)
        acc_sc[...] = jnp.zeros_like(acc_sc)

    def step(k_ref, vt_ref):
        def scores(h):
            kv = h // group
            return _dot_nt(k_ref[:, kv * dq:(kv + 1) * dq], q_ref[:, h * dq:(h + 1) * dq])

        s_next = scores(0)
        for h in range(heads):
            kv = h // group
            s = s_next
            if h + 1 < heads:
                s_next = scores(h + 1)
            m_prev = m_sc[h]
            m_new = jnp.maximum(m_prev, jnp.max(s, axis=0, keepdims=True))
            p = jnp.exp2(s - m_new)
            alpha = jnp.exp2(m_prev - m_new)
            l_sc[h] = alpha * l_sc[h] + jnp.sum(p, axis=0, keepdims=True)
            acc_sc[h] = alpha * acc_sc[h] + _dot(vt_ref[kv * dv:(kv + 1) * dv, :], p.astype(BF16))
            m_sc[h] = m_new

    if ncb:
        pl.when(ki < ncb)(lambda: step(refs[1], refs[2]))
        pl.when(ki >= ncb)(lambda: step(refs[3], refs[4]))
    else:
        step(refs[1], refs[2])

    @pl.when(ki == pl.num_programs(3) - 1)
    def _():
        for h in range(heads):
            o_ref[:, h * dv:(h + 1) * dv] = (acc_sc[h] / l_sc[h]).T.astype(o_ref.dtype)


def _attention(q, k, vt, *, batch, nq, nk, row0, heads, group, dq, dv, tq, tk, hb, cache=None, prev=None):
    T = q.shape[0]
    nqb, nnb = nq // tq, nk // tk
    ncb = 0 if cache is None else cache[0].shape[0] // batch // tk
    qb0, kb0 = row0 // tq, row0 // tk
    kvb = max(1, hb // group)
    new_blk = lambda b, j: kb0 + b * nnb + jnp.maximum(j - ncb, 0)
    in_specs = [pl.BlockSpec((tq, hb * dq), lambda b, h, i, j: (qb0 + b * nqb + i, h))]
    operands = [q]
    if ncb:
        cache_blk = lambda b, j: b * ncb + jnp.minimum(j, ncb - 1)
        in_specs += [pl.BlockSpec((tk, kvb * dq), lambda b, h, i, j: (cache_blk(b, j), h)),
                     pl.BlockSpec((kvb * dv, tk), lambda b, h, i, j: (h, cache_blk(b, j)))]
        operands += list(cache)
    in_specs += [pl.BlockSpec((tk, kvb * dq), lambda b, h, i, j: (new_blk(b, j), h)),
                 pl.BlockSpec((kvb * dv, tk), lambda b, h, i, j: (h, new_blk(b, j)))]
    operands += [k, vt]
    aliases = {}
    if prev is not None:
        in_specs.append(pl.BlockSpec(memory_space=pl.ANY))
        aliases = {len(operands): 0}
        operands.append(prev)
    return pl.pallas_call(
        functools.partial(_attn_kernel, heads=hb, group=group, dq=dq, dv=dv, ncb=ncb, has_prev=prev is not None),
        grid=(batch, heads // hb, nqb, ncb + nnb),
        in_specs=in_specs,
        out_specs=pl.BlockSpec((tq, hb * dv), lambda b, h, i, j: (qb0 + b * nqb + i, h)),
        out_shape=jax.ShapeDtypeStruct((T, heads * dv), BF16),
        scratch_shapes=[pltpu.VMEM((hb, 1, tq), F32), pltpu.VMEM((hb, 1, tq), F32), pltpu.VMEM((hb, dv, tq), F32)],
        input_output_aliases=aliases,
        compiler_params=_cparams("parallel", "parallel", "parallel", "arbitrary"),
        name="attention",
    )(*operands)


def _hgrn_constants(c):
    levels = int(np.log2(c))
    tri = np.zeros((2, c, c), np.float32)
    M = np.zeros((2, levels, c, c), np.float32)
    tri[0] = np.tril(np.ones((c, c), np.float32))
    for lv in range(levels):
        m = c >> (lv + 1)
        for a in range(0, c, 2 * m):
            M[0, lv, a + m:a + 2 * m, a:a + m] = 1.0
    tri[1] = tri[0][::-1, ::-1]
    M[1] = M[0][:, ::-1, ::-1]
    return tri, M


def _level_exponents(bq, bwd, c):
    rowid = lax.broadcasted_iota(jnp.int32, bq.shape, 0)
    out = []
    m = c // 2
    while m >= 1:
        if 2 * m >= 8:
            blocks = [jnp.broadcast_to(jnp.where(bwd, bq[a + m:a + m + 1, :], bq[a + m - 1:a + m, :]), (2 * m, bq.shape[1]))
                      for a in range(0, c, 2 * m)]
            ref = blocks[0] if len(blocks) == 1 else jnp.concatenate(blocks, axis=0)
        else:
            target = m - 1 + bwd.astype(jnp.int32)
            rowmod = rowid % (2 * m)
            ref = bq
            for delta in range(-m, m + 1):
                if delta != 0:
                    ref = jnp.where(rowmod + delta == target, pltpu.roll(bq, (-delta) % c, 0), ref)
        out.append(-jnp.abs(bq - ref))
        m //= 2
    return out


def _hgrn_kernel(q_ref, z_ref, v_ref, lb_ref, s0_ref, L_ref, M_ref, o_ref, sf_ref, s_sc):
    c = q_ref.shape[0]
    levels = M_ref.shape[0]

    @pl.when(pl.program_id(2) == 0)
    def _():
        s_sc[...] = s0_ref[...]

    z = z_ref[...]
    e = jnp.exp(-jnp.abs(z))
    r = 1.0 / (1.0 + e)
    logsig = jnp.minimum(z, 0.0) + jnp.log(r)
    a = lb_ref[0:1, :]
    cc = lb_ref[1:2, :] + logsig
    logf = jnp.maximum(a, cc) + jnp.log(1.0 + jnp.exp(-jnp.abs(a - cc)))
    kk = lb_ref[2:3, :] * jnp.where(z >= 0.0, e * r, r)

    tri = L_ref[...]
    p1, p2, p3 = _split3(logf)
    bq = _dot(tri, p1) + _dot(tri, p2) + _dot(tri, p3)
    tot = jnp.minimum(bq[0:1, :], bq[c - 1:c, :])
    ex_q = jnp.exp(bq)
    ex_k = jnp.exp(tot - bq)
    ex_lv = [jnp.exp(t) for t in _level_exponents(bq, pl.program_id(0) == 1, c)]
    ones = jnp.ones((c, HEAD_DIM), BF16)
    for h in range(HG_HEADS):
        sl = slice(h * HEAD_DIM, (h + 1) * HEAD_DIM)
        q, k, v = q_ref[:, sl], kk[:, sl], v_ref[:, sl]
        vb = v.astype(BF16)
        amat = jnp.zeros((c, c), F32)
        for lv in range(levels):
            el = ex_lv[lv][:, sl]
            amat = amat + M_ref[lv] * _dot_nt((q * el).astype(BF16), (k * el).astype(BF16))
        s_prev = s_sc[h]
        o = (_dot(amat.astype(BF16), vb) + jnp.sum(q * k, axis=-1, keepdims=True) * v
             + _dot((q * ex_q[:, sl]).astype(BF16), s_prev.astype(BF16)))
        o_ref[:, sl] = o
        dcol = jnp.exp(_dot_tn(p1[:, sl], ones) + _dot_tn(p2[:, sl], ones) + _dot_tn(p3[:, sl], ones))
        s_new = dcol * s_prev + _dot_tn((k * ex_k[:, sl]).astype(BF16), vb)
        s_sc[h] = s_new
        sf_ref[h] = s_new


def _hgrn(z, lbtab, s0, *, batch, n, row0, prev=None):
    c = min(HG_CHUNK, n)
    nc = n // c
    cb0 = row0 // c
    Lm, Mm = _hgrn_constants(c)
    Lm, Mm = jnp.asarray(Lm, BF16), jnp.asarray(Mm, F32)
    w = HG_HEADS * HEAD_DIM

    def rows(d, b, j):
        return cb0 + b * nc + j + d * (nc - 1 - 2 * j)

    kernel, extra_specs, extra, aliases = _hgrn_kernel, [], [], {}
    if prev is not None:
        kernel = lambda *refs: _hgrn_kernel(*refs[:7], *refs[8:])
        extra_specs, extra, aliases = [pl.BlockSpec(memory_space=pl.ANY)], [prev], {7: 0}
    return pl.pallas_call(
        kernel,
        grid=(2, batch, nc),
        in_specs=[pl.BlockSpec((c, w), lambda d, b, j: (rows(d, b, j), 5)),
                  pl.BlockSpec((c, w), lambda d, b, j: (rows(d, b, j), 6 + d)),
                  pl.BlockSpec((c, w), lambda d, b, j: (rows(d, b, j), 8)),
                  pl.BlockSpec((None, 8, w), lambda d, b, j: (d, 0, 0)),
                  pl.BlockSpec((None, None, HG_HEADS, HEAD_DIM, HEAD_DIM), lambda d, b, j: (b, d, 0, 0, 0)),
                  pl.BlockSpec((None,) + Lm.shape[1:], lambda d, b, j: (d, 0, 0)),
                  pl.BlockSpec((None,) + Mm.shape[1:], lambda d, b, j: (d, 0, 0, 0))] + extra_specs,
        out_specs=[pl.BlockSpec((None, c, w), lambda d, b, j: (d, rows(d, b, j), 0)),
                   pl.BlockSpec((None, None, HG_HEADS, HEAD_DIM, HEAD_DIM), lambda d, b, j: (b, d, 0, 0, 0))],
        out_shape=[jax.ShapeDtypeStruct((2, z.shape[0], w), F32),
                   jax.ShapeDtypeStruct((batch, 2, HG_HEADS, HEAD_DIM, HEAD_DIM), F32)],
        scratch_shapes=[pltpu.VMEM((HG_HEADS, HEAD_DIM, HEAD_DIM), F32)],
        input_output_aliases=aliases,
        compiler_params=_cparams("parallel", "parallel", "arbitrary"),
        name="hgrn_scan",
    )(z, z, z, lbtab, s0, Lm, Mm, *extra)


def _out_kernel(x_ref, ga_ref, of_ref, ob_ref, hg_ref, ma_ref, w_ref, on_ref, g2_ref,
                g1_ref, sh2_ref, sc2_ref, xo_ref, hhi_ref, hlo_ref, ht_ref):
    o = of_ref[...] + ob_ref[...]
    g = hg_ref[...]
    g = g * (1.0 / (1.0 + jnp.exp(-g)))
    n_g, n_h = GQA_HEADS * HEAD_DIM, HG_HEADS * HEAD_DIM
    mix = _dot(ga_ref[...], w_ref[0:n_g, :]) + _dot(ma_ref[...], w_ref[n_g + n_h:, :])
    for h in range(HG_HEADS):
        sl = slice(h * HEAD_DIM, (h + 1) * HEAD_DIM)
        oh = _rms(o[:, sl], HEAD_DIM) * on_ref[...] * g[:, sl]
        mix = mix + _dot(oh.astype(BF16), w_ref[n_g + h * HEAD_DIM:n_g + (h + 1) * HEAD_DIM, :])
    x = x_ref[...] + g1_ref[...] * mix
    xo_ref[...] = x
    h2 = _rms(x, x.shape[-1]) * g2_ref[...] * (1.0 + sc2_ref[...]) + sh2_ref[...]
    hi, lo = _split2(h2)
    hhi_ref[...] = hi
    hlo_ref[...] = lo
    ht_ref[...] = h2.T.astype(BF16)


def _mixer_out(x, ga, o2, z, ma, w, on, g2, modv, mod_row_of_tile, tm):
    T, D = x.shape
    wh = HG_HEADS * HEAD_DIM

    def mspec(k):
        return pl.BlockSpec((None, 1, D), lambda i: (mod_row_of_tile(i) + k, 0, 0))

    return pl.pallas_call(
        _out_kernel,
        grid=(T // tm,),
        in_specs=[pl.BlockSpec((tm, D), lambda i: (i, 0)),
                  pl.BlockSpec((tm, ga.shape[1]), lambda i: (i, 0)),
                  pl.BlockSpec((None, tm, wh), lambda i: (0, i, 0)),
                  pl.BlockSpec((None, tm, wh), lambda i: (1, i, 0)),
                  pl.BlockSpec((tm, wh), lambda i: (i, 9)),
                  pl.BlockSpec((tm, ma.shape[1]), lambda i: (i, 0)),
                  pl.BlockSpec(w.shape, lambda i: (0, 0)),
                  pl.BlockSpec((1, HEAD_DIM), lambda i: (0, 0)),
                  pl.BlockSpec((1, D), lambda i: (0, 0)),
                  mspec(2), mspec(3), mspec(4)],
        out_specs=[pl.BlockSpec((tm, D), lambda i: (i, 0))] * 3 + [pl.BlockSpec((D, tm), lambda i: (0, i))],
        out_shape=[jax.ShapeDtypeStruct((T, D), F32), jax.ShapeDtypeStruct((T, D), BF16),
                   jax.ShapeDtypeStruct((T, D), BF16), jax.ShapeDtypeStruct((D, T), BF16)],
        compiler_params=_cparams("parallel"),
        name="mixer_out",
    )(x, ga, o2, o2, z, ma, w, on, g2, modv, modv, modv)


def _ws_kernel(k_ref, wq_ref, hi_ref, lo_ref):
    khi, klo = _split2(k_ref[...])
    whi, wlo = _split2(wq_ref[...])
    acc = _dot_nt(khi, whi) + _dot_nt(klo, whi) + _dot_nt(khi, wlo)
    hi, lo = _split2(acc * LOG2E)
    hi_ref[...] = hi
    lo_ref[...] = lo


def _score_weights(keys, wq):
    L, D, _ = wq.shape
    nb = 2 * PEER_HEADS
    sub = keys.shape[-1]
    kk = keys.reshape(L, nb, PEER_KEYS, sub)
    out = jax.ShapeDtypeStruct((L, nb * PEER_KEYS, D), BF16)
    ospec = pl.BlockSpec((None, PEER_KEYS, D), lambda l, b: (l, (b % 2) * PEER_HEADS + b // 2, 0))
    return pl.pallas_call(
        _ws_kernel,
        grid=(L, nb),
        in_specs=[pl.BlockSpec((None, None, PEER_KEYS, sub), lambda l, b: (l, b, 0, 0)),
                  pl.BlockSpec((None, D, sub), lambda l, b: (l, 0, b))],
        out_specs=[ospec, ospec],
        out_shape=[out, out],
        compiler_params=_cparams("parallel", "parallel"),
        name="score_weights",
    )(kk, wq)


def _score_kernel(whi_ref, wlo_ref, hhi_ref, hlo_ref, o_ref):
    whi, hhi = whi_ref[...], hhi_ref[...]
    o_ref[...] = _dot_nt(whi, hhi) + _dot_nt(whi, hlo_ref[...]) + _dot_nt(wlo_ref[...], hhi)


def _scores(whi, wlo, hhi, hlo, tm):
    R, D = whi.shape
    T = hhi.shape[0]
    tr = min(512, R)
    return pl.pallas_call(
        _score_kernel,
        grid=(R // tr, T // tm),
        in_specs=[pl.BlockSpec((tr, D), lambda r, i: (r, 0)), pl.BlockSpec((tr, D), lambda r, i: (r, 0)),
                  pl.BlockSpec((tm, D), lambda r, i: (i, 0)), pl.BlockSpec((tm, D), lambda r, i: (i, 0))],
        out_specs=pl.BlockSpec((tr, tm), lambda r, i: (r, i)),
        out_shape=jax.ShapeDtypeStruct((R, T), F32),
        compiler_params=_cparams("parallel", "parallel"),
        name="scores",
    )(whi, wlo, hhi, hlo)


def _batcher_pairs(n):
    pairs = []
    p = 1
    while p < n:
        k = p
        while k >= 1:
            for j in range(k % p, n - k, 2 * k):
                for i in range(min(k, n - j - k)):
                    if (i + j) // (2 * p) == (i + j + k) // (2 * p):
                        pairs.append((i + j, i + j + k))
            k //= 2
        p *= 2
    return pairs


_SORT16 = _batcher_pairs(PEER_TOPK)


def _sort_desc(v):
    v = list(v)
    for i, j in _SORT16:
        v[i], v[j] = jnp.maximum(v[i], v[j]), jnp.minimum(v[i], v[j])
    return v


def _merge_top(a, b):
    n = len(a)
    v = [jnp.maximum(a[i], b[n - 1 - i]) for i in range(n)]
    s = n // 2
    while s >= 1:
        for i in range(n):
            if (i & s) == 0:
                v[i], v[i + s] = jnp.maximum(v[i], v[i + s]), jnp.minimum(v[i], v[i + s])
        s //= 2
    return v


def _top16_of_keys(s_ref, row0, lanes):
    k = PEER_TOPK
    groups = [_sort_desc([s_ref[pl.ds(row0 + PEER_HEADS * (k * g + i), PEER_HEADS), lanes] for i in range(k)])
              for g in range(PEER_KEYS // k)]
    while len(groups) > 1:
        groups = [_merge_top(groups[i], groups[i + 1]) for i in range(0, len(groups), 2)]
    return groups[0]


def _topk_kernel(s_ref, thr_ref, s1n_ref):
    k = PEER_TOPK
    half = PEER_HEADS * PEER_KEYS
    ninf = jnp.full((PEER_HEADS, 128), -jnp.inf, F32)

    def chunk(ch, carry):
        lanes = pl.ds(pl.multiple_of(ch * 128, 128), 128)
        a = _top16_of_keys(s_ref, 0, lanes)
        b = _top16_of_keys(s_ref, half, lanes)
        cand = [[a[i] + b[j] for j in range(k // (i + 1))] for i in range(k)]
        g0 = cand[0]
        g1 = _sort_desc(cand[1] + cand[2] + cand[5] + cand[8])
        g2 = _sort_desc(cand[3] + cand[4] + cand[6] + cand[7] + cand[9] + cand[10] + cand[11] + cand[12] + cand[13])
        g3 = cand[14] + cand[15] + [ninf] * (k - 2)
        top = _merge_top(_merge_top(g0, g1), _merge_top(g2, g3))
        zsum = jnp.zeros((PEER_HEADS, 128), F32)
        for t in top:
            zsum = zsum + jnp.exp2(t - top[0])
        c0 = top[0] + jnp.log2(zsum)
        thr_ref[:, lanes] = top[k - 1] - c0
        for key in range(PEER_KEYS):
            tile = s_ref[pl.ds(half + PEER_HEADS * key, PEER_HEADS), lanes] - c0
            s1n_ref[ch, pl.ds(key, PEER_HEADS, stride=PEER_KEYS), :] = tile
        return carry

    lax.fori_loop(0, s_ref.shape[1] // 128, chunk, 0)


def _topk(st, tm):
    R, T = st.shape
    return pl.pallas_call(
        _topk_kernel,
        grid=(T // tm,),
        in_specs=[pl.BlockSpec((R, tm), lambda i: (0, i))],
        out_specs=[pl.BlockSpec((PEER_HEADS, tm), lambda i: (0, i)),
                   pl.BlockSpec((tm // 128, R // 2, 128), lambda i: (i, 0, 0))],
        out_shape=[jax.ShapeDtypeStruct((PEER_HEADS, T), F32), jax.ShapeDtypeStruct((T // 128, R // 2, 128), F32)],
        compiler_params=_cparams("parallel"),
        name="topk",
    )(st)


def _peer_kernel(ht_ref, s0_ref, s1n_ref, thr_ref, u_ref, vt_ref, x_ref, g2_ref, o_ref,
                 pre_a, pre_b, c_a, c_b, acc_sc):
    s = pl.program_id(1)
    te, tm = c_a.shape
    nk = PEER_KEYS

    @pl.when(s == 0)
    def _():
        acc_sc[...] = jnp.zeros_like(acc_sc)
        pre_b[...] = jnp.zeros_like(pre_b)
        c_a[...] = jnp.zeros_like(c_a)

    d = ht_ref.shape[0]
    ni = te // nk
    jr = 16
    nblk = (tm // 128) * (nk // jr)
    nh = 2 if tm % 256 == 0 else 1
    th = tm // nh
    ka = mc = max(1, min(nblk // nh, d // 256))
    kq, mq = d // ka, d // mc

    def pre_piece(p, pre_w):
        half, q = divmod(p, ka)
        part = _dot(u_ref[:, q * kq:(q + 1) * kq], ht_ref[q * kq:(q + 1) * kq, half * th:(half + 1) * th])
        if q == 0:
            pre_w[:, half * th:(half + 1) * th] = part
        else:
            pre_w[:, half * th:(half + 1) * th] += part

    def value_piece(p, c_r):
        half, q = divmod(p, mc)
        acc_sc[q * mq:(q + 1) * mq, half * th:(half + 1) * th] += _dot(
            vt_ref[q * mq:(q + 1) * mq, :], c_r[:, half * th:(half + 1) * th])

    slots = {}
    for p in range(nh * ka):
        slots.setdefault(p * nblk // (nh * ka), []).append((pre_piece, p))
    for p in range(nh * mc):
        slots.setdefault((p * nblk // (nh * mc) + 1) % nblk, []).append((value_piece, p))

    def stages(pre_w, pre_r, c_w, c_r):
        for blk in range(nblk):
            for piece, p in slots.get(blk, ()):
                piece(p, pre_w if piece is pre_piece else c_r)
            ch, jq = divmod(blk, nk // jr)
            lanes = slice(ch * 128, (ch + 1) * 128)
            gates = [jnp.zeros((jr, 128), F32) for _ in range(ni)]
            for h in range(PEER_HEADS):
                s1 = s1n_ref[ch, h * nk + jq * jr:h * nk + (jq + 1) * jr, :]
                thr = thr_ref[h:h + 1, lanes]
                for ii in range(ni):
                    r0 = ii * PEER_HEADS + h
                    val = s1 + s0_ref[r0:r0 + 1, lanes]
                    gates[ii] = gates[ii] + jnp.where(val >= thr, jnp.exp2(val), 0.0)
            for ii in range(ni):
                rows = slice(ii * nk + jq * jr, ii * nk + (jq + 1) * jr)
                pre = pre_r[rows, lanes]
                act = 0.5 * pre * (1.0 + lax.erf(pre * (2.0 ** -0.5)))
                c_w[rows, lanes] = (gates[ii] * act).astype(BF16)

    @pl.when(s % 2 == 0)
    def _():
        stages(pre_a, pre_b, c_b, c_a)

    @pl.when(s % 2 == 1)
    def _():
        stages(pre_b, pre_a, c_a, c_b)

    @pl.when(s == pl.num_programs(1) - 1)
    def _():
        o_ref[...] = x_ref[...] + g2_ref[...] * acc_sc[...].T


def _peer(ht, st, thr, s1n, u, vt, x, modv, mod_row_of_tile, tm):
    T, D = x.shape
    ne, _, te = vt.shape
    half = PEER_HEADS * PEER_KEYS
    tile = lambda s, lag: jnp.clip(s - lag, 0, ne - 1)
    return pl.pallas_call(
        _peer_kernel,
        grid=(T // tm, ne + 2),
        in_specs=[pl.BlockSpec((D, tm), lambda i, s: (0, i)),
                  pl.BlockSpec((te // PEER_KEYS * PEER_HEADS, tm), lambda i, s: (tile(s, 1), i)),
                  pl.BlockSpec((tm // 128, half, 128), lambda i, s: (i, 0, 0)),
                  pl.BlockSpec((PEER_HEADS, tm), lambda i, s: (0, i)),
                  pl.BlockSpec((te, D), lambda i, s: (tile(s, 0), 0)),
                  pl.BlockSpec((None, D, te), lambda i, s: (tile(s, 2), 0, 0)),
                  pl.BlockSpec((tm, D), lambda i, s: (i, 0)),
                  pl.BlockSpec((None, 1, D), lambda i, s: (mod_row_of_tile(i) + 5, 0, 0))],
        out_specs=pl.BlockSpec((tm, D), lambda i, s: (i, 0)),
        out_shape=jax.ShapeDtypeStruct((T, D), F32),
        scratch_shapes=[pltpu.VMEM((te, tm), F32), pltpu.VMEM((te, tm), F32), pltpu.VMEM((te, tm), BF16),
                        pltpu.VMEM((te, tm), BF16), pltpu.VMEM((D, tm), F32)],
        compiler_params=_cparams("parallel", "arbitrary"),
        name="peer_experts",
    )(ht, st, s1n, thr, u, vt, x, modv)


def _rope_tables(n, ident_rows):
    pos = jnp.arange(n, dtype=jnp.int32)
    row, col = pos // GRID_W, pos % GRID_W
    lane = jnp.arange(HEAD_DIM)

    def build(nfreq, width):
        inv = ROPE_THETA ** (-jnp.arange(nfreq, dtype=F32) / nfreq)
        ar = row.astype(F32)[:, None] * inv[None, :]
        ac = col.astype(F32)[:, None] * inv[None, :]
        cos = jnp.concatenate([jnp.cos(ar), jnp.cos(ar), jnp.cos(ac), jnp.cos(ac)], axis=-1)
        sin = jnp.concatenate([jnp.sin(ar), jnp.sin(ar), jnp.sin(ac), jnp.sin(ac)], axis=-1)
        pad = HEAD_DIM - width
        cos = jnp.pad(cos, ((0, 0), (0, pad)))
        sin = jnp.pad(sin, ((0, 0), (0, pad)))
        first = (lane % (2 * nfreq)) < nfreq
        tab = jnp.stack([cos, jnp.where(first, 0.0, sin), jnp.where(first, -sin, 0.0)])
        ident = jnp.stack([jnp.ones((ident_rows, HEAD_DIM), F32), jnp.zeros((ident_rows, HEAD_DIM), F32),
                           jnp.zeros((ident_rows, HEAD_DIM), F32)])
        return jnp.concatenate([ident, tab], axis=1)

    return build(HEAD_DIM // 4, HEAD_DIM), build(MLA_ROPE // 4, MLA_ROPE)


def kernel(x_prompt, x_sample, c, cache_gqa_k, cache_gqa_v, cache_mla_ckv, cache_mla_krope, state_hgrn, c_ctx,
           w_mod, b_mod, norm1_g, norm2_g, w_in, gqa_qn, gqa_kn, mla_qa_n, mla_kva_n, w_mla_qb, w_mla_kvb,
           mla_qn, mla_kn, hg_lb_logits, hg_on, w_out, peer_wq, peer_keys, peer_u, peer_v):
    B, S, D = x_prompt.shape
    DB, DS, _ = x_sample.shape
    L = w_mod.shape[0]
    P = cache_gqa_k.shape[2]
    TC, TL = B * S, DB * DS
    T = TC + TL
    tm = 512 if (TC % 512 == 0 and DS % 512 == 0) else 128
    tm_small = min(256, tm)

    ngroups = 1 + DB

    def mod_row_of(t, layer):
        def f(i):
            group = jnp.where(i < TC // t, 0, 1 + (i - TC // t) // (DS // t))
            return (layer * ngroups + group) * 6
        return f

    def table_of(t):
        def f(i):
            return jnp.where(i < TC // t, 0, 1 + (i - TC // t) % (DS // t))
        return f

    cond8 = jnp.zeros((8, D), F32).at[0].set(c_ctx).at[1:1 + DB].set(c)
    modv = _modulation(cond8, w_mod, b_mod)[:, :ngroups].reshape(L * ngroups * 6, 1, D)

    lb_cum = jnp.cumsum(jax.nn.softmax(hg_lb_logits.astype(F32), axis=0), axis=0)
    lbs = jnp.maximum(lb_cum - lb_cum[:1], 0.0)
    lbtab = jnp.stack([jnp.log(lbs), jnp.log1p(-lbs), 1.0 - lbs] + [jnp.zeros_like(lbs)] * 5, axis=2)

    sizes = np.cumsum([0, 768, 256, 256, 512, 512, 512, 512, 512, 512, 512, 64])
    part = lambda a, k: a[..., sizes[k]:sizes[k + 1]]
    order = (0, 1, 2, 10, 8, 9, 3, 4, 5, 6, 7)
    cols = []
    for k in order:
        cols.append(part(w_in, k))
        if k == 10:
            cols.append(jnp.zeros((L, D, 256 - MLA_ROPE), w_in.dtype))
    w_in_b = jnp.concatenate(cols, axis=-1).astype(BF16)
    w_in_b = jnp.swapaxes(w_in_b.reshape(L, D, -1, IN_TILE), 1, 2)

    wqb = w_mla_qb.reshape(L, MLA_RANK, MLA_HEADS, MLA_QK)
    wqb = jnp.pad(wqb, ((0, 0), (0, 0), (0, 0), (0, MLA_PAD - MLA_QK))).reshape(L, MLA_RANK, MLA_HEADS * MLA_PAD).astype(BF16)
    wkvb = w_mla_kvb.reshape(L, MLA_RANK, MLA_HEADS, 2, HEAD_DIM)
    wkvb = jnp.moveaxis(wkvb, 3, 2).reshape(L, MLA_RANK, 2 * MLA_HEADS * HEAD_DIM).astype(BF16)
    rpad = lambda g: jnp.pad(g[:, HEAD_DIM:], ((0, 0), (0, HEAD_DIM - MLA_ROPE)))[:, None, :]
    mqn_n, mqn_r = mla_qn[:, None, :HEAD_DIM], rpad(mla_qn)
    mkn_n, mkn_r = mla_kn[:, None, :HEAD_DIM], rpad(mla_kn)
    w_out_b = w_out.astype(BF16)
    u_b = peer_u.astype(BF16)
    vt_b = jnp.swapaxes(peer_v.reshape(L, -1, PEER_EXPERT_TILE, D), 2, 3).astype(BF16)
    def key_major(w):
        return jnp.swapaxes(w.reshape(L, 2, PEER_HEADS, PEER_KEYS, D), 2, 3).reshape(L, 2 * PEER_HEADS * PEER_KEYS, D)

    ws_hi, ws_lo = map(key_major, _score_weights(peer_keys, peer_wq))

    tabg, tabm = _rope_tables(DS, tm_small)

    cck = jnp.moveaxis(cache_mla_ckv, 1, 0).reshape(L, DB * P, MLA_RANK)
    ckr = jnp.pad(jnp.moveaxis(cache_mla_krope, 1, 0).reshape(L, DB * P, MLA_ROPE), ((0, 0), (0, 0), (0, HEAD_DIM - MLA_ROPE)))
    cmk, cmv = _cache_expand(cck, ckr, wkvb, mkn_n, mkn_r)
    cgk = jnp.moveaxis(cache_gqa_k, 1, 0).reshape(L, DB * P, -1).astype(BF16)
    cgv = jnp.transpose(cache_gqa_v.reshape(DB, L, P, -1), (1, 3, 0, 2)).reshape(L, -1, DB * P).astype(BF16)
    s0_ctx = jnp.zeros((B, 2, HG_HEADS, HEAD_DIM, HEAD_DIM), F32)

    x = jnp.concatenate([x_prompt.reshape(TC, D), x_sample.reshape(TL, D)], axis=0)
    new_k, new_v, new_ckv, new_kr, new_s = [], [], [], [], []
    tq_c = min(256, S)
    tq_l, tk_l = min(1024, DS), min(512, P, DS)
    for l in range(L):
        z = _in_proj(x, norm1_g[l][None], modv, w_in_b[l], mod_row_of(tm, l), tm)
        gq, gkn, gkr, gvt, mq, ckv, mk, mvt = _prep(
            z, tabg, tabm, table_of(tm_small), tm_small, gqa_qn[l][None], gqa_kn[l][None], mla_qa_n[l][None],
            mla_kva_n[l][None], wqb[l], wkvb[l], mqn_n[l], mqn_r[l], mkn_n[l], mkn_r[l])
        new_k.append(gkn[:TC].reshape(B, S, GQA_KV_HEADS, HEAD_DIM))
        new_v.append(z[:TC, 1024:1280].reshape(B, S, GQA_KV_HEADS, HEAD_DIM))
        new_ckv.append(ckv[:TC].reshape(B, S, MLA_RANK))
        new_kr.append(z[:TC, 1280:1280 + MLA_ROPE].reshape(B, S, MLA_ROPE))

        gqa = dict(heads=GQA_HEADS, group=GQA_HEADS // GQA_KV_HEADS, dq=HEAD_DIM, dv=HEAD_DIM, hb=GQA_HEADS // GQA_KV_HEADS)
        mla = dict(heads=MLA_HEADS, group=1, dq=MLA_PAD, dv=HEAD_DIM, hb=MLA_HEADS // 2)
        ctx_a = dict(batch=B, nq=S, nk=S, row0=0, tq=tq_c, tk=tq_c)
        lat_a = dict(batch=DB, nq=DS, nk=DS, row0=TC, tq=tq_l, tk=tk_l)
        ga = _attention(gq, gkr, gvt, **gqa, **ctx_a)
        ga = _attention(gq, gkr, gvt, **gqa, **lat_a, cache=(cgk[l], cgv[l]), prev=ga)
        ma = _attention(mq, mk, mvt, **mla, **ctx_a)
        ma = _attention(mq, mk, mvt, **mla, **lat_a, cache=(cmk[l], cmv[l]), prev=ma)

        o2, s_c = _hgrn(z, lbtab[l], s0_ctx, batch=B, n=S, row0=0)
        o2, _ = _hgrn(z, lbtab[l], state_hgrn[:, l].astype(F32), batch=DB, n=DS, row0=TC, prev=o2)
        new_s.append(s_c)

        x, hhi, hlo, ht = _mixer_out(x, ga, o2, z, ma, w_out_b[l], hg_on[l][None], norm2_g[l][None], modv,
                                 mod_row_of(tm_small, l), tm_small)
        st = _scores(ws_hi[l], ws_lo[l], hhi, hlo, tm)
        thr, s1n = _topk(st, tm)
        x = _peer(ht, st, thr, s1n, u_b[l], vt_b[l], x, modv, mod_row_of(tm, l), tm)

    return (x[:TC].reshape(B, S, D), x[TC:].reshape(DB, DS, D),
            jnp.stack(new_k, axis=1), jnp.stack(new_v, axis=1), jnp.stack(new_ckv, axis=1),
            jnp.stack(new_kr, axis=1), jnp.stack(new_s, axis=1))
```

```python
import functools

import numpy as np
import jax
import jax.numpy as jnp
from jax import lax
from jax.experimental import pallas as pl
from jax.experimental.pallas import tpu as pltpu

F32 = jnp.float32
BF16 = jnp.bfloat16

EPS = 1e-6
LOG2E = 1.4426950408889634
ROPE_THETA = 10000.0
GRID_W = 64

HEAD_DIM = 128
GQA_HEADS, GQA_KV_HEADS = 6, 2
HG_HEADS = 4
MLA_HEADS, MLA_ROPE, MLA_RANK = 6, 64, 512
MLA_QK = HEAD_DIM + MLA_ROPE
MLA_PAD = 2 * HEAD_DIM
PEER_HEADS, PEER_KEYS, PEER_TOPK = 8, 128, 16
HG_CHUNK = 128
PEER_EXPERT_TILE = 512
ATTN_QUERY_CHUNK = 512
IN_TILE = 1024

VMEM_LIMIT = 60 * 1024 * 1024


def _cparams(*sem):
    return pltpu.CompilerParams(dimension_semantics=sem, vmem_limit_bytes=VMEM_LIMIT)


def _dot(a, b):
    return jnp.dot(a, b, preferred_element_type=F32)


def _dot_nt(a, b):
    return lax.dot_general(a, b, (((1,), (1,)), ((), ())), preferred_element_type=F32)


def _dot_tn(a, b):
    return lax.dot_general(a, b, (((0,), (0,)), ((), ())), preferred_element_type=F32)


def _split2(a):
    hi = a.astype(BF16)
    lo = (a - hi.astype(F32)).astype(BF16)
    return hi, lo


def _split3(a):
    p1 = a.astype(BF16)
    r = a - p1.astype(F32)
    p2 = r.astype(BF16)
    p3 = (r - p2.astype(F32)).astype(BF16)
    return p1, p2, p3


def _rms(x, n):
    return x * lax.rsqrt(jnp.sum(x * x, axis=-1, keepdims=True) * (1.0 / n) + EPS)


def _mod_kernel(cond_ref, w_ref, b_ref, o_ref):
    c = cond_ref[...]
    s = c * (1.0 / (1.0 + jnp.exp(-c)))
    shi, slo = _split2(s)
    whi, wlo = _split2(w_ref[...])
    o_ref[...] = _dot(shi, whi) + _dot(slo, whi) + _dot(shi, wlo) + b_ref[...]


def _modulation(cond8, w_mod, b_mod):
    L, D, N = w_mod.shape
    tn = min(512, N)
    return pl.pallas_call(
        _mod_kernel,
        grid=(L, N // tn),
        in_specs=[pl.BlockSpec((8, D), lambda l, j: (0, 0)),
                  pl.BlockSpec((None, D, tn), lambda l, j: (l, 0, j)),
                  pl.BlockSpec((None, 1, tn), lambda l, j: (l, 0, j))],
        out_specs=pl.BlockSpec((None, 8, tn), lambda l, j: (l, 0, j)),
        out_shape=jax.ShapeDtypeStruct((L, 8, N), F32),
        compiler_params=_cparams("parallel", "parallel"),
        name="modulation",
    )(cond8, w_mod, b_mod.reshape(L, 1, N))


def _in_kernel(x_ref, g_ref, sh_ref, sc_ref, w_ref, o_ref, h_sc):
    @pl.when(pl.program_id(1) == 0)
    def _():
        x = x_ref[...]
        h = _rms(x, x.shape[-1]) * g_ref[...] * (1.0 + sc_ref[...]) + sh_ref[...]
        h_sc[...] = h.astype(BF16)

    o_ref[...] = _dot(h_sc[...], w_ref[...])


def _in_proj(x, g, modv, w, mod_row_of_tile, tm):
    T, D = x.shape
    nj, _, tn = w.shape
    N = nj * tn

    def mspec(k):
        return pl.BlockSpec((None, 1, D), lambda i, j: (mod_row_of_tile(i) + k, 0, 0))

    return pl.pallas_call(
        _in_kernel,
        grid=(T // tm, nj),
        in_specs=[pl.BlockSpec((tm, D), lambda i, j: (i, 0)),
                  pl.BlockSpec((1, D), lambda i, j: (0, 0)),
                  mspec(0), mspec(1),
                  pl.BlockSpec((None, D, tn), lambda i, j: (j, 0, 0))],
        out_specs=pl.BlockSpec((tm, tn), lambda i, j: (i, j)),
        out_shape=jax.ShapeDtypeStruct((T, N), F32),
        scratch_shapes=[pltpu.VMEM((tm, D), BF16)],
        compiler_params=_cparams("parallel", "arbitrary"),
        name="in_proj",
    )(x, g, modv, modv, w)


def _rope(x, tab_ref, blk):
    return (x * tab_ref[0] + pltpu.roll(x, blk, 1) * tab_ref[1]
            + pltpu.roll(x, HEAD_DIM - blk, 1) * tab_ref[2])


def _mla_kv(ckv, krope, wkvb_ref, gkn_ref, gkr_ref, tab_ref, mk_ref, mv_ref):
    kv = _dot(ckv.astype(BF16), wkvb_ref[...])
    nv = MLA_HEADS * HEAD_DIM
    mv_ref[...] = kv[:, nv:].T.astype(BF16)
    ssr = jnp.sum(krope * krope, axis=-1, keepdims=True)
    for h in range(MLA_HEADS):
        kn = kv[:, h * HEAD_DIM:(h + 1) * HEAD_DIM]
        rs = lax.rsqrt((jnp.sum(kn * kn, axis=-1, keepdims=True) + ssr) * (1.0 / MLA_QK) + EPS)
        kr = krope * rs * gkr_ref[...]
        if tab_ref is not None:
            kr = _rope(kr, tab_ref, MLA_ROPE // 4)
        mk_ref[:, h * MLA_PAD:h * MLA_PAD + HEAD_DIM] = (kn * rs * gkn_ref[...]).astype(BF16)
        mk_ref[:, h * MLA_PAD + HEAD_DIM:(h + 1) * MLA_PAD] = kr.astype(BF16)


def _prep_kernel(gq_ref, gk_ref, gv_ref, mkr_ref, mqa_ref, mkva_ref, tg_ref, tm_ref,
                 gqn_ref, gkn_ref, qan_ref, kvan_ref, wqb_ref, wkvb_ref,
                 mqn_n_ref, mqn_r_ref, mkn_n_ref, mkn_r_ref,
                 gq_o, gkn_o, gkr_o, gvt_o, mq_o, ckv_o, mk_o, mv_o):
    gscale = HEAD_DIM ** -0.5 * LOG2E
    for h in range(GQA_HEADS):
        sl = slice(h * HEAD_DIM, (h + 1) * HEAD_DIM)
        q = _rms(gq_ref[:, sl], HEAD_DIM) * gqn_ref[...]
        gq_o[:, sl] = (_rope(q, tg_ref, HEAD_DIM // 4) * gscale).astype(BF16)
    for h in range(GQA_KV_HEADS):
        sl = slice(h * HEAD_DIM, (h + 1) * HEAD_DIM)
        k = _rms(gk_ref[:, sl], HEAD_DIM) * gkn_ref[...]
        gkn_o[:, sl] = k
        gkr_o[:, sl] = _rope(k, tg_ref, HEAD_DIM // 4).astype(BF16)
    gvt_o[...] = gv_ref[...].T.astype(BF16)

    qa = _rms(mqa_ref[...], MLA_RANK) * qan_ref[...]
    mq = _dot(qa.astype(BF16), wqb_ref[...])
    mscale = MLA_QK ** -0.5 * LOG2E
    for h in range(MLA_HEADS):
        qn = mq[:, h * MLA_PAD:h * MLA_PAD + HEAD_DIM]
        qr = mq[:, h * MLA_PAD + HEAD_DIM:(h + 1) * MLA_PAD]
        ss = jnp.sum(qn * qn, axis=-1, keepdims=True) + jnp.sum(qr * qr, axis=-1, keepdims=True)
        rs = lax.rsqrt(ss * (1.0 / MLA_QK) + EPS) * mscale
        mq_o[:, h * MLA_PAD:h * MLA_PAD + HEAD_DIM] = (qn * rs * mqn_n_ref[...]).astype(BF16)
        mq_o[:, h * MLA_PAD + HEAD_DIM:(h + 1) * MLA_PAD] = _rope(qr * rs * mqn_r_ref[...], tm_ref, MLA_ROPE // 4).astype(BF16)

    ckv = _rms(mkva_ref[...], MLA_RANK) * kvan_ref[...]
    ckv_o[...] = ckv
    _mla_kv(ckv, mkr_ref[...], wkvb_ref, mkn_n_ref, mkn_r_ref, tm_ref, mk_o, mv_o)


def _prep(z, tabg, tabm, table_of_tile, tm, gqn, gkn, qan, kvan, wqb, wkvb, mqn_n, mqn_r, mkn_n, mkn_r):
    T = z.shape[0]
    row = lambda w: pl.BlockSpec((1, w), lambda i: (0, 0))
    full = lambda a: pl.BlockSpec(a.shape, lambda i: (0, 0))
    tab = pl.BlockSpec((3, tm, HEAD_DIM), lambda i: (0, table_of_tile(i), 0))
    zspec = lambda w, blk: pl.BlockSpec((tm, w), lambda i: (i, blk))
    widths = (GQA_HEADS * HEAD_DIM, GQA_KV_HEADS * HEAD_DIM, GQA_KV_HEADS * HEAD_DIM, GQA_KV_HEADS * HEAD_DIM,
              MLA_HEADS * MLA_PAD, MLA_RANK, MLA_HEADS * MLA_PAD, MLA_HEADS * HEAD_DIM)
    dtypes = (BF16, F32, BF16, BF16, BF16, F32, BF16, BF16)
    transposed = (False, False, False, True, False, False, False, True)
    ospec = lambda w, t: pl.BlockSpec((w, tm), lambda i: (0, i)) if t else pl.BlockSpec((tm, w), lambda i: (i, 0))
    return pl.pallas_call(
        _prep_kernel,
        grid=(T // tm,),
        in_specs=[zspec(768, 0), zspec(256, 3), zspec(256, 4), zspec(128, 10), zspec(512, 3), zspec(512, 4),
                  tab, tab, row(128), row(128), row(512), row(512), full(wqb), full(wkvb),
                  row(128), row(128), row(128), row(128)],
        out_specs=[ospec(w, t) for w, t in zip(widths, transposed)],
        out_shape=[jax.ShapeDtypeStruct((w, T) if t else (T, w), d) for w, d, t in zip(widths, dtypes, transposed)],
        compiler_params=_cparams("parallel"),
        name="prep",
    )(z, z, z, z, z, z, tabg, tabm, gqn, gkn, qan, kvan, wqb, wkvb, mqn_n, mqn_r, mkn_n, mkn_r)


def _cache_kernel(ckv_ref, kr_ref, wkvb_ref, gkn_ref, gkr_ref, mk_o, mv_o):
    _mla_kv(ckv_ref[...], kr_ref[...], wkvb_ref, gkn_ref, gkr_ref, None, mk_o, mv_o)


def _cache_expand(ckv, kr, wkvb, mkn_n, mkn_r):
    L, R, _ = ckv.shape
    tm = min(512, R)
    return pl.pallas_call(
        _cache_kernel,
        grid=(L, R // tm),
        in_specs=[pl.BlockSpec((None, tm, MLA_RANK), lambda l, i: (l, i, 0)),
                  pl.BlockSpec((None, tm, HEAD_DIM), lambda l, i: (l, i, 0)),
                  pl.BlockSpec((None,) + wkvb.shape[1:], lambda l, i: (l, 0, 0)),
                  pl.BlockSpec((None, 1, HEAD_DIM), lambda l, i: (l, 0, 0)),
                  pl.BlockSpec((None, 1, HEAD_DIM), lambda l, i: (l, 0, 0))],
        out_specs=[pl.BlockSpec((None, tm, MLA_HEADS * MLA_PAD), lambda l, i: (l, i, 0)),
                   pl.BlockSpec((None, MLA_HEADS * HEAD_DIM, tm), lambda l, i: (l, 0, i))],
        out_shape=[jax.ShapeDtypeStruct((L, R, MLA_HEADS * MLA_PAD), BF16),
                   jax.ShapeDtypeStruct((L, MLA_HEADS * HEAD_DIM, R), BF16)],
        compiler_params=_cparams("parallel", "parallel"),
        name="cache_expand",
    )(ckv, kr, wkvb, mkn_n, mkn_r)


def _attn_kernel(*refs, heads, group, dq, dv, ncb, has_prev):
    n_in = (5 if ncb else 3) + (1 if has_prev else 0)
    q_ref = refs[0]
    o_ref, m_sc, l_sc, acc_sc = refs[n_in:]
    ki = pl.program_id(3)

    @pl.when(ki == 0)
    def _():
        m_sc[...] = jnp.full_like(m_sc, -jnp.inf)
        l_sc[...] = jnp.zeros_like(l_sc)
        acc_sc[...] = jnp.zeros_like(acc_sc)

    tq = q_ref.shape[0]
    cq = min(ATTN_QUERY_CHUNK, tq)
    units = [(h, c) for h in range(heads) for c in range(tq // cq)]

    def step(k_ref, vt_ref):
        def scores(h, c):
            kv = h // group
            return _dot_nt(k_ref[:, kv * dq:(kv + 1) * dq], q_ref[c * cq:(c + 1) * cq, h * dq:(h + 1) * dq])

        s_next = scores(*units[0])
        for i, (h, c) in enumerate(units):
            kv = h // group
            cols = slice(c * cq, (c + 1) * cq)
            s = s_next
            if i + 1 < len(units):
                s_next = scores(*units[i + 1])
            m_prev = m_sc[h, :, cols]
            m_new = jnp.maximum(m_prev, jnp.max(s, axis=0, keepdims=True))
            p = jnp.exp2(s - m_new)
            alpha = jnp.exp2(m_prev - m_new)
            l_sc[h, :, cols] = alpha * l_sc[h, :, cols] + jnp.sum(p, axis=0, keepdims=True)
            acc_sc[h, :, cols] = alpha * acc_sc[h, :, cols] + _dot(vt_ref[kv * dv:(kv + 1) * dv, :], p.astype(BF16))
            m_sc[h, :, cols] = m_new

    if ncb:
        pl.when(ki < ncb)(lambda: step(refs[1], refs[2]))
        pl.when(ki >= ncb)(lambda: step(refs[3], refs[4]))
    else:
        step(refs[1], refs[2])

    @pl.when(ki == pl.num_programs(3) - 1)
    def _():
        for h in range(heads):
            o_ref[:, h * dv:(h + 1) * dv] = (acc_sc[h] / l_sc[h]).T.astype(o_ref.dtype)


def _attention(q, k, vt, *, batch, nq, nk, row0, heads, group, dq, dv, tq, tk, hb, cache=None, prev=None):
    T = q.shape[0]
    nqb, nnb = nq // tq, nk // tk
    ncb = 0 if cache is None else cache[0].shape[0] // batch // tk
    qb0, kb0 = row0 // tq, row0 // tk
    kvb = max(1, hb // group)
    new_blk = lambda b, j: kb0 + b * nnb + jnp.maximum(j - ncb, 0)
    in_specs = [pl.BlockSpec((tq, hb * dq), lambda b, h, i, j: (qb0 + b * nqb + i, h))]
    operands = [q]
    if ncb:
        cache_blk = lambda b, j: b * ncb + jnp.minimum(j, ncb - 1)
        in_specs += [pl.BlockSpec((tk, kvb * dq), lambda b, h, i, j: (cache_blk(b, j), h)),
                     pl.BlockSpec((kvb * dv, tk), lambda b, h, i, j: (h, cache_blk(b, j)))]
        operands += list(cache)
    in_specs += [pl.BlockSpec((tk, kvb * dq), lambda b, h, i, j: (new_blk(b, j), h)),
                 pl.BlockSpec((kvb * dv, tk), lambda b, h, i, j: (h, new_blk(b, j)))]
    operands += [k, vt]
    aliases = {}
    if prev is not None:
        in_specs.append(pl.BlockSpec(memory_space=pl.ANY))
        aliases = {len(operands): 0}
        operands.append(prev)
    return pl.pallas_call(
        functools.partial(_attn_kernel, heads=hb, group=group, dq=dq, dv=dv, ncb=ncb, has_prev=prev is not None),
        grid=(batch, heads // hb, nqb, ncb + nnb),
        in_specs=in_specs,
        out_specs=pl.BlockSpec((tq, hb * dv), lambda b, h, i, j: (qb0 + b * nqb + i, h)),
        out_shape=jax.ShapeDtypeStruct((T, heads * dv), BF16),
        scratch_shapes=[pltpu.VMEM((hb, 1, tq), F32), pltpu.VMEM((hb, 1, tq), F32), pltpu.VMEM((hb, dv, tq), F32)],
        input_output_aliases=aliases,
        compiler_params=_cparams("parallel", "parallel", "parallel", "arbitrary"),
        name="attention",
    )(*operands)


def _hgrn_constants(c):
    levels = int(np.log2(c))
    tri = np.zeros((2, c, c), np.float32)
    M = np.zeros((2, levels, c, c), np.float32)
    tri[0] = np.tril(np.ones((c, c), np.float32))
    for lv in range(levels):
        m = c >> (lv + 1)
        for a in range(0, c, 2 * m):
            M[0, lv, a + m:a + 2 * m, a:a + m] = 1.0
    tri[1] = tri[0][::-1, ::-1]
    M[1] = M[0][:, ::-1, ::-1]
    return tri, M


def _level_exponents(bq, bwd, c):
    rowid = lax.broadcasted_iota(jnp.int32, bq.shape, 0)
    out = []
    m = c // 2
    while m >= 1:
        if 2 * m >= 8:
            blocks = [jnp.broadcast_to(jnp.where(bwd, bq[a + m:a + m + 1, :], bq[a + m - 1:a + m, :]), (2 * m, bq.shape[1]))
                      for a in range(0, c, 2 * m)]
            ref = blocks[0] if len(blocks) == 1 else jnp.concatenate(blocks, axis=0)
        else:
            target = m - 1 + bwd.astype(jnp.int32)
            rowmod = rowid % (2 * m)
            ref = bq
            for delta in range(-m, m + 1):
                if delta != 0:
                    ref = jnp.where(rowmod + delta == target, pltpu.roll(bq, (-delta) % c, 0), ref)
        out.append(-jnp.abs(bq - ref))
        m //= 2
    return out


def _hgrn_kernel(q_ref, z_ref, v_ref, lb_ref, s0_ref, L_ref, M_ref, o_ref, sf_ref, s_sc):
    c = q_ref.shape[0]
    levels = M_ref.shape[0]

    @pl.when(pl.program_id(2) == 0)
    def _():
        s_sc[...] = s0_ref[...]

    z = z_ref[...]
    e = jnp.exp(-jnp.abs(z))
    r = 1.0 / (1.0 + e)
    logsig = jnp.minimum(z, 0.0) + jnp.log(r)
    a = lb_ref[0:1, :]
    cc = lb_ref[1:2, :] + logsig
    logf = jnp.maximum(a, cc) + jnp.log(1.0 + jnp.exp(-jnp.abs(a - cc)))
    kk = lb_ref[2:3, :] * jnp.where(z >= 0.0, e * r, r)

    tri = L_ref[...]
    p1, p2, p3 = _split3(logf)
    bq = _dot(tri, p1) + _dot(tri, p2) + _dot(tri, p3)
    tot = jnp.minimum(bq[0:1, :], bq[c - 1:c, :])
    ex_q = jnp.exp(bq)
    ex_k = jnp.exp(tot - bq)
    ex_lv = [jnp.exp(t) for t in _level_exponents(bq, pl.program_id(0) == 1, c)]
    ones = jnp.ones((c, HEAD_DIM), BF16)
    for h in range(HG_HEADS):
        sl = slice(h * HEAD_DIM, (h + 1) * HEAD_DIM)
        q, k, v = q_ref[:, sl], kk[:, sl], v_ref[:, sl]
        vb = v.astype(BF16)
        amat = jnp.zeros((c, c), F32)
        for lv in range(levels):
            el = ex_lv[lv][:, sl]
            amat = amat + M_ref[lv] * _dot_nt((q * el).astype(BF16), (k * el).astype(BF16))
        s_prev = s_sc[h]
        o = (_dot(amat.astype(BF16), vb) + jnp.sum(q * k, axis=-1, keepdims=True) * v
             + _dot((q * ex_q[:, sl]).astype(BF16), s_prev.astype(BF16)))
        o_ref[:, sl] = o
        dcol = jnp.exp(_dot_tn(p1[:, sl], ones) + _dot_tn(p2[:, sl], ones) + _dot_tn(p3[:, sl], ones))
        s_new = dcol * s_prev + _dot_tn((k * ex_k[:, sl]).astype(BF16), vb)
        s_sc[h] = s_new
        sf_ref[h] = s_new


def _hgrn(z, lbtab, s0, *, batch, n, row0, prev=None):
    c = min(HG_CHUNK, n)
    nc = n // c
    cb0 = row0 // c
    Lm, Mm = _hgrn_constants(c)
    Lm, Mm = jnp.asarray(Lm, BF16), jnp.asarray(Mm, F32)
    w = HG_HEADS * HEAD_DIM

    def rows(d, b, j):
        return cb0 + b * nc + j + d * (nc - 1 - 2 * j)

    kernel, extra_specs, extra, aliases = _hgrn_kernel, [], [], {}
    if prev is not None:
        kernel = lambda *refs: _hgrn_kernel(*refs[:7], *refs[8:])
        extra_specs, extra, aliases = [pl.BlockSpec(memory_space=pl.ANY)], [prev], {7: 0}
    return pl.pallas_call(
        kernel,
        grid=(2, batch, nc),
        in_specs=[pl.BlockSpec((c, w), lambda d, b, j: (rows(d, b, j), 5)),
                  pl.BlockSpec((c, w), lambda d, b, j: (rows(d, b, j), 6 + d)),
                  pl.BlockSpec((c, w), lambda d, b, j: (rows(d, b, j), 8)),
                  pl.BlockSpec((None, 8, w), lambda d, b, j: (d, 0, 0)),
                  pl.BlockSpec((None, None, HG_HEADS, HEAD_DIM, HEAD_DIM), lambda d, b, j: (b, d, 0, 0, 0)),
                  pl.BlockSpec((None,) + Lm.shape[1:], lambda d, b, j: (d, 0, 0)),
                  pl.BlockSpec((None,) + Mm.shape[1:], lambda d, b, j: (d, 0, 0, 0))] + extra_specs,
        out_specs=[pl.BlockSpec((None, c, w), lambda d, b, j: (d, rows(d, b, j), 0)),
                   pl.BlockSpec((None, None, HG_HEADS, HEAD_DIM, HEAD_DIM), lambda d, b, j: (b, d, 0, 0, 0))],
        out_shape=[jax.ShapeDtypeStruct((2, z.shape[0], w), F32),
                   jax.ShapeDtypeStruct((batch, 2, HG_HEADS, HEAD_DIM, HEAD_DIM), F32)],
        scratch_shapes=[pltpu.VMEM((HG_HEADS, HEAD_DIM, HEAD_DIM), F32)],
        input_output_aliases=aliases,
        compiler_params=_cparams("parallel", "parallel", "arbitrary"),
        name="hgrn_scan",
    )(z, z, z, lbtab, s0, Lm, Mm, *extra)


def _out_kernel(x_ref, ga_ref, of_ref, ob_ref, hg_ref, ma_ref, w_ref, on_ref, g2_ref,
                g1_ref, sh2_ref, sc2_ref, xo_ref, hhi_ref, hlo_ref, ht_ref):
    o = of_ref[...] + ob_ref[...]
    g = hg_ref[...]
    g = g * (1.0 / (1.0 + jnp.exp(-g)))
    n_g, n_h = GQA_HEADS * HEAD_DIM, HG_HEADS * HEAD_DIM
    mix = _dot(ga_ref[...], w_ref[0:n_g, :]) + _dot(ma_ref[...], w_ref[n_g + n_h:, :])
    for h in range(HG_HEADS):
        sl = slice(h * HEAD_DIM, (h + 1) * HEAD_DIM)
        oh = _rms(o[:, sl], HEAD_DIM) * on_ref[...] * g[:, sl]
        mix = mix + _dot(oh.astype(BF16), w_ref[n_g + h * HEAD_DIM:n_g + (h + 1) * HEAD_DIM, :])
    x = x_ref[...] + g1_ref[...] * mix
    xo_ref[...] = x
    h2 = _rms(x, x.shape[-1]) * g2_ref[...] * (1.0 + sc2_ref[...]) + sh2_ref[...]
    hi, lo = _split2(h2)
    hhi_ref[...] = hi
    hlo_ref[...] = lo
    ht_ref[...] = h2.T.astype(BF16)


def _mixer_out(x, ga, o2, z, ma, w, on, g2, modv, mod_row_of_tile, tm):
    T, D = x.shape
    wh = HG_HEADS * HEAD_DIM

    def mspec(k):
        return pl.BlockSpec((None, 1, D), lambda i: (mod_row_of_tile(i) + k, 0, 0))

    return pl.pallas_call(
        _out_kernel,
        grid=(T // tm,),
        in_specs=[pl.BlockSpec((tm, D), lambda i: (i, 0)),
                  pl.BlockSpec((tm, ga.shape[1]), lambda i: (i, 0)),
                  pl.BlockSpec((None, tm, wh), lambda i: (0, i, 0)),
                  pl.BlockSpec((None, tm, wh), lambda i: (1, i, 0)),
                  pl.BlockSpec((tm, wh), lambda i: (i, 9)),
                  pl.BlockSpec((tm, ma.shape[1]), lambda i: (i, 0)),
                  pl.BlockSpec(w.shape, lambda i: (0, 0)),
                  pl.BlockSpec((1, HEAD_DIM), lambda i: (0, 0)),
                  pl.BlockSpec((1, D), lambda i: (0, 0)),
                  mspec(2), mspec(3), mspec(4)],
        out_specs=[pl.BlockSpec((tm, D), lambda i: (i, 0))] * 3 + [pl.BlockSpec((D, tm), lambda i: (0, i))],
        out_shape=[jax.ShapeDtypeStruct((T, D), F32), jax.ShapeDtypeStruct((T, D), BF16),
                   jax.ShapeDtypeStruct((T, D), BF16), jax.ShapeDtypeStruct((D, T), BF16)],
        compiler_params=_cparams("parallel"),
        name="mixer_out",
    )(x, ga, o2, o2, z, ma, w, on, g2, modv, modv, modv)


def _ws_kernel(k_ref, wq_ref, hi_ref, lo_ref):
    khi, klo = _split2(k_ref[...])
    whi, wlo = _split2(wq_ref[...])
    acc = _dot_nt(khi, whi) + _dot_nt(klo, whi) + _dot_nt(khi, wlo)
    hi, lo = _split2(acc * LOG2E)
    hi_ref[...] = hi
    lo_ref[...] = lo


def _score_weights(keys, wq):
    L, D, _ = wq.shape
    nb = 2 * PEER_HEADS
    sub = keys.shape[-1]
    kk = keys.reshape(L, nb, PEER_KEYS, sub)
    out = jax.ShapeDtypeStruct((L, nb * PEER_KEYS, D), BF16)
    ospec = pl.BlockSpec((None, PEER_KEYS, D), lambda l, b: (l, (b % 2) * PEER_HEADS + b // 2, 0))
    return pl.pallas_call(
        _ws_kernel,
        grid=(L, nb),
        in_specs=[pl.BlockSpec((None, None, PEER_KEYS, sub), lambda l, b: (l, b, 0, 0)),
                  pl.BlockSpec((None, D, sub), lambda l, b: (l, 0, b))],
        out_specs=[ospec, ospec],
        out_shape=[out, out],
        compiler_params=_cparams("parallel", "parallel"),
        name="score_weights",
    )(kk, wq)


def _score_kernel(whi_ref, wlo_ref, hhi_ref, hlo_ref, o_ref):
    whi, hhi = whi_ref[...], hhi_ref[...]
    o_ref[...] = _dot_nt(whi, hhi) + _dot_nt(whi, hlo_ref[...]) + _dot_nt(wlo_ref[...], hhi)


def _scores(whi, wlo, hhi, hlo, tm):
    R, D = whi.shape
    T = hhi.shape[0]
    tr = min(512, R)
    return pl.pallas_call(
        _score_kernel,
        grid=(R // tr, T // tm),
        in_specs=[pl.BlockSpec((tr, D), lambda r, i: (r, 0)), pl.BlockSpec((tr, D), lambda r, i: (r, 0)),
                  pl.BlockSpec((tm, D), lambda r, i: (i, 0)), pl.BlockSpec((tm, D), lambda r, i: (i, 0))],
        out_specs=pl.BlockSpec((tr, tm), lambda r, i: (r, i)),
        out_shape=jax.ShapeDtypeStruct((R, T), F32),
        compiler_params=_cparams("parallel", "parallel"),
        name="scores",
    )(whi, wlo, hhi, hlo)


def _batcher_pairs(n):
    pairs = []
    p = 1
    while p < n:
        k = p
        while k >= 1:
            for j in range(k % p, n - k, 2 * k):
                for i in range(min(k, n - j - k)):
                    if (i + j) // (2 * p) == (i + j + k) // (2 * p):
                        pairs.append((i + j, i + j + k))
            k //= 2
        p *= 2
    return pairs


_SORT16 = _batcher_pairs(PEER_TOPK)


def _sort_desc(v):
    v = list(v)
    for i, j in _SORT16:
        v[i], v[j] = jnp.maximum(v[i], v[j]), jnp.minimum(v[i], v[j])
    return v


def _merge_top(a, b):
    n = len(a)
    v = [jnp.maximum(a[i], b[n - 1 - i]) for i in range(n)]
    s = n // 2
    while s >= 1:
        for i in range(n):
            if (i & s) == 0:
                v[i], v[i + s] = jnp.maximum(v[i], v[i + s]), jnp.minimum(v[i], v[i + s])
        s //= 2
    return v


def _top16_of_keys(s_ref, row0, lanes):
    k = PEER_TOPK
    groups = [_sort_desc([s_ref[pl.ds(row0 + PEER_HEADS * (k * g + i), PEER_HEADS), lanes] for i in range(k)])
              for g in range(PEER_KEYS // k)]
    while len(groups) > 1:
        groups = [_merge_top(groups[i], groups[i + 1]) for i in range(0, len(groups), 2)]
    return groups[0]


def _topk_kernel(s_ref, thr_ref, s1n_ref):
    k = PEER_TOPK
    half = PEER_HEADS * PEER_KEYS
    ninf = jnp.full((PEER_HEADS, 128), -jnp.inf, F32)

    def chunk(ch, carry):
        lanes = pl.ds(pl.multiple_of(ch * 128, 128), 128)
        a = _top16_of_keys(s_ref, 0, lanes)
        b = _top16_of_keys(s_ref, half, lanes)
        cand = [[a[i] + b[j] for j in range(k // (i + 1))] for i in range(k)]
        g0 = cand[0]
        g1 = _sort_desc(cand[1] + cand[2] + cand[5] + cand[8])
        g2 = _sort_desc(cand[3] + cand[4] + cand[6] + cand[7] + cand[9] + cand[10] + cand[11] + cand[12] + cand[13])
        g3 = cand[14] + cand[15] + [ninf] * (k - 2)
        top = _merge_top(_merge_top(g0, g1), _merge_top(g2, g3))
        zsum = jnp.zeros((PEER_HEADS, 128), F32)
        for t in top:
            zsum = zsum + jnp.exp2(t - top[0])
        c0 = top[0] + jnp.log2(zsum)
        thr_ref[:, lanes] = top[k - 1] - c0
        for key in range(PEER_KEYS):
            tile = s_ref[pl.ds(half + PEER_HEADS * key, PEER_HEADS), lanes] - c0
            s1n_ref[ch, pl.ds(key, PEER_HEADS, stride=PEER_KEYS), :] = tile
        return carry

    lax.fori_loop(0, s_ref.shape[1] // 128, chunk, 0)


def _topk(st, tm):
    R, T = st.shape
    return pl.pallas_call(
        _topk_kernel,
        grid=(T // tm,),
        in_specs=[pl.BlockSpec((R, tm), lambda i: (0, i))],
        out_specs=[pl.BlockSpec((PEER_HEADS, tm), lambda i: (0, i)),
                   pl.BlockSpec((tm // 128, R // 2, 128), lambda i: (i, 0, 0))],
        out_shape=[jax.ShapeDtypeStruct((PEER_HEADS, T), F32), jax.ShapeDtypeStruct((T // 128, R // 2, 128), F32)],
        compiler_params=_cparams("parallel"),
        name="topk",
    )(st)


def _peer_kernel(ht_ref, s0_ref, s1n_ref, thr_ref, u_ref, vt_ref, x_ref, g2_ref, o_ref,
                 pre_a, pre_b, c_a, c_b, acc_sc):
    s = pl.program_id(1)
    te, tm = c_a.shape
    nk = PEER_KEYS

    ns = pl.num_programs(1)

    @pl.when(s == 0)
    def _():
        acc_sc[...] = jnp.zeros_like(acc_sc)

    d = ht_ref.shape[0]
    ni = te // nk
    jr = 16
    nblk = (tm // 128) * (nk // jr)
    nh = 2 if tm % 256 == 0 else 1
    th = tm // nh
    ka = mc = max(1, min(nblk // nh, d // 256))
    kq, mq = d // ka, d // mc

    def pre_piece(p, pre_w):
        half, q = divmod(p, ka)
        part = _dot(u_ref[:, q * kq:(q + 1) * kq], ht_ref[q * kq:(q + 1) * kq, half * th:(half + 1) * th])
        if q == 0:
            pre_w[:, half * th:(half + 1) * th] = part
        else:
            pre_w[:, half * th:(half + 1) * th] += part

    def value_piece(p, c_r):
        half, q = divmod(p, mc)
        acc_sc[q * mq:(q + 1) * mq, half * th:(half + 1) * th] += _dot(
            vt_ref[q * mq:(q + 1) * mq, :], c_r[:, half * th:(half + 1) * th])

    slots = {}
    for p in range(nh * ka):
        slots.setdefault(p * nblk // (nh * ka), []).append((pre_piece, p))
    for p in range(nh * mc):
        slots.setdefault((p * nblk // (nh * mc) + 1) % nblk, []).append((value_piece, p))

    def stages(pre_w, pre_r, c_w, c_r):
        for blk in range(nblk):
            for piece, p in slots.get(blk, ()):
                buf = pre_w if piece is pre_piece else c_r
                if buf is not None:
                    piece(p, buf)
            if pre_r is None:
                continue
            ch, jq = divmod(blk, nk // jr)
            lanes = slice(ch * 128, (ch + 1) * 128)
            gates = [jnp.zeros((jr, 128), F32) for _ in range(ni)]
            for h in range(PEER_HEADS):
                s1 = s1n_ref[ch, h * nk + jq * jr:h * nk + (jq + 1) * jr, :]
                thr = thr_ref[h:h + 1, lanes]
                for ii in range(ni):
                    r0 = ii * PEER_HEADS + h
                    val = s1 + s0_ref[r0:r0 + 1, lanes]
                    gates[ii] = gates[ii] + jnp.where(val >= thr, jnp.exp2(val), 0.0)
            for ii in range(ni):
                rows = slice(ii * nk + jq * jr, ii * nk + (jq + 1) * jr)
                pre = pre_r[rows, lanes]
                act = 0.5 * pre * (1.0 + lax.erf(pre * (2.0 ** -0.5)))
                c_w[rows, lanes] = (gates[ii] * act).astype(BF16)

    steady = jnp.logical_and(s >= 2, s < ns - 2)
    pl.when(jnp.logical_and(steady, s % 2 == 0))(lambda: stages(pre_a, pre_b, c_b, c_a))
    pl.when(jnp.logical_and(steady, s % 2 == 1))(lambda: stages(pre_b, pre_a, c_a, c_b))
    pl.when(s == 0)(lambda: stages(pre_a, None, None, None))
    pl.when(s == 1)(lambda: stages(pre_b, pre_a, c_a, None))
    pl.when(s == ns - 2)(lambda: stages(None, pre_b, c_b, c_a))
    pl.when(s == ns - 1)(lambda: stages(None, None, None, c_b))

    @pl.when(s == ns - 1)
    def _():
        o_ref[...] = x_ref[...] + g2_ref[...] * acc_sc[...].T


def _peer(ht, st, thr, s1n, u, vt, x, modv, mod_row_of_tile, tm):
    T, D = x.shape
    ne, _, te = vt.shape
    assert ne % 2 == 0
    half = PEER_HEADS * PEER_KEYS
    tile = lambda s, lag: jnp.clip(s - lag, 0, ne - 1)
    return pl.pallas_call(
        _peer_kernel,
        grid=(T // tm, ne + 2),
        in_specs=[pl.BlockSpec((D, tm), lambda i, s: (0, i)),
                  pl.BlockSpec((te // PEER_KEYS * PEER_HEADS, tm), lambda i, s: (tile(s, 1), i)),
                  pl.BlockSpec((tm // 128, half, 128), lambda i, s: (i, 0, 0)),
                  pl.BlockSpec((PEER_HEADS, tm), lambda i, s: (0, i)),
                  pl.BlockSpec((te, D), lambda i, s: (tile(s, 0), 0)),
                  pl.BlockSpec((None, D, te), lambda i, s: (tile(s, 2), 0, 0)),
                  pl.BlockSpec((tm, D), lambda i, s: (i, 0)),
                  pl.BlockSpec((None, 1, D), lambda i, s: (mod_row_of_tile(i) + 5, 0, 0))],
        out_specs=pl.BlockSpec((tm, D), lambda i, s: (i, 0)),
        out_shape=jax.ShapeDtypeStruct((T, D), F32),
        scratch_shapes=[pltpu.VMEM((te, tm), F32), pltpu.VMEM((te, tm), F32), pltpu.VMEM((te, tm), BF16),
                        pltpu.VMEM((te, tm), BF16), pltpu.VMEM((D, tm), F32)],
        compiler_params=_cparams("parallel", "arbitrary"),
        name="peer_experts",
    )(ht, st, s1n, thr, u, vt, x, modv)


def _rope_tables(n, ident_rows):
    pos = jnp.arange(n, dtype=jnp.int32)
    row, col = pos // GRID_W, pos % GRID_W
    lane = jnp.arange(HEAD_DIM)

    def build(nfreq, width):
        inv = ROPE_THETA ** (-jnp.arange(nfreq, dtype=F32) / nfreq)
        ar = row.astype(F32)[:, None] * inv[None, :]
        ac = col.astype(F32)[:, None] * inv[None, :]
        cos = jnp.concatenate([jnp.cos(ar), jnp.cos(ar), jnp.cos(ac), jnp.cos(ac)], axis=-1)
        sin = jnp.concatenate([jnp.sin(ar), jnp.sin(ar), jnp.sin(ac), jnp.sin(ac)], axis=-1)
        pad = HEAD_DIM - width
        cos = jnp.pad(cos, ((0, 0), (0, pad)))
        sin = jnp.pad(sin, ((0, 0), (0, pad)))
        first = (lane % (2 * nfreq)) < nfreq
        tab = jnp.stack([cos, jnp.where(first, 0.0, sin), jnp.where(first, -sin, 0.0)])
        ident = jnp.stack([jnp.ones((ident_rows, HEAD_DIM), F32), jnp.zeros((ident_rows, HEAD_DIM), F32),
                           jnp.zeros((ident_rows, HEAD_DIM), F32)])
        return jnp.concatenate([ident, tab], axis=1)

    return build(HEAD_DIM // 4, HEAD_DIM), build(MLA_ROPE // 4, MLA_ROPE)


def kernel(x_prompt, x_sample, c, cache_gqa_k, cache_gqa_v, cache_mla_ckv, cache_mla_krope, state_hgrn, c_ctx,
           w_mod, b_mod, norm1_g, norm2_g, w_in, gqa_qn, gqa_kn, mla_qa_n, mla_kva_n, w_mla_qb, w_mla_kvb,
           mla_qn, mla_kn, hg_lb_logits, hg_on, w_out, peer_wq, peer_keys, peer_u, peer_v):
    B, S, D = x_prompt.shape
    DB, DS, _ = x_sample.shape
    L = w_mod.shape[0]
    P = cache_gqa_k.shape[2]
    TC, TL = B * S, DB * DS
    T = TC + TL
    tm = 512 if (TC % 512 == 0 and DS % 512 == 0) else 128
    tm_small = min(256, tm)

    ngroups = 1 + DB

    def mod_row_of(t, layer):
        def f(i):
            group = jnp.where(i < TC // t, 0, 1 + (i - TC // t) // (DS // t))
            return (layer * ngroups + group) * 6
        return f

    def table_of(t):
        def f(i):
            return jnp.where(i < TC // t, 0, 1 + (i - TC // t) % (DS // t))
        return f

    cond8 = jnp.zeros((8, D), F32).at[0].set(c_ctx).at[1:1 + DB].set(c)
    modv = _modulation(cond8, w_mod, b_mod)[:, :ngroups].reshape(L * ngroups * 6, 1, D)

    lb_cum = jnp.cumsum(jax.nn.softmax(hg_lb_logits.astype(F32), axis=0), axis=0)
    lbs = jnp.maximum(lb_cum - lb_cum[:1], 0.0)
    lbtab = jnp.stack([jnp.log(lbs), jnp.log1p(-lbs), 1.0 - lbs] + [jnp.zeros_like(lbs)] * 5, axis=2)

    sizes = np.cumsum([0, 768, 256, 256, 512, 512, 512, 512, 512, 512, 512, 64])
    part = lambda a, k: a[..., sizes[k]:sizes[k + 1]]
    order = (0, 1, 2, 10, 8, 9, 3, 4, 5, 6, 7)
    cols = []
    for k in order:
        cols.append(part(w_in, k))
        if k == 10:
            cols.append(jnp.zeros((L, D, 256 - MLA_ROPE), w_in.dtype))
    w_in_b = jnp.concatenate(cols, axis=-1).astype(BF16)
    w_in_b = jnp.swapaxes(w_in_b.reshape(L, D, -1, IN_TILE), 1, 2)

    wqb = w_mla_qb.reshape(L, MLA_RANK, MLA_HEADS, MLA_QK)
    wqb = jnp.pad(wqb, ((0, 0), (0, 0), (0, 0), (0, MLA_PAD - MLA_QK))).reshape(L, MLA_RANK, MLA_HEADS * MLA_PAD).astype(BF16)
    wkvb = w_mla_kvb.reshape(L, MLA_RANK, MLA_HEADS, 2, HEAD_DIM)
    wkvb = jnp.moveaxis(wkvb, 3, 2).reshape(L, MLA_RANK, 2 * MLA_HEADS * HEAD_DIM).astype(BF16)
    rpad = lambda g: jnp.pad(g[:, HEAD_DIM:], ((0, 0), (0, HEAD_DIM - MLA_ROPE)))[:, None, :]
    mqn_n, mqn_r = mla_qn[:, None, :HEAD_DIM], rpad(mla_qn)
    mkn_n, mkn_r = mla_kn[:, None, :HEAD_DIM], rpad(mla_kn)
    w_out_b = w_out.astype(BF16)
    u_b = peer_u.astype(BF16)
    vt_b = jnp.swapaxes(peer_v.reshape(L, -1, PEER_EXPERT_TILE, D), 2, 3).astype(BF16)
    def key_major(w):
        return jnp.swapaxes(w.reshape(L, 2, PEER_HEADS, PEER_KEYS, D), 2, 3).reshape(L, 2 * PEER_HEADS * PEER_KEYS, D)

    ws_hi, ws_lo = map(key_major, _score_weights(peer_keys, peer_wq))

    tabg, tabm = _rope_tables(DS, tm_small)

    cck = jnp.moveaxis(cache_mla_ckv, 1, 0).reshape(L, DB * P, MLA_RANK)
    ckr = jnp.pad(jnp.moveaxis(cache_mla_krope, 1, 0).reshape(L, DB * P, MLA_ROPE), ((0, 0), (0, 0), (0, HEAD_DIM - MLA_ROPE)))
    cmk, cmv = _cache_expand(cck, ckr, wkvb, mkn_n, mkn_r)
    cgk = jnp.moveaxis(cache_gqa_k, 1, 0).reshape(L, DB * P, -1).astype(BF16)
    cgv = jnp.transpose(cache_gqa_v.reshape(DB, L, P, -1), (1, 3, 0, 2)).reshape(L, -1, DB * P).astype(BF16)
    s0_ctx = jnp.zeros((B, 2, HG_HEADS, HEAD_DIM, HEAD_DIM), F32)

    x = jnp.concatenate([x_prompt.reshape(TC, D), x_sample.reshape(TL, D)], axis=0)
    new_k, new_v, new_ckv, new_kr, new_s = [], [], [], [], []
    tq_c = min(256, S)
    tq_l, tk_l = min(1024, DS), min(512, P, DS)
    for l in range(L):
        z = _in_proj(x, norm1_g[l][None], modv, w_in_b[l], mod_row_of(tm, l), tm)
        gq, gkn, gkr, gvt, mq, ckv, mk, mvt = _prep(
            z, tabg, tabm, table_of(tm_small), tm_small, gqa_qn[l][None], gqa_kn[l][None], mla_qa_n[l][None],
            mla_kva_n[l][None], wqb[l], wkvb[l], mqn_n[l], mqn_r[l], mkn_n[l], mkn_r[l])
        new_k.append(gkn[:TC].reshape(B, S, GQA_KV_HEADS, HEAD_DIM))
        new_v.append(z[:TC, 1024:1280].reshape(B, S, GQA_KV_HEADS, HEAD_DIM))
        new_ckv.append(ckv[:TC].reshape(B, S, MLA_RANK))
        new_kr.append(z[:TC, 1280:1280 + MLA_ROPE].reshape(B, S, MLA_ROPE))

        gqa = dict(heads=GQA_HEADS, group=GQA_HEADS // GQA_KV_HEADS, dq=HEAD_DIM, dv=HEAD_DIM, hb=GQA_HEADS // GQA_KV_HEADS)
        mla = dict(heads=MLA_HEADS, group=1, dq=MLA_PAD, dv=HEAD_DIM, hb=MLA_HEADS // 2)
        ctx_a = dict(batch=B, nq=S, nk=S, row0=0, tq=tq_c, tk=tq_c)
        lat_a = dict(batch=DB, nq=DS, nk=DS, row0=TC, tq=tq_l, tk=tk_l)
        ga = _attention(gq, gkr, gvt, **gqa, **ctx_a)
        ga = _attention(gq, gkr, gvt, **gqa, **lat_a, cache=(cgk[l], cgv[l]), prev=ga)
        ma = _attention(mq, mk, mvt, **mla, **ctx_a)
        ma = _attention(mq, mk, mvt, **mla, **lat_a, cache=(cmk[l], cmv[l]), prev=ma)

        o2, s_c = _hgrn(z, lbtab[l], s0_ctx, batch=B, n=S, row0=0)
        o2, _ = _hgrn(z, lbtab[l], state_hgrn[:, l].astype(F32), batch=DB, n=DS, row0=TC, prev=o2)
        new_s.append(s_c)

        x, hhi, hlo, ht = _mixer_out(x, ga, o2, z, ma, w_out_b[l], hg_on[l][None], norm2_g[l][None], modv,
                                 mod_row_of(tm_small, l), tm_small)
        st = _scores(ws_hi[l], ws_lo[l], hhi, hlo, tm)
        thr, s1n = _topk(st, tm)
        x = _peer(ht, st, thr, s1n, u_b[l], vt_b[l], x, modv, mod_row_of(tm, l), tm)

    return (x[:TC].reshape(B, S, D), x[TC:].reshape(DB, DS, D),
            jnp.stack(new_k, axis=1), jnp.stack(new_v, axis=1), jnp.stack(new_ckv, axis=1),
            jnp.stack(new_kr, axis=1), jnp.stack(new_s, axis=1))
```

```python
import functools

import numpy as np
import jax
import jax.numpy as jnp
from jax import lax
from jax.experimental import pallas as pl
from jax.experimental.pallas import tpu as pltpu

F32 = jnp.float32
BF16 = jnp.bfloat16

EPS = 1e-6
LOG2E = 1.4426950408889634
ROPE_THETA = 10000.0
GRID_W = 64

HEAD_DIM = 128
GQA_HEADS, GQA_KV_HEADS = 6, 2
HG_HEADS = 4
MLA_HEADS, MLA_ROPE, MLA_RANK = 6, 64, 512
MLA_QK = HEAD_DIM + MLA_ROPE
MLA_PAD = 2 * HEAD_DIM
PEER_HEADS, PEER_KEYS, PEER_TOPK = 8, 128, 16
HG_CHUNK = 128
PEER_EXPERT_TILE = 512
IN_TILE = 1024

VMEM_LIMIT = 60 * 1024 * 1024


def _cparams(*sem):
    return pltpu.CompilerParams(dimension_semantics=sem, vmem_limit_bytes=VMEM_LIMIT)


def _dot(a, b):
    return jnp.dot(a, b, preferred_element_type=F32)


def _dot_nt(a, b):
    return lax.dot_general(a, b, (((1,), (1,)), ((), ())), preferred_element_type=F32)


def _dot_tn(a, b):
    return lax.dot_general(a, b, (((0,), (0,)), ((), ())), preferred_element_type=F32)


def _split2(a):
    hi = a.astype(BF16)
    lo = (a - hi.astype(F32)).astype(BF16)
    return hi, lo


def _split3(a):
    p1 = a.astype(BF16)
    r = a - p1.astype(F32)
    p2 = r.astype(BF16)
    p3 = (r - p2.astype(F32)).astype(BF16)
    return p1, p2, p3


def _rms(x, n):
    return x * lax.rsqrt(jnp.sum(x * x, axis=-1, keepdims=True) * (1.0 / n) + EPS)


def _mod_kernel(cond_ref, w_ref, b_ref, o_ref):
    c = cond_ref[...]
    s = c * (1.0 / (1.0 + jnp.exp(-c)))
    shi, slo = _split2(s)
    whi, wlo = _split2(w_ref[...])
    o_ref[...] = _dot(shi, whi) + _dot(slo, whi) + _dot(shi, wlo) + b_ref[...]


def _modulation(cond8, w_mod, b_mod):
    L, D, N = w_mod.shape
    tn = min(512, N)
    return pl.pallas_call(
        _mod_kernel,
        grid=(L, N // tn),
        in_specs=[pl.BlockSpec((8, D), lambda l, j: (0, 0)),
                  pl.BlockSpec((None, D, tn), lambda l, j: (l, 0, j)),
                  pl.BlockSpec((None, 1, tn), lambda l, j: (l, 0, j))],
        out_specs=pl.BlockSpec((None, 8, tn), lambda l, j: (l, 0, j)),
        out_shape=jax.ShapeDtypeStruct((L, 8, N), F32),
        compiler_params=_cparams("parallel", "parallel"),
        name="modulation",
    )(cond8, w_mod, b_mod.reshape(L, 1, N))


def _in_kernel(x_ref, g_ref, sh_ref, sc_ref, w_ref, o_ref, h_sc):
    @pl.when(pl.program_id(1) == 0)
    def _():
        x = x_ref[...]
        h = _rms(x, x.shape[-1]) * g_ref[...] * (1.0 + sc_ref[...]) + sh_ref[...]
        h_sc[...] = h.astype(BF16)

    o_ref[...] = _dot(h_sc[...], w_ref[...])


def _in_proj(x, g, modv, w, mod_row_of_tile, tm):
    T, D = x.shape
    nj, _, tn = w.shape
    N = nj * tn

    def mspec(k):
        return pl.BlockSpec((None, 1, D), lambda i, j: (mod_row_of_tile(i) + k, 0, 0))

    return pl.pallas_call(
        _in_kernel,
        grid=(T // tm, nj),
        in_specs=[pl.BlockSpec((tm, D), lambda i, j: (i, 0)),
                  pl.BlockSpec((1, D), lambda i, j: (0, 0)),
                  mspec(0), mspec(1),
                  pl.BlockSpec((None, D, tn), lambda i, j: (j, 0, 0))],
        out_specs=pl.BlockSpec((tm, tn), lambda i, j: (i, j)),
        out_shape=jax.ShapeDtypeStruct((T, N), F32),
        scratch_shapes=[pltpu.VMEM((tm, D), BF16)],
        compiler_params=_cparams("parallel", "arbitrary"),
        name="in_proj",
    )(x, g, modv, modv, w)


def _rope(x, tab_ref, blk):
    return (x * tab_ref[0] + pltpu.roll(x, blk, 1) * tab_ref[1]
            + pltpu.roll(x, HEAD_DIM - blk, 1) * tab_ref[2])


def _mla_kv(ckv, krope, wkvb_ref, gkn_ref, gkr_ref, tab_ref, mk_ref, mv_ref):
    kv = _dot(ckv.astype(BF16), wkvb_ref[...])
    nv = MLA_HEADS * HEAD_DIM
    mv_ref[...] = kv[:, nv:].T.astype(BF16)
    ssr = jnp.sum(krope * krope, axis=-1, keepdims=True)
    for h in range(MLA_HEADS):
        kn = kv[:, h * HEAD_DIM:(h + 1) * HEAD_DIM]
        rs = lax.rsqrt((jnp.sum(kn * kn, axis=-1, keepdims=True) + ssr) * (1.0 / MLA_QK) + EPS)
        kr = krope * rs * gkr_ref[...]
        if tab_ref is not None:
            kr = _rope(kr, tab_ref, MLA_ROPE // 4)
        mk_ref[:, h * MLA_PAD:h * MLA_PAD + HEAD_DIM] = (kn * rs * gkn_ref[...]).astype(BF16)
        mk_ref[:, h * MLA_PAD + HEAD_DIM:(h + 1) * MLA_PAD] = kr.astype(BF16)


def _prep_kernel(gq_ref, gk_ref, gv_ref, mkr_ref, mqa_ref, mkva_ref, tg_ref, tm_ref,
                 gqn_ref, gkn_ref, qan_ref, kvan_ref, wqb_ref, wkvb_ref,
                 mqn_n_ref, mqn_r_ref, mkn_n_ref, mkn_r_ref, *rest, n_ctx_tiles):
    gq_o, gkr_o, gvt_o, mq_o, mk_o, mv_o, nk_o, nv_o, nckv_o, nkr_o = rest[4:]
    is_ctx = pl.program_id(0) < n_ctx_tiles
    gscale = HEAD_DIM ** -0.5 * LOG2E
    for h in range(GQA_HEADS):
        sl = slice(h * HEAD_DIM, (h + 1) * HEAD_DIM)
        q = _rms(gq_ref[:, sl], HEAD_DIM) * gqn_ref[...]
        gq_o[:, sl] = (_rope(q, tg_ref, HEAD_DIM // 4) * gscale).astype(BF16)
    for h in range(GQA_KV_HEADS):
        sl = slice(h * HEAD_DIM, (h + 1) * HEAD_DIM)
        k = _rms(gk_ref[:, sl], HEAD_DIM) * gkn_ref[...]
        gkr_o[:, sl] = _rope(k, tg_ref, HEAD_DIM // 4).astype(BF16)

        @pl.when(is_ctx)
        def _():
            nk_o[:, sl] = k
    gvt_o[...] = gv_ref[...].T.astype(BF16)

    qa = _rms(mqa_ref[...], MLA_RANK) * qan_ref[...]
    mq = _dot(qa.astype(BF16), wqb_ref[...])
    mscale = MLA_QK ** -0.5 * LOG2E
    for h in range(MLA_HEADS):
        qn = mq[:, h * MLA_PAD:h * MLA_PAD + HEAD_DIM]
        qr = mq[:, h * MLA_PAD + HEAD_DIM:(h + 1) * MLA_PAD]
        ss = jnp.sum(qn * qn, axis=-1, keepdims=True) + jnp.sum(qr * qr, axis=-1, keepdims=True)
        rs = lax.rsqrt(ss * (1.0 / MLA_QK) + EPS) * mscale
        mq_o[:, h * MLA_PAD:h * MLA_PAD + HEAD_DIM] = (qn * rs * mqn_n_ref[...]).astype(BF16)
        mq_o[:, h * MLA_PAD + HEAD_DIM:(h + 1) * MLA_PAD] = _rope(qr * rs * mqn_r_ref[...], tm_ref, MLA_ROPE // 4).astype(BF16)

    ckv = _rms(mkva_ref[...], MLA_RANK) * kvan_ref[...]

    @pl.when(is_ctx)
    def _():
        nv_o[...] = gv_ref[...]
        nckv_o[...] = ckv
        nkr_o[...] = mkr_ref[:, :MLA_ROPE]

    _mla_kv(ckv, mkr_ref[...], wkvb_ref, mkn_n_ref, mkn_r_ref, tm_ref, mk_o, mv_o)


def _prep(z, tabg, tabm, table_of_tile, tm, gqn, gkn, qan, kvan, wqb, wkvb, mqn_n, mqn_r, mkn_n, mkn_r, caches, layer):
    T = z.shape[0]
    B, _, S, _ = caches[0].shape
    per_seq = S // tm
    n_ctx_tiles = B * per_seq
    row = lambda w: pl.BlockSpec((1, w), lambda i: (0, 0))
    full = lambda a: pl.BlockSpec(a.shape, lambda i: (0, 0))
    tab = pl.BlockSpec((3, tm, HEAD_DIM), lambda i: (0, table_of_tile(i), 0))
    zspec = lambda w, blk: pl.BlockSpec((tm, w), lambda i: (i, blk))
    widths = (GQA_HEADS * HEAD_DIM, GQA_KV_HEADS * HEAD_DIM, GQA_KV_HEADS * HEAD_DIM,
              MLA_HEADS * MLA_PAD, MLA_HEADS * MLA_PAD, MLA_HEADS * HEAD_DIM)
    transposed = (False, False, True, False, False, True)
    ospec = lambda w, t: pl.BlockSpec((w, tm), lambda i: (0, i)) if t else pl.BlockSpec((tm, w), lambda i: (i, 0))

    def cspec(c):
        def idx(i):
            t = jnp.minimum(i, n_ctx_tiles - 1)
            return (t // per_seq, layer, t % per_seq, 0)
        return pl.BlockSpec((None, None, tm, c.shape[-1]), idx)

    n_in = 18
    return pl.pallas_call(
        functools.partial(_prep_kernel, n_ctx_tiles=n_ctx_tiles),
        grid=(T // tm,),
        in_specs=[zspec(768, 0), zspec(256, 3), zspec(256, 4), zspec(128, 10), zspec(512, 3), zspec(512, 4),
                  tab, tab, row(128), row(128), row(512), row(512), full(wqb), full(wkvb),
                  row(128), row(128), row(128), row(128)] + [pl.BlockSpec(memory_space=pl.ANY)] * 4,
        out_specs=[ospec(w, t) for w, t in zip(widths, transposed)] + [cspec(c) for c in caches],
        out_shape=[jax.ShapeDtypeStruct((w, T) if t else (T, w), BF16) for w, t in zip(widths, transposed)]
        + [jax.ShapeDtypeStruct(c.shape, c.dtype) for c in caches],
        input_output_aliases={n_in + k: len(widths) + k for k in range(4)},
        compiler_params=_cparams("arbitrary"),
        name="prep",
    )(z, z, z, z, z, z, tabg, tabm, gqn, gkn, qan, kvan, wqb, wkvb, mqn_n, mqn_r, mkn_n, mkn_r, *caches)


def _cache_kernel(ckv_ref, kr_ref, wkvb_ref, gkn_ref, gkr_ref, mk_o, mv_o):
    _mla_kv(ckv_ref[...], kr_ref[...], wkvb_ref, gkn_ref, gkr_ref, None, mk_o, mv_o)


def _cache_expand(ckv, kr, wkvb, mkn_n, mkn_r):
    L, R, _ = ckv.shape
    tm = min(512, R)
    return pl.pallas_call(
        _cache_kernel,
        grid=(L, R // tm),
        in_specs=[pl.BlockSpec((None, tm, MLA_RANK), lambda l, i: (l, i, 0)),
                  pl.BlockSpec((None, tm, HEAD_DIM), lambda l, i: (l, i, 0)),
                  pl.BlockSpec((None,) + wkvb.shape[1:], lambda l, i: (l, 0, 0)),
                  pl.BlockSpec((None, 1, HEAD_DIM), lambda l, i: (l, 0, 0)),
                  pl.BlockSpec((None, 1, HEAD_DIM), lambda l, i: (l, 0, 0))],
        out_specs=[pl.BlockSpec((None, tm, MLA_HEADS * MLA_PAD), lambda l, i: (l, i, 0)),
                   pl.BlockSpec((None, MLA_HEADS * HEAD_DIM, tm), lambda l, i: (l, 0, i))],
        out_shape=[jax.ShapeDtypeStruct((L, R, MLA_HEADS * MLA_PAD), BF16),
                   jax.ShapeDtypeStruct((L, MLA_HEADS * HEAD_DIM, R), BF16)],
        compiler_params=_cparams("parallel", "parallel"),
        name="cache_expand",
    )(ckv, kr, wkvb, mkn_n, mkn_r)


def _attn_kernel(*refs, heads, group, dq, dv, ncb):
    n_in = (5 if ncb else 3) + 1
    q_ref = refs[0]
    o_ref, m_sc, l_sc, acc_sc = refs[n_in:]
    ki = pl.program_id(3)

    @pl.when(ki == 0)
    def _():
        m_sc[...] = jnp.full_like(m_sc, -jnp.inf)
        l_sc[...] = jnp.zeros_like(l_sc)
        acc_sc[...] = jnp.zeros_like(acc_sc)

    def step(k_ref, vt_ref):
        def scores(h):
            kv = h // group
            return _dot_nt(k_ref[:, kv * dq:(kv + 1) * dq], q_ref[:, h * dq:(h + 1) * dq])

        s_next = scores(0)
        for h in range(heads):
            kv = h // group
            s = s_next
            if h + 1 < heads:
                s_next = scores(h + 1)
            m_prev = m_sc[h]
            m_new = jnp.maximum(m_prev, jnp.max(s, axis=0, keepdims=True))
            p = jnp.exp2(s - m_new)
            alpha = jnp.exp2(m_prev - m_new)
            l_sc[h] = alpha * l_sc[h] + jnp.sum(p, axis=0, keepdims=True)
            acc_sc[h] = alpha * acc_sc[h] + _dot(vt_ref[kv * dv:(kv + 1) * dv, :], p.astype(BF16))
            m_sc[h] = m_new

    if ncb:
        pl.when(ki < ncb)(lambda: step(refs[1], refs[2]))
        pl.when(ki >= ncb)(lambda: step(refs[3], refs[4]))
    else:
        step(refs[1], refs[2])

    @pl.when(ki == pl.num_programs(3) - 1)
    def _():
        for h in range(heads):
            o_ref[:, h * dv:(h + 1) * dv] = (acc_sc[h] / l_sc[h]).T.astype(o_ref.dtype)


def _attention(q, k, vt, *, batch, nq, nk, row0, heads, group, dq, dv, tq, tk, hb, prev, cache=None):
    T = q.shape[0]
    nqb, nnb = nq // tq, nk // tk
    ncb = 0 if cache is None else cache[0].shape[0] // batch // tk
    qb0, kb0 = row0 // tq, row0 // tk
    kvb = max(1, hb // group)
    new_blk = lambda b, j: kb0 + b * nnb + jnp.maximum(j - ncb, 0)
    in_specs = [pl.BlockSpec((tq, hb * dq), lambda b, h, i, j: (qb0 + b * nqb + i, h))]
    operands = [q]
    if ncb:
        cache_blk = lambda b, j: b * ncb + jnp.minimum(j, ncb - 1)
        in_specs += [pl.BlockSpec((tk, kvb * dq), lambda b, h, i, j: (cache_blk(b, j), h)),
                     pl.BlockSpec((kvb * dv, tk), lambda b, h, i, j: (h, cache_blk(b, j)))]
        operands += list(cache)
    in_specs += [pl.BlockSpec((tk, kvb * dq), lambda b, h, i, j: (new_blk(b, j), h)),
                 pl.BlockSpec((kvb * dv, tk), lambda b, h, i, j: (h, new_blk(b, j)))]
    operands += [k, vt]
    in_specs.append(pl.BlockSpec(memory_space=pl.ANY))
    aliases = {len(operands): 0}
    operands.append(prev)
    return pl.pallas_call(
        functools.partial(_attn_kernel, heads=hb, group=group, dq=dq, dv=dv, ncb=ncb),
        grid=(batch, heads // hb, nqb, ncb + nnb),
        in_specs=in_specs,
        out_specs=pl.BlockSpec((tq, hb * dv), lambda b, h, i, j: (qb0 + b * nqb + i, h)),
        out_shape=jax.ShapeDtypeStruct((T, heads * dv), BF16),
        scratch_shapes=[pltpu.VMEM((hb, 1, tq), F32), pltpu.VMEM((hb, 1, tq), F32), pltpu.VMEM((hb, dv, tq), F32)],
        input_output_aliases=aliases,
        compiler_params=_cparams("parallel", "parallel", "parallel", "arbitrary"),
        name="attention",
    )(*operands)


def _hgrn_constants(c):
    levels = int(np.log2(c))
    tri = np.zeros((2, c, c), np.float32)
    M = np.zeros((2, levels, c, c), np.float32)
    tri[0] = np.tril(np.ones((c, c), np.float32))
    for lv in range(levels):
        m = c >> (lv + 1)
        for a in range(0, c, 2 * m):
            M[0, lv, a + m:a + 2 * m, a:a + m] = 1.0
    tri[1] = tri[0][::-1, ::-1]
    M[1] = M[0][:, ::-1, ::-1]
    return tri, M


def _level_exponents(bq, bwd, c):
    rowid = lax.broadcasted_iota(jnp.int32, bq.shape, 0)
    out = []
    m = c // 2
    while m >= 1:
        if 2 * m >= 8:
            blocks = [jnp.broadcast_to(jnp.where(bwd, bq[a + m:a + m + 1, :], bq[a + m - 1:a + m, :]), (2 * m, bq.shape[1]))
                      for a in range(0, c, 2 * m)]
            ref = blocks[0] if len(blocks) == 1 else jnp.concatenate(blocks, axis=0)
        else:
            target = m - 1 + bwd.astype(jnp.int32)
            rowmod = rowid % (2 * m)
            ref = bq
            for delta in range(-m, m + 1):
                if delta != 0:
                    ref = jnp.where(rowmod + delta == target, pltpu.roll(bq, (-delta) % c, 0), ref)
        out.append(-jnp.abs(bq - ref))
        m //= 2
    return out


def _hgrn_kernel(q_ref, z_ref, v_ref, lb_ref, s0_ref, L_ref, M_ref, o_ref, sf_ref, s_sc):
    c = q_ref.shape[0]
    levels = M_ref.shape[0]

    @pl.when(pl.program_id(2) == 0)
    def _():
        s_sc[...] = s0_ref[...]

    z = z_ref[...]
    e = jnp.exp(-jnp.abs(z))
    r = 1.0 / (1.0 + e)
    logsig = jnp.minimum(z, 0.0) + jnp.log(r)
    a = lb_ref[0:1, :]
    cc = lb_ref[1:2, :] + logsig
    logf = jnp.maximum(a, cc) + jnp.log(1.0 + jnp.exp(-jnp.abs(a - cc)))
    kk = lb_ref[2:3, :] * jnp.where(z >= 0.0, e * r, r)

    tri = L_ref[...]
    p1, p2, p3 = _split3(logf)
    bq = _dot(tri, p1) + _dot(tri, p2) + _dot(tri, p3)
    tot = jnp.minimum(bq[0:1, :], bq[c - 1:c, :])
    ex_q = jnp.exp(bq)
    ex_k = jnp.exp(tot - bq)
    ex_lv = [jnp.exp(t) for t in _level_exponents(bq, pl.program_id(0) == 1, c)]
    ones = jnp.ones((c, HEAD_DIM), BF16)
    for h in range(HG_HEADS):
        sl = slice(h * HEAD_DIM, (h + 1) * HEAD_DIM)
        q, k, v = q_ref[:, sl], kk[:, sl], v_ref[:, sl]
        vb = v.astype(BF16)
        amat = jnp.zeros((c, c), F32)
        for lv in range(levels):
            el = ex_lv[lv][:, sl]
            amat = amat + M_ref[lv] * _dot_nt((q * el).astype(BF16), (k * el).astype(BF16))
        s_prev = s_sc[h]
        o = (_dot(amat.astype(BF16), vb) + jnp.sum(q * k, axis=-1, keepdims=True) * v
             + _dot((q * ex_q[:, sl]).astype(BF16), s_prev.astype(BF16)))
        o_ref[:, sl] = o
        dcol = jnp.exp(_dot_tn(p1[:, sl], ones) + _dot_tn(p2[:, sl], ones) + _dot_tn(p3[:, sl], ones))
        s_new = dcol * s_prev + _dot_tn((k * ex_k[:, sl]).astype(BF16), vb)
        s_sc[h] = s_new
        sf_ref[h] = s_new


def _hgrn(z, lbtab, s0, *, batch, n, row0, prev, states=None, layer=0):
    c = min(HG_CHUNK, n)
    nc = n // c
    cb0 = row0 // c
    Lm, Mm = _hgrn_constants(c)
    Lm, Mm = jnp.asarray(Lm, BF16), jnp.asarray(Mm, F32)
    w = HG_HEADS * HEAD_DIM

    def rows(d, b, j):
        return cb0 + b * nc + j + d * (nc - 1 - 2 * j)

    sdims = (HG_HEADS, HEAD_DIM, HEAD_DIM)
    extra, aliases = [prev], {7: 0}
    if states is None:
        s_spec = pl.BlockSpec((None, None) + sdims, lambda d, b, j: (b, d, 0, 0, 0))
        s_shape = jax.ShapeDtypeStruct((batch, 2) + sdims, F32)
    else:
        s_spec = pl.BlockSpec((None, None, None) + sdims, lambda d, b, j: (b, layer, d, 0, 0, 0))
        s_shape = jax.ShapeDtypeStruct(states.shape, F32)
        extra, aliases = [prev, states], {7: 0, 8: 1}
    n_in = 7
    return pl.pallas_call(
        lambda *refs: _hgrn_kernel(*refs[:n_in], *refs[n_in + len(extra):]),
        grid=(2, batch, nc),
        in_specs=[pl.BlockSpec((c, w), lambda d, b, j: (rows(d, b, j), 5)),
                  pl.BlockSpec((c, w), lambda d, b, j: (rows(d, b, j), 6 + d)),
                  pl.BlockSpec((c, w), lambda d, b, j: (rows(d, b, j), 8)),
                  pl.BlockSpec((None, 8, w), lambda d, b, j: (d, 0, 0)),
                  pl.BlockSpec((None, None, HG_HEADS, HEAD_DIM, HEAD_DIM), lambda d, b, j: (b, d, 0, 0, 0)),
                  pl.BlockSpec((None,) + Lm.shape[1:], lambda d, b, j: (d, 0, 0)),
                  pl.BlockSpec((None,) + Mm.shape[1:], lambda d, b, j: (d, 0, 0, 0))]
        + [pl.BlockSpec(memory_space=pl.ANY)] * len(extra),
        out_specs=[pl.BlockSpec((None, c, w), lambda d, b, j: (d, rows(d, b, j), 0)), s_spec],
        out_shape=[jax.ShapeDtypeStruct((2, z.shape[0], w), F32), s_shape],
        scratch_shapes=[pltpu.VMEM((HG_HEADS, HEAD_DIM, HEAD_DIM), F32)],
        input_output_aliases=aliases,
        compiler_params=_cparams("parallel", "parallel", "arbitrary"),
        name="hgrn_scan",
    )(z, z, z, lbtab, s0, Lm, Mm, *extra)


def _out_kernel(x_ref, ga_ref, of_ref, ob_ref, hg_ref, ma_ref, w_ref, on_ref, g2_ref,
                g1_ref, sh2_ref, sc2_ref, xo_ref, hhi_ref, hlo_ref, ht_ref):
    o = of_ref[...] + ob_ref[...]
    g = hg_ref[...]
    g = g * (1.0 / (1.0 + jnp.exp(-g)))
    n_g, n_h = GQA_HEADS * HEAD_DIM, HG_HEADS * HEAD_DIM
    mix = _dot(ga_ref[...], w_ref[0:n_g, :]) + _dot(ma_ref[...], w_ref[n_g + n_h:, :])
    for h in range(HG_HEADS):
        sl = slice(h * HEAD_DIM, (h + 1) * HEAD_DIM)
        oh = _rms(o[:, sl], HEAD_DIM) * on_ref[...] * g[:, sl]
        mix = mix + _dot(oh.astype(BF16), w_ref[n_g + h * HEAD_DIM:n_g + (h + 1) * HEAD_DIM, :])
    x = x_ref[...] + g1_ref[...] * mix
    xo_ref[...] = x
    h2 = _rms(x, x.shape[-1]) * g2_ref[...] * (1.0 + sc2_ref[...]) + sh2_ref[...]
    hi, lo = _split2(h2)
    hhi_ref[...] = hi
    hlo_ref[...] = lo
    ht_ref[...] = h2.T.astype(BF16)


def _mixer_out(x, ga, o2, z, ma, w, on, g2, modv, mod_row_of_tile, tm):
    T, D = x.shape
    wh = HG_HEADS * HEAD_DIM

    def mspec(k):
        return pl.BlockSpec((None, 1, D), lambda i: (mod_row_of_tile(i) + k, 0, 0))

    return pl.pallas_call(
        _out_kernel,
        grid=(T // tm,),
        in_specs=[pl.BlockSpec((tm, D), lambda i: (i, 0)),
                  pl.BlockSpec((tm, ga.shape[1]), lambda i: (i, 0)),
                  pl.BlockSpec((None, tm, wh), lambda i: (0, i, 0)),
                  pl.BlockSpec((None, tm, wh), lambda i: (1, i, 0)),
                  pl.BlockSpec((tm, wh), lambda i: (i, 9)),
                  pl.BlockSpec((tm, ma.shape[1]), lambda i: (i, 0)),
                  pl.BlockSpec(w.shape, lambda i: (0, 0)),
                  pl.BlockSpec((1, HEAD_DIM), lambda i: (0, 0)),
                  pl.BlockSpec((1, D), lambda i: (0, 0)),
                  mspec(2), mspec(3), mspec(4)],
        out_specs=[pl.BlockSpec((tm, D), lambda i: (i, 0))] * 3 + [pl.BlockSpec((D, tm), lambda i: (0, i))],
        out_shape=[jax.ShapeDtypeStruct((T, D), F32), jax.ShapeDtypeStruct((T, D), BF16),
                   jax.ShapeDtypeStruct((T, D), BF16), jax.ShapeDtypeStruct((D, T), BF16)],
        compiler_params=_cparams("parallel"),
        name="mixer_out",
    )(x, ga, o2, o2, z, ma, w, on, g2, modv, modv, modv)


def _ws_kernel(k_ref, wq_ref, hi_ref, lo_ref):
    khi, klo = _split2(k_ref[...])
    whi, wlo = _split2(wq_ref[...])
    acc = _dot_nt(khi, whi) + _dot_nt(klo, whi) + _dot_nt(khi, wlo)
    hi, lo = _split2(acc * LOG2E)
    hi_ref[...] = hi
    lo_ref[...] = lo


def _score_weights(keys, wq):
    L, D, _ = wq.shape
    nb = 2 * PEER_HEADS
    sub = keys.shape[-1]
    kk = keys.reshape(L, nb, PEER_KEYS, sub)
    out = jax.ShapeDtypeStruct((L, nb * PEER_KEYS, D), BF16)
    ospec = pl.BlockSpec((None, PEER_KEYS, D), lambda l, b: (l, (b % 2) * PEER_HEADS + b // 2, 0))
    return pl.pallas_call(
        _ws_kernel,
        grid=(L, nb),
        in_specs=[pl.BlockSpec((None, None, PEER_KEYS, sub), lambda l, b: (l, b, 0, 0)),
                  pl.BlockSpec((None, D, sub), lambda l, b: (l, 0, b))],
        out_specs=[ospec, ospec],
        out_shape=[out, out],
        compiler_params=_cparams("parallel", "parallel"),
        name="score_weights",
    )(kk, wq)


def _score_kernel(whi_ref, wlo_ref, hhi_ref, hlo_ref, o_ref):
    whi, hhi = whi_ref[...], hhi_ref[...]
    o_ref[...] = _dot_nt(whi, hhi) + _dot_nt(whi, hlo_ref[...]) + _dot_nt(wlo_ref[...], hhi)


def _scores(whi, wlo, hhi, hlo, tm):
    R, D = whi.shape
    T = hhi.shape[0]
    tr = min(512, R)
    return pl.pallas_call(
        _score_kernel,
        grid=(R // tr, T // tm),
        in_specs=[pl.BlockSpec((tr, D), lambda r, i: (r, 0)), pl.BlockSpec((tr, D), lambda r, i: (r, 0)),
                  pl.BlockSpec((tm, D), lambda r, i: (i, 0)), pl.BlockSpec((tm, D), lambda r, i: (i, 0))],
        out_specs=pl.BlockSpec((tr, tm), lambda r, i: (r, i)),
        out_shape=jax.ShapeDtypeStruct((R, T), F32),
        compiler_params=_cparams("parallel", "parallel"),
        name="scores",
    )(whi, wlo, hhi, hlo)


def _batcher_pairs(n):
    pairs = []
    p = 1
    while p < n:
        k = p
        while k >= 1:
            for j in range(k % p, n - k, 2 * k):
                for i in range(min(k, n - j - k)):
                    if (i + j) // (2 * p) == (i + j + k) // (2 * p):
                        pairs.append((i + j, i + j + k))
            k //= 2
        p *= 2
    return pairs


_SORT16 = _batcher_pairs(PEER_TOPK)


def _sort_desc(v):
    v = list(v)
    for i, j in _SORT16:
        v[i], v[j] = jnp.maximum(v[i], v[j]), jnp.minimum(v[i], v[j])
    return v


def _merge_top(a, b):
    n = len(a)
    v = [jnp.maximum(a[i], b[n - 1 - i]) for i in range(n)]
    s = n // 2
    while s >= 1:
        for i in range(n):
            if (i & s) == 0:
                v[i], v[i + s] = jnp.maximum(v[i], v[i + s]), jnp.minimum(v[i], v[i + s])
        s //= 2
    return v


def _top16_of_keys(s_ref, row0, lanes):
    k = PEER_TOPK
    groups = [_sort_desc([s_ref[pl.ds(row0 + PEER_HEADS * (k * g + i), PEER_HEADS), lanes] for i in range(k)])
              for g in range(PEER_KEYS // k)]
    while len(groups) > 1:
        groups = [_merge_top(groups[i], groups[i + 1]) for i in range(0, len(groups), 2)]
    return groups[0]


def _topk_kernel(s_ref, thr_ref, s1n_ref):
    k = PEER_TOPK
    half = PEER_HEADS * PEER_KEYS
    ninf = jnp.full((PEER_HEADS, 128), -jnp.inf, F32)

    def chunk(ch, carry):
        lanes = pl.ds(pl.multiple_of(ch * 128, 128), 128)
        a = _top16_of_keys(s_ref, 0, lanes)
        b = _top16_of_keys(s_ref, half, lanes)
        cand = [[a[i] + b[j] for j in range(k // (i + 1))] for i in range(k)]
        g0 = cand[0]
        g1 = _sort_desc(cand[1] + cand[2] + cand[5] + cand[8])
        g2 = _sort_desc(cand[3] + cand[4] + cand[6] + cand[7] + cand[9] + cand[10] + cand[11] + cand[12] + cand[13])
        g3 = cand[14] + cand[15] + [ninf] * (k - 2)
        top = _merge_top(_merge_top(g0, g1), _merge_top(g2, g3))
        zsum = jnp.zeros((PEER_HEADS, 128), F32)
        for t in top:
            zsum = zsum + jnp.exp2(t - top[0])
        c0 = top[0] + jnp.log2(zsum)
        thr_ref[:, lanes] = top[k - 1] - c0
        for key in range(PEER_KEYS):
            tile = s_ref[pl.ds(half + PEER_HEADS * key, PEER_HEADS), lanes] - c0
            s1n_ref[ch, pl.ds(key, PEER_HEADS, stride=PEER_KEYS), :] = tile
        return carry

    lax.fori_loop(0, s_ref.shape[1] // 128, chunk, 0)


def _topk(st, tm):
    R, T = st.shape
    return pl.pallas_call(
        _topk_kernel,
        grid=(T // tm,),
        in_specs=[pl.BlockSpec((R, tm), lambda i: (0, i))],
        out_specs=[pl.BlockSpec((PEER_HEADS, tm), lambda i: (0, i)),
                   pl.BlockSpec((tm // 128, R // 2, 128), lambda i: (i, 0, 0))],
        out_shape=[jax.ShapeDtypeStruct((PEER_HEADS, T), F32), jax.ShapeDtypeStruct((T // 128, R // 2, 128), F32)],
        compiler_params=_cparams("parallel"),
        name="topk",
    )(st)


def _peer_kernel(ht_ref, s0_ref, s1n_ref, thr_ref, u_ref, vt_ref, x_ref, g2_ref, o_ref,
                 pre_a, pre_b, c_a, c_b, acc_sc):
    s = pl.program_id(1)
    te, tm = c_a.shape
    nk = PEER_KEYS

    @pl.when(s == 0)
    def _():
        acc_sc[...] = jnp.zeros_like(acc_sc)
        pre_b[...] = jnp.zeros_like(pre_b)
        c_a[...] = jnp.zeros_like(c_a)

    d = ht_ref.shape[0]
    ni = te // nk
    jr = 16
    nblk = (tm // 128) * (nk // jr)
    nh = 2 if tm % 256 == 0 else 1
    th = tm // nh
    ka = mc = max(1, min(nblk // nh, d // 256))
    kq, mq = d // ka, d // mc

    def pre_piece(p, pre_w):
        half, q = divmod(p, ka)
        part = _dot(u_ref[:, q * kq:(q + 1) * kq], ht_ref[q * kq:(q + 1) * kq, half * th:(half + 1) * th])
        if q == 0:
            pre_w[:, half * th:(half + 1) * th] = part
        else:
            pre_w[:, half * th:(half + 1) * th] += part

    def value_piece(p, c_r):
        half, q = divmod(p, mc)
        acc_sc[q * mq:(q + 1) * mq, half * th:(half + 1) * th] += _dot(
            vt_ref[q * mq:(q + 1) * mq, :], c_r[:, half * th:(half + 1) * th])

    slots = {}
    for p in range(nh * ka):
        slots.setdefault(p * nblk // (nh * ka), []).append((pre_piece, p))
    for p in range(nh * mc):
        slots.setdefault((p * nblk // (nh * mc) + 1) % nblk, []).append((value_piece, p))

    def stages(pre_w, pre_r, c_w, c_r):
        for blk in range(nblk):
            for piece, p in slots.get(blk, ()):
                piece(p, pre_w if piece is pre_piece else c_r)
            ch, jq = divmod(blk, nk // jr)
            lanes = slice(ch * 128, (ch + 1) * 128)
            gates = [jnp.zeros((jr, 128), F32) for _ in range(ni)]
            for h in range(PEER_HEADS):
                s1 = s1n_ref[ch, h * nk + jq * jr:h * nk + (jq + 1) * jr, :]
                thr = thr_ref[h:h + 1, lanes]
                for ii in range(ni):
                    r0 = ii * PEER_HEADS + h
                    val = s1 + s0_ref[r0:r0 + 1, lanes]
                    gates[ii] = gates[ii] + jnp.where(val >= thr, jnp.exp2(val), 0.0)
            for ii in range(ni):
                rows = slice(ii * nk + jq * jr, ii * nk + (jq + 1) * jr)
                pre = pre_r[rows, lanes]
                act = 0.5 * pre * (1.0 + lax.erf(pre * (2.0 ** -0.5)))
                c_w[rows, lanes] = (gates[ii] * act).astype(BF16)

    @pl.when(s % 2 == 0)
    def _():
        stages(pre_a, pre_b, c_b, c_a)

    @pl.when(s % 2 == 1)
    def _():
        stages(pre_b, pre_a, c_a, c_b)

    @pl.when(s == pl.num_programs(1) - 1)
    def _():
        o_ref[...] = x_ref[...] + g2_ref[...] * acc_sc[...].T


def _peer(ht, st, thr, s1n, u, vt, x, modv, mod_row_of_tile, tm):
    T, D = x.shape
    ne, _, te = vt.shape
    half = PEER_HEADS * PEER_KEYS
    tile = lambda s, lag: jnp.clip(s - lag, 0, ne - 1)
    return pl.pallas_call(
        _peer_kernel,
        grid=(T // tm, ne + 2),
        in_specs=[pl.BlockSpec((D, tm), lambda i, s: (0, i)),
                  pl.BlockSpec((te // PEER_KEYS * PEER_HEADS, tm), lambda i, s: (tile(s, 1), i)),
                  pl.BlockSpec((tm // 128, half, 128), lambda i, s: (i, 0, 0)),
                  pl.BlockSpec((PEER_HEADS, tm), lambda i, s: (0, i)),
                  pl.BlockSpec((te, D), lambda i, s: (tile(s, 0), 0)),
                  pl.BlockSpec((None, D, te), lambda i, s: (tile(s, 2), 0, 0)),
                  pl.BlockSpec((tm, D), lambda i, s: (i, 0)),
                  pl.BlockSpec((None, 1, D), lambda i, s: (mod_row_of_tile(i) + 5, 0, 0))],
        out_specs=pl.BlockSpec((tm, D), lambda i, s: (i, 0)),
        out_shape=jax.ShapeDtypeStruct((T, D), F32),
        scratch_shapes=[pltpu.VMEM((te, tm), F32), pltpu.VMEM((te, tm), F32), pltpu.VMEM((te, tm), BF16),
                        pltpu.VMEM((te, tm), BF16), pltpu.VMEM((D, tm), F32)],
        compiler_params=_cparams("parallel", "arbitrary"),
        name="peer_experts",
    )(ht, st, s1n, thr, u, vt, x, modv)


def _rope_tables(n, ident_rows):
    pos = jnp.arange(n, dtype=jnp.int32)
    row, col = pos // GRID_W, pos % GRID_W
    lane = jnp.arange(HEAD_DIM)

    def build(nfreq, width):
        inv = ROPE_THETA ** (-jnp.arange(nfreq, dtype=F32) / nfreq)
        ar = row.astype(F32)[:, None] * inv[None, :]
        ac = col.astype(F32)[:, None] * inv[None, :]
        cos = jnp.concatenate([jnp.cos(ar), jnp.cos(ar), jnp.cos(ac), jnp.cos(ac)], axis=-1)
        sin = jnp.concatenate([jnp.sin(ar), jnp.sin(ar), jnp.sin(ac), jnp.sin(ac)], axis=-1)
        pad = HEAD_DIM - width
        cos = jnp.pad(cos, ((0, 0), (0, pad)))
        sin = jnp.pad(sin, ((0, 0), (0, pad)))
        first = (lane % (2 * nfreq)) < nfreq
        tab = jnp.stack([cos, jnp.where(first, 0.0, sin), jnp.where(first, -sin, 0.0)])
        ident = jnp.stack([jnp.ones((ident_rows, HEAD_DIM), F32), jnp.zeros((ident_rows, HEAD_DIM), F32),
                           jnp.zeros((ident_rows, HEAD_DIM), F32)])
        return jnp.concatenate([ident, tab], axis=1)

    return build(HEAD_DIM // 4, HEAD_DIM), build(MLA_ROPE // 4, MLA_ROPE)


def kernel(x_prompt, x_sample, c, cache_gqa_k, cache_gqa_v, cache_mla_ckv, cache_mla_krope, state_hgrn, c_ctx,
           w_mod, b_mod, norm1_g, norm2_g, w_in, gqa_qn, gqa_kn, mla_qa_n, mla_kva_n, w_mla_qb, w_mla_kvb,
           mla_qn, mla_kn, hg_lb_logits, hg_on, w_out, peer_wq, peer_keys, peer_u, peer_v):
    B, S, D = x_prompt.shape
    DB, DS, _ = x_sample.shape
    L = w_mod.shape[0]
    P = cache_gqa_k.shape[2]
    TC, TL = B * S, DB * DS
    T = TC + TL
    tm = 512 if (TC % 512 == 0 and DS % 512 == 0) else 128
    tm_small = min(256, tm)

    ngroups = 1 + DB

    def mod_row_of(t, layer):
        def f(i):
            group = jnp.where(i < TC // t, 0, 1 + (i - TC // t) // (DS // t))
            return (layer * ngroups + group) * 6
        return f

    def table_of(t):
        def f(i):
            return jnp.where(i < TC // t, 0, 1 + (i - TC // t) % (DS // t))
        return f

    cond8 = jnp.zeros((8, D), F32).at[0].set(c_ctx).at[1:1 + DB].set(c)
    modv = _modulation(cond8, w_mod, b_mod)[:, :ngroups].reshape(L * ngroups * 6, 1, D)

    lb_cum = jnp.cumsum(jax.nn.softmax(hg_lb_logits.astype(F32), axis=0), axis=0)
    lbs = jnp.maximum(lb_cum - lb_cum[:1], 0.0)
    lbtab = jnp.stack([jnp.log(lbs), jnp.log1p(-lbs), 1.0 - lbs] + [jnp.zeros_like(lbs)] * 5, axis=2)

    sizes = np.cumsum([0, 768, 256, 256, 512, 512, 512, 512, 512, 512, 512, 64])
    part = lambda a, k: a[..., sizes[k]:sizes[k + 1]]
    order = (0, 1, 2, 10, 8, 9, 3, 4, 5, 6, 7)
    cols = []
    for k in order:
        cols.append(part(w_in, k))
        if k == 10:
            cols.append(jnp.zeros((L, D, 256 - MLA_ROPE), w_in.dtype))
    w_in_b = jnp.concatenate(cols, axis=-1).astype(BF16)
    w_in_b = jnp.swapaxes(w_in_b.reshape(L, D, -1, IN_TILE), 1, 2)

    wqb = w_mla_qb.reshape(L, MLA_RANK, MLA_HEADS, MLA_QK)
    wqb = jnp.pad(wqb, ((0, 0), (0, 0), (0, 0), (0, MLA_PAD - MLA_QK))).reshape(L, MLA_RANK, MLA_HEADS * MLA_PAD).astype(BF16)
    wkvb = w_mla_kvb.reshape(L, MLA_RANK, MLA_HEADS, 2, HEAD_DIM)
    wkvb = jnp.moveaxis(wkvb, 3, 2).reshape(L, MLA_RANK, 2 * MLA_HEADS * HEAD_DIM).astype(BF16)
    rpad = lambda g: jnp.pad(g[:, HEAD_DIM:], ((0, 0), (0, HEAD_DIM - MLA_ROPE)))[:, None, :]
    mqn_n, mqn_r = mla_qn[:, None, :HEAD_DIM], rpad(mla_qn)
    mkn_n, mkn_r = mla_kn[:, None, :HEAD_DIM], rpad(mla_kn)
    w_out_b = w_out.astype(BF16)
    u_b = peer_u.astype(BF16)
    vt_b = jnp.swapaxes(peer_v.reshape(L, -1, PEER_EXPERT_TILE, D), 2, 3).astype(BF16)
    def key_major(w):
        return jnp.swapaxes(w.reshape(L, 2, PEER_HEADS, PEER_KEYS, D), 2, 3).reshape(L, 2 * PEER_HEADS * PEER_KEYS, D)

    ws_hi, ws_lo = map(key_major, _score_weights(peer_keys, peer_wq))

    tabg, tabm = _rope_tables(DS, tm_small)

    cck = jnp.moveaxis(cache_mla_ckv, 1, 0).reshape(L, DB * P, MLA_RANK)
    ckr = jnp.pad(jnp.moveaxis(cache_mla_krope, 1, 0).reshape(L, DB * P, MLA_ROPE), ((0, 0), (0, 0), (0, HEAD_DIM - MLA_ROPE)))
    cmk, cmv = _cache_expand(cck, ckr, wkvb, mkn_n, mkn_r)
    cgk = jnp.moveaxis(cache_gqa_k, 1, 0).reshape(L, DB * P, -1).astype(BF16)
    cgv = jnp.transpose(cache_gqa_v.reshape(DB, L, P, -1), (1, 3, 0, 2)).reshape(L, -1, DB * P).astype(BF16)
    s0_ctx = jnp.zeros((B, 2, HG_HEADS, HEAD_DIM, HEAD_DIM), F32)

    x = jnp.concatenate([x_prompt.reshape(TC, D), x_sample.reshape(TL, D)], axis=0)
    kvw = GQA_KV_HEADS * HEAD_DIM
    caches = tuple(jnp.zeros((B, L, S, w), F32) for w in (kvw, kvw, MLA_RANK, MLA_ROPE))
    states = jnp.zeros((B, L, 2, HG_HEADS, HEAD_DIM, HEAD_DIM), F32)
    ga = jnp.zeros((T, GQA_HEADS * HEAD_DIM), BF16)
    ma = jnp.zeros((T, MLA_HEADS * HEAD_DIM), BF16)
    o2 = jnp.zeros((2, T, HG_HEADS * HEAD_DIM), F32)
    tq_c = min(256, S)
    tq_l, tk_l = min(1024, DS), min(512, P, DS)
    for l in range(L):
        z = _in_proj(x, norm1_g[l][None], modv, w_in_b[l], mod_row_of(tm, l), tm)
        gq, gkr, gvt, mq, mk, mvt, *caches = _prep(
            z, tabg, tabm, table_of(tm_small), tm_small, gqa_qn[l][None], gqa_kn[l][None], mla_qa_n[l][None],
            mla_kva_n[l][None], wqb[l], wkvb[l], mqn_n[l], mqn_r[l], mkn_n[l], mkn_r[l], caches, l)

        gqa = dict(heads=GQA_HEADS, group=GQA_HEADS // GQA_KV_HEADS, dq=HEAD_DIM, dv=HEAD_DIM, hb=GQA_HEADS // GQA_KV_HEADS)
        mla = dict(heads=MLA_HEADS, group=1, dq=MLA_PAD, dv=HEAD_DIM, hb=MLA_HEADS // 2)
        ctx_a = dict(batch=B, nq=S, nk=S, row0=0, tq=tq_c, tk=tq_c)
        lat_a = dict(batch=DB, nq=DS, nk=DS, row0=TC, tq=tq_l, tk=tk_l)
        ga = _attention(gq, gkr, gvt, **gqa, **ctx_a, prev=ga)
        ga = _attention(gq, gkr, gvt, **gqa, **lat_a, prev=ga, cache=(cgk[l], cgv[l]))
        ma = _attention(mq, mk, mvt, **mla, **ctx_a, prev=ma)
        ma = _attention(mq, mk, mvt, **mla, **lat_a, prev=ma, cache=(cmk[l], cmv[l]))

        o2, states = _hgrn(z, lbtab[l], s0_ctx, batch=B, n=S, row0=0, prev=o2, states=states, layer=l)
        o2, _ = _hgrn(z, lbtab[l], state_hgrn[:, l].astype(F32), batch=DB, n=DS, row0=TC, prev=o2)

        x, hhi, hlo, ht = _mixer_out(x, ga, o2, z, ma, w_out_b[l], hg_on[l][None], norm2_g[l][None], modv,
                                 mod_row_of(tm_small, l), tm_small)
        st = _scores(ws_hi[l], ws_lo[l], hhi, hlo, tm)
        thr, s1n = _topk(st, tm)
        x = _peer(ht, st, thr, s1n, u_b[l], vt_b[l], x, modv, mod_row_of(tm, l), tm)

    new_k, new_v, new_ckv, new_kr = caches
    return (x[:TC].reshape(B, S, D), x[TC:].reshape(DB, DS, D),
            new_k.reshape(B, L, S, GQA_KV_HEADS, HEAD_DIM), new_v.reshape(B, L, S, GQA_KV_HEADS, HEAD_DIM),
            new_ckv, new_kr, states)
```

```python
import functools

import numpy as np
import jax
import jax.numpy as jnp
from jax import lax
from jax.experimental import pallas as pl
from jax.experimental.pallas import tpu as pltpu

F32 = jnp.float32
BF16 = jnp.bfloat16

EPS = 1e-6
LOG2E = 1.4426950408889634
ROPE_THETA = 10000.0
GRID_W = 64

HEAD_DIM = 128
GQA_HEADS, GQA_KV_HEADS = 6, 2
HG_HEADS = 4
MLA_HEADS, MLA_ROPE, MLA_RANK = 6, 64, 512
MLA_QK = HEAD_DIM + MLA_ROPE
MLA_PAD = 2 * HEAD_DIM
PEER_HEADS, PEER_KEYS, PEER_TOPK = 8, 128, 16
HG_CHUNK = 128
PEER_EXPERT_TILE = 512
IN_TILE = 1024

VMEM_LIMIT = 60 * 1024 * 1024


def _cparams(*sem):
    return pltpu.CompilerParams(dimension_semantics=sem, vmem_limit_bytes=VMEM_LIMIT)


def _dot(a, b):
    return jnp.dot(a, b, preferred_element_type=F32)


def _dot_nt(a, b):
    return lax.dot_general(a, b, (((1,), (1,)), ((), ())), preferred_element_type=F32)


def _dot_tn(a, b):
    return lax.dot_general(a, b, (((0,), (0,)), ((), ())), preferred_element_type=F32)


def _split2(a):
    hi = a.astype(BF16)
    lo = (a - hi.astype(F32)).astype(BF16)
    return hi, lo


def _split3(a):
    p1 = a.astype(BF16)
    r = a - p1.astype(F32)
    p2 = r.astype(BF16)
    p3 = (r - p2.astype(F32)).astype(BF16)
    return p1, p2, p3


def _rms(x, n):
    return x * lax.rsqrt(jnp.sum(x * x, axis=-1, keepdims=True) * (1.0 / n) + EPS)


def _mod_kernel(cond_ref, w_ref, b_ref, o_ref):
    c = cond_ref[...]
    s = c * (1.0 / (1.0 + jnp.exp(-c)))
    shi, slo = _split2(s)
    whi, wlo = _split2(w_ref[...])
    o_ref[...] = _dot(shi, whi) + _dot(slo, whi) + _dot(shi, wlo) + b_ref[...]


def _modulation(cond8, w_mod, b_mod):
    L, D, N = w_mod.shape
    tn = min(512, N)
    return pl.pallas_call(
        _mod_kernel,
        grid=(L, N // tn),
        in_specs=[pl.BlockSpec((8, D), lambda l, j: (0, 0)),
                  pl.BlockSpec((None, D, tn), lambda l, j: (l, 0, j)),
                  pl.BlockSpec((None, 1, tn), lambda l, j: (l, 0, j))],
        out_specs=pl.BlockSpec((None, 8, tn), lambda l, j: (l, 0, j)),
        out_shape=jax.ShapeDtypeStruct((L, 8, N), F32),
        compiler_params=_cparams("parallel", "parallel"),
        name="modulation",
    )(cond8, w_mod, b_mod.reshape(L, 1, N))


def _in_kernel(x_ref, g_ref, sh_ref, sc_ref, w_ref, o_ref, h_sc):
    @pl.when(pl.program_id(1) == 0)
    def _():
        x = x_ref[...]
        h = _rms(x, x.shape[-1]) * g_ref[...] * (1.0 + sc_ref[...]) + sh_ref[...]
        h_sc[...] = h.astype(BF16)

    o_ref[...] = _dot(h_sc[...], w_ref[...])


def _in_proj(x, g, modv, w, mod_row_of_tile, tm, layer):
    T, D = x.shape
    _, nj, _, tn = w.shape
    N = nj * tn

    def mspec(k):
        return pl.BlockSpec((None, 1, D), lambda i, j: (mod_row_of_tile(i) + k, 0, 0))

    return pl.pallas_call(
        _in_kernel,
        grid=(T // tm, nj),
        in_specs=[pl.BlockSpec((tm, D), lambda i, j: (i, 0)),
                  pl.BlockSpec((1, D), lambda i, j: (0, 0)),
                  mspec(0), mspec(1),
                  pl.BlockSpec((None, None, D, tn), lambda i, j: (layer, j, 0, 0))],
        out_specs=pl.BlockSpec((tm, tn), lambda i, j: (i, j)),
        out_shape=jax.ShapeDtypeStruct((T, N), F32),
        scratch_shapes=[pltpu.VMEM((tm, D), BF16)],
        compiler_params=_cparams("parallel", "arbitrary"),
        name="in_proj",
    )(x, g, modv, modv, w)


def _rope(x, tab_ref, blk):
    return (x * tab_ref[0] + pltpu.roll(x, blk, 1) * tab_ref[1]
            + pltpu.roll(x, HEAD_DIM - blk, 1) * tab_ref[2])


def _mla_kv(ckv, krope, wkvb_ref, gkn_ref, gkr_ref, tab_ref, mk_ref, mv_ref):
    kv = _dot(ckv.astype(BF16), wkvb_ref[...])
    nv = MLA_HEADS * HEAD_DIM
    mv_ref[...] = kv[:, nv:].T.astype(BF16)
    ssr = jnp.sum(krope * krope, axis=-1, keepdims=True)
    for h in range(MLA_HEADS):
        kn = kv[:, h * HEAD_DIM:(h + 1) * HEAD_DIM]
        rs = lax.rsqrt((jnp.sum(kn * kn, axis=-1, keepdims=True) + ssr) * (1.0 / MLA_QK) + EPS)
        kr = krope * rs * gkr_ref[...]
        if tab_ref is not None:
            kr = _rope(kr, tab_ref, MLA_ROPE // 4)
        mk_ref[:, h * MLA_PAD:h * MLA_PAD + HEAD_DIM] = (kn * rs * gkn_ref[...]).astype(BF16)
        mk_ref[:, h * MLA_PAD + HEAD_DIM:(h + 1) * MLA_PAD] = kr.astype(BF16)


def _prep_kernel(gq_ref, gk_ref, gv_ref, mkr_ref, mqa_ref, mkva_ref, tg_ref, tm_ref,
                 gqn_ref, gkn_ref, qan_ref, kvan_ref, wqb_ref, wkvb_ref,
                 mqn_n_ref, mqn_r_ref, mkn_n_ref, mkn_r_ref, *rest, n_ctx_tiles):
    gq_o, gkr_o, gvt_o, mq_o, mk_o, mv_o, nk_o, nv_o, nckv_o, nkr_o = rest[4:]
    is_ctx = pl.program_id(0) < n_ctx_tiles
    gscale = HEAD_DIM ** -0.5 * LOG2E
    for h in range(GQA_HEADS):
        sl = slice(h * HEAD_DIM, (h + 1) * HEAD_DIM)
        q = _rms(gq_ref[:, sl], HEAD_DIM) * gqn_ref[...]
        gq_o[:, sl] = (_rope(q, tg_ref, HEAD_DIM // 4) * gscale).astype(BF16)
    for h in range(GQA_KV_HEADS):
        sl = slice(h * HEAD_DIM, (h + 1) * HEAD_DIM)
        k = _rms(gk_ref[:, sl], HEAD_DIM) * gkn_ref[...]
        gkr_o[:, sl] = _rope(k, tg_ref, HEAD_DIM // 4).astype(BF16)

        @pl.when(is_ctx)
        def _():
            nk_o[:, sl] = k
    gvt_o[...] = gv_ref[...].T.astype(BF16)

    qa = _rms(mqa_ref[...], MLA_RANK) * qan_ref[...]
    mq = _dot(qa.astype(BF16), wqb_ref[...])
    mscale = MLA_QK ** -0.5 * LOG2E
    for h in range(MLA_HEADS):
        qn = mq[:, h * MLA_PAD:h * MLA_PAD + HEAD_DIM]
        qr = mq[:, h * MLA_PAD + HEAD_DIM:(h + 1) * MLA_PAD]
        ss = jnp.sum(qn * qn, axis=-1, keepdims=True) + jnp.sum(qr * qr, axis=-1, keepdims=True)
        rs = lax.rsqrt(ss * (1.0 / MLA_QK) + EPS) * mscale
        mq_o[:, h * MLA_PAD:h * MLA_PAD + HEAD_DIM] = (qn * rs * mqn_n_ref[...]).astype(BF16)
        mq_o[:, h * MLA_PAD + HEAD_DIM:(h + 1) * MLA_PAD] = _rope(qr * rs * mqn_r_ref[...], tm_ref, MLA_ROPE // 4).astype(BF16)

    ckv = _rms(mkva_ref[...], MLA_RANK) * kvan_ref[...]

    @pl.when(is_ctx)
    def _():
        nv_o[...] = gv_ref[...]
        nckv_o[...] = ckv
        nkr_o[...] = mkr_ref[:, :MLA_ROPE]

    _mla_kv(ckv, mkr_ref[...], wkvb_ref, mkn_n_ref, mkn_r_ref, tm_ref, mk_o, mv_o)


def _prep(z, tabg, tabm, table_of_tile, tm, gqn, gkn, qan, kvan, wqb, wkvb, mqn_n, mqn_r, mkn_n, mkn_r, caches, layer):
    T = z.shape[0]
    B, _, S, _ = caches[0].shape
    per_seq = S // tm
    n_ctx_tiles = B * per_seq
    row = lambda w: pl.BlockSpec((1, w), lambda i: (0, 0))
    full = lambda a: pl.BlockSpec(a.shape, lambda i: (0, 0))
    tab = pl.BlockSpec((3, tm, HEAD_DIM), lambda i: (0, table_of_tile(i), 0))
    zspec = lambda w, blk: pl.BlockSpec((tm, w), lambda i: (i, blk))
    widths = (GQA_HEADS * HEAD_DIM, GQA_KV_HEADS * HEAD_DIM, GQA_KV_HEADS * HEAD_DIM,
              MLA_HEADS * MLA_PAD, MLA_HEADS * MLA_PAD, MLA_HEADS * HEAD_DIM)
    transposed = (False, False, True, False, False, True)
    ospec = lambda w, t: pl.BlockSpec((w, tm), lambda i: (0, i)) if t else pl.BlockSpec((tm, w), lambda i: (i, 0))

    def cspec(c):
        def idx(i):
            t = jnp.minimum(i, n_ctx_tiles - 1)
            return (t // per_seq, layer, t % per_seq, 0)
        return pl.BlockSpec((None, None, tm, c.shape[-1]), idx)

    n_in = 18
    return pl.pallas_call(
        functools.partial(_prep_kernel, n_ctx_tiles=n_ctx_tiles),
        grid=(T // tm,),
        in_specs=[zspec(768, 0), zspec(256, 3), zspec(256, 4), zspec(128, 10), zspec(512, 3), zspec(512, 4),
                  tab, tab, row(128), row(128), row(512), row(512), full(wqb), full(wkvb),
                  row(128), row(128), row(128), row(128)] + [pl.BlockSpec(memory_space=pl.ANY)] * 4,
        out_specs=[ospec(w, t) for w, t in zip(widths, transposed)] + [cspec(c) for c in caches],
        out_shape=[jax.ShapeDtypeStruct((w, T) if t else (T, w), BF16) for w, t in zip(widths, transposed)]
        + [jax.ShapeDtypeStruct(c.shape, c.dtype) for c in caches],
        input_output_aliases={n_in + k: len(widths) + k for k in range(4)},
        compiler_params=_cparams("arbitrary"),
        name="prep",
    )(z, z, z, z, z, z, tabg, tabm, gqn, gkn, qan, kvan, wqb, wkvb, mqn_n, mqn_r, mkn_n, mkn_r, *caches)


def _cache_kernel(ckv_ref, kr_ref, wkvb_ref, gkn_ref, gkr_ref, mk_o, mv_o):
    _mla_kv(ckv_ref[...], kr_ref[...], wkvb_ref, gkn_ref, gkr_ref, None, mk_o, mv_o)


def _cache_expand(ckv, kr, wkvb, mkn_n, mkn_r):
    L, R, _ = ckv.shape
    tm = min(512, R)
    return pl.pallas_call(
        _cache_kernel,
        grid=(L, R // tm),
        in_specs=[pl.BlockSpec((None, tm, MLA_RANK), lambda l, i: (l, i, 0)),
                  pl.BlockSpec((None, tm, HEAD_DIM), lambda l, i: (l, i, 0)),
                  pl.BlockSpec((None,) + wkvb.shape[1:], lambda l, i: (l, 0, 0)),
                  pl.BlockSpec((None, 1, HEAD_DIM), lambda l, i: (l, 0, 0)),
                  pl.BlockSpec((None, 1, HEAD_DIM), lambda l, i: (l, 0, 0))],
        out_specs=[pl.BlockSpec((None, tm, MLA_HEADS * MLA_PAD), lambda l, i: (l, i, 0)),
                   pl.BlockSpec((None, MLA_HEADS * HEAD_DIM, tm), lambda l, i: (l, 0, i))],
        out_shape=[jax.ShapeDtypeStruct((L, R, MLA_HEADS * MLA_PAD), BF16),
                   jax.ShapeDtypeStruct((L, MLA_HEADS * HEAD_DIM, R), BF16)],
        compiler_params=_cparams("parallel", "parallel"),
        name="cache_expand",
    )(ckv, kr, wkvb, mkn_n, mkn_r)


def _attn_kernel(*refs, heads, group, dq, dv, ncb):
    n_in = (5 if ncb else 3) + 1
    q_ref = refs[0]
    o_ref, m_sc, l_sc, acc_sc = refs[n_in:]
    ki = pl.program_id(3)

    @pl.when(ki == 0)
    def _():
        m_sc[...] = jnp.full_like(m_sc, -jnp.inf)
        l_sc[...] = jnp.zeros_like(l_sc)
        acc_sc[...] = jnp.zeros_like(acc_sc)

    def step(k_ref, vt_ref):
        def scores(h):
            kv = h // group
            return _dot_nt(k_ref[:, kv * dq:(kv + 1) * dq], q_ref[:, h * dq:(h + 1) * dq])

        s_next = scores(0)
        for h in range(heads):
            kv = h // group
            s = s_next
            if h + 1 < heads:
                s_next = scores(h + 1)
            m_prev = m_sc[h]
            m_new = jnp.maximum(m_prev, jnp.max(s, axis=0, keepdims=True))
            p = jnp.exp2(s - m_new)
            alpha = jnp.exp2(m_prev - m_new)
            l_sc[h] = alpha * l_sc[h] + jnp.sum(p, axis=0, keepdims=True)
            acc_sc[h] = alpha * acc_sc[h] + _dot(vt_ref[kv * dv:(kv + 1) * dv, :], p.astype(BF16))
            m_sc[h] = m_new

    if ncb:
        pl.when(ki < ncb)(lambda: step(refs[1], refs[2]))
        pl.when(ki >= ncb)(lambda: step(refs[3], refs[4]))
    else:
        step(refs[1], refs[2])

    @pl.when(ki == pl.num_programs(3) - 1)
    def _():
        for h in range(heads):
            o_ref[:, h * dv:(h + 1) * dv] = (acc_sc[h] / l_sc[h]).T.astype(o_ref.dtype)


def _attention(q, k, vt, *, batch, nq, nk, row0, heads, group, dq, dv, tq, tk, hb, prev, cache=None):
    T = q.shape[0]
    nqb, nnb = nq // tq, nk // tk
    ncb = 0 if cache is None else cache[0].shape[0] // batch // tk
    qb0, kb0 = row0 // tq, row0 // tk
    kvb = max(1, hb // group)
    new_blk = lambda b, j: kb0 + b * nnb + jnp.maximum(j - ncb, 0)
    in_specs = [pl.BlockSpec((tq, hb * dq), lambda b, h, i, j: (qb0 + b * nqb + i, h))]
    operands = [q]
    if ncb:
        cache_blk = lambda b, j: b * ncb + jnp.minimum(j, ncb - 1)
        in_specs += [pl.BlockSpec((tk, kvb * dq), lambda b, h, i, j: (cache_blk(b, j), h)),
                     pl.BlockSpec((kvb * dv, tk), lambda b, h, i, j: (h, cache_blk(b, j)))]
        operands += list(cache)
    in_specs += [pl.BlockSpec((tk, kvb * dq), lambda b, h, i, j: (new_blk(b, j), h)),
                 pl.BlockSpec((kvb * dv, tk), lambda b, h, i, j: (h, new_blk(b, j)))]
    operands += [k, vt]
    in_specs.append(pl.BlockSpec(memory_space=pl.ANY))
    aliases = {len(operands): 0}
    operands.append(prev)
    return pl.pallas_call(
        functools.partial(_attn_kernel, heads=hb, group=group, dq=dq, dv=dv, ncb=ncb),
        grid=(batch, heads // hb, nqb, ncb + nnb),
        in_specs=in_specs,
        out_specs=pl.BlockSpec((tq, hb * dv), lambda b, h, i, j: (qb0 + b * nqb + i, h)),
        out_shape=jax.ShapeDtypeStruct((T, heads * dv), BF16),
        scratch_shapes=[pltpu.VMEM((hb, 1, tq), F32), pltpu.VMEM((hb, 1, tq), F32), pltpu.VMEM((hb, dv, tq), F32)],
        input_output_aliases=aliases,
        compiler_params=_cparams("parallel", "parallel", "parallel", "arbitrary"),
        name="attention",
    )(*operands)


def _hgrn_constants(c):
    levels = int(np.log2(c))
    tri = np.zeros((2, c, c), np.float32)
    M = np.zeros((2, levels, c, c), np.float32)
    tri[0] = np.tril(np.ones((c, c), np.float32))
    for lv in range(levels):
        m = c >> (lv + 1)
        for a in range(0, c, 2 * m):
            M[0, lv, a + m:a + 2 * m, a:a + m] = 1.0
    tri[1] = tri[0][::-1, ::-1]
    M[1] = M[0][:, ::-1, ::-1]
    return tri, M


def _level_exponents(bq, bwd, c):
    rowid = lax.broadcasted_iota(jnp.int32, bq.shape, 0)
    out = []
    m = c // 2
    while m >= 1:
        if 2 * m >= 8:
            blocks = [jnp.broadcast_to(jnp.where(bwd, bq[a + m:a + m + 1, :], bq[a + m - 1:a + m, :]), (2 * m, bq.shape[1]))
                      for a in range(0, c, 2 * m)]
            ref = blocks[0] if len(blocks) == 1 else jnp.concatenate(blocks, axis=0)
        else:
            target = m - 1 + bwd.astype(jnp.int32)
            rowmod = rowid % (2 * m)
            ref = bq
            for delta in range(-m, m + 1):
                if delta != 0:
                    ref = jnp.where(rowmod + delta == target, pltpu.roll(bq, (-delta) % c, 0), ref)
        out.append(-jnp.abs(bq - ref))
        m //= 2
    return out


def _hgrn_kernel(q_ref, z_ref, v_ref, lb_ref, s0_ref, L_ref, M_ref, o_ref, sf_ref, s_sc):
    c = q_ref.shape[0]
    levels = M_ref.shape[0]

    @pl.when(pl.program_id(2) == 0)
    def _():
        s_sc[...] = s0_ref[...]

    z = z_ref[...]
    e = jnp.exp(-jnp.abs(z))
    r = 1.0 / (1.0 + e)
    logsig = jnp.minimum(z, 0.0) + jnp.log(r)
    a = lb_ref[0:1, :]
    cc = lb_ref[1:2, :] + logsig
    logf = jnp.maximum(a, cc) + jnp.log(1.0 + jnp.exp(-jnp.abs(a - cc)))
    kk = lb_ref[2:3, :] * jnp.where(z >= 0.0, e * r, r)

    tri = L_ref[...]
    p1, p2, p3 = _split3(logf)
    bq = _dot(tri, p1) + _dot(tri, p2) + _dot(tri, p3)
    tot = jnp.minimum(bq[0:1, :], bq[c - 1:c, :])
    ex_q = jnp.exp(bq)
    ex_k = jnp.exp(tot - bq)
    ex_lv = [jnp.exp(t) for t in _level_exponents(bq, pl.program_id(0) == 1, c)]
    ones = jnp.ones((c, HEAD_DIM), BF16)
    for h in range(HG_HEADS):
        sl = slice(h * HEAD_DIM, (h + 1) * HEAD_DIM)
        q, k, v = q_ref[:, sl], kk[:, sl], v_ref[:, sl]
        vb = v.astype(BF16)
        amat = jnp.zeros((c, c), F32)
        for lv in range(levels):
            el = ex_lv[lv][:, sl]
            amat = amat + M_ref[lv] * _dot_nt((q * el).astype(BF16), (k * el).astype(BF16))
        s_prev = s_sc[h]
        o = (_dot(amat.astype(BF16), vb) + jnp.sum(q * k, axis=-1, keepdims=True) * v
             + _dot((q * ex_q[:, sl]).astype(BF16), s_prev.astype(BF16)))
        o_ref[:, sl] = o
        dcol = jnp.exp(_dot_tn(p1[:, sl], ones) + _dot_tn(p2[:, sl], ones) + _dot_tn(p3[:, sl], ones))
        s_new = dcol * s_prev + _dot_tn((k * ex_k[:, sl]).astype(BF16), vb)
        s_sc[h] = s_new
        sf_ref[h] = s_new


def _hgrn(z, lbtab, s0, *, batch, n, row0, prev, states=None, layer=0):
    c = min(HG_CHUNK, n)
    nc = n // c
    cb0 = row0 // c
    Lm, Mm = _hgrn_constants(c)
    Lm, Mm = jnp.asarray(Lm, BF16), jnp.asarray(Mm, F32)
    w = HG_HEADS * HEAD_DIM

    def rows(d, b, j):
        return cb0 + b * nc + j + d * (nc - 1 - 2 * j)

    sdims = (HG_HEADS, HEAD_DIM, HEAD_DIM)
    extra, aliases = [prev], {7: 0}
    if states is None:
        s_spec = pl.BlockSpec((None, None) + sdims, lambda d, b, j: (b, d, 0, 0, 0))
        s_shape = jax.ShapeDtypeStruct((batch, 2) + sdims, F32)
    else:
        s_spec = pl.BlockSpec((None, None, None) + sdims, lambda d, b, j: (b, layer, d, 0, 0, 0))
        s_shape = jax.ShapeDtypeStruct(states.shape, F32)
        extra, aliases = [prev, states], {7: 0, 8: 1}
    n_in = 7
    return pl.pallas_call(
        lambda *refs: _hgrn_kernel(*refs[:n_in], *refs[n_in + len(extra):]),
        grid=(2, batch, nc),
        in_specs=[pl.BlockSpec((c, w), lambda d, b, j: (rows(d, b, j), 5)),
                  pl.BlockSpec((c, w), lambda d, b, j: (rows(d, b, j), 6 + d)),
                  pl.BlockSpec((c, w), lambda d, b, j: (rows(d, b, j), 8)),
                  pl.BlockSpec((None, 8, w), lambda d, b, j: (d, 0, 0)),
                  pl.BlockSpec((None, None, HG_HEADS, HEAD_DIM, HEAD_DIM), lambda d, b, j: (b, d, 0, 0, 0)),
                  pl.BlockSpec((None,) + Lm.shape[1:], lambda d, b, j: (d, 0, 0)),
                  pl.BlockSpec((None,) + Mm.shape[1:], lambda d, b, j: (d, 0, 0, 0))]
        + [pl.BlockSpec(memory_space=pl.ANY)] * len(extra),
        out_specs=[pl.BlockSpec((None, c, w), lambda d, b, j: (d, rows(d, b, j), 0)), s_spec],
        out_shape=[jax.ShapeDtypeStruct((2, z.shape[0], w), F32), s_shape],
        scratch_shapes=[pltpu.VMEM((HG_HEADS, HEAD_DIM, HEAD_DIM), F32)],
        input_output_aliases=aliases,
        compiler_params=_cparams("parallel", "parallel", "arbitrary"),
        name="hgrn_scan",
    )(z, z, z, lbtab, s0, Lm, Mm, *extra)


def _out_kernel(x_ref, ga_ref, of_ref, ob_ref, hg_ref, ma_ref, w_ref, on_ref, g2_ref,
                g1_ref, sh2_ref, sc2_ref, xo_ref, hhi_ref, hlo_ref, ht_ref):
    o = of_ref[...] + ob_ref[...]
    g = hg_ref[...]
    g = g * (1.0 / (1.0 + jnp.exp(-g)))
    n_g, n_h = GQA_HEADS * HEAD_DIM, HG_HEADS * HEAD_DIM
    mix = _dot(ga_ref[...], w_ref[0:n_g, :]) + _dot(ma_ref[...], w_ref[n_g + n_h:, :])
    for h in range(HG_HEADS):
        sl = slice(h * HEAD_DIM, (h + 1) * HEAD_DIM)
        oh = _rms(o[:, sl], HEAD_DIM) * on_ref[...] * g[:, sl]
        mix = mix + _dot(oh.astype(BF16), w_ref[n_g + h * HEAD_DIM:n_g + (h + 1) * HEAD_DIM, :])
    x = x_ref[...] + g1_ref[...] * mix
    xo_ref[...] = x
    h2 = _rms(x, x.shape[-1]) * g2_ref[...] * (1.0 + sc2_ref[...]) + sh2_ref[...]
    hi, lo = _split2(h2)
    hhi_ref[...] = hi
    hlo_ref[...] = lo
    ht_ref[...] = h2.T.astype(BF16)


def _mixer_out(x, ga, o2, z, ma, w, on, g2, modv, mod_row_of_tile, tm, layer):
    T, D = x.shape
    wh = HG_HEADS * HEAD_DIM

    def mspec(k):
        return pl.BlockSpec((None, 1, D), lambda i: (mod_row_of_tile(i) + k, 0, 0))

    return pl.pallas_call(
        _out_kernel,
        grid=(T // tm,),
        in_specs=[pl.BlockSpec((tm, D), lambda i: (i, 0)),
                  pl.BlockSpec((tm, ga.shape[1]), lambda i: (i, 0)),
                  pl.BlockSpec((None, tm, wh), lambda i: (0, i, 0)),
                  pl.BlockSpec((None, tm, wh), lambda i: (1, i, 0)),
                  pl.BlockSpec((tm, wh), lambda i: (i, 9)),
                  pl.BlockSpec((tm, ma.shape[1]), lambda i: (i, 0)),
                  pl.BlockSpec((None,) + w.shape[1:], lambda i: (layer, 0, 0)),
                  pl.BlockSpec((1, HEAD_DIM), lambda i: (0, 0)),
                  pl.BlockSpec((1, D), lambda i: (0, 0)),
                  mspec(2), mspec(3), mspec(4)],
        out_specs=[pl.BlockSpec((tm, D), lambda i: (i, 0))] * 3 + [pl.BlockSpec((D, tm), lambda i: (0, i))],
        out_shape=[jax.ShapeDtypeStruct((T, D), F32), jax.ShapeDtypeStruct((T, D), BF16),
                   jax.ShapeDtypeStruct((T, D), BF16), jax.ShapeDtypeStruct((D, T), BF16)],
        compiler_params=_cparams("parallel"),
        name="mixer_out",
    )(x, ga, o2, o2, z, ma, w, on, g2, modv, modv, modv)


def _ws_kernel(k_ref, wq_ref, hi_ref, lo_ref):
    khi, klo = _split2(k_ref[...])
    whi, wlo = _split2(wq_ref[...])
    acc = _dot_nt(khi, whi) + _dot_nt(klo, whi) + _dot_nt(khi, wlo)
    hi, lo = _split2(acc * LOG2E)
    hi_ref[...] = hi
    lo_ref[...] = lo


def _score_weights(keys, wq):
    L, D, _ = wq.shape
    nb = 2 * PEER_HEADS
    sub = keys.shape[-1]
    kk = keys.reshape(L, nb, PEER_KEYS, sub)
    out = jax.ShapeDtypeStruct((L, nb * PEER_KEYS, D), BF16)
    ospec = pl.BlockSpec((None, PEER_KEYS, D), lambda l, b: (l, (b % 2) * PEER_HEADS + b // 2, 0))
    return pl.pallas_call(
        _ws_kernel,
        grid=(L, nb),
        in_specs=[pl.BlockSpec((None, None, PEER_KEYS, sub), lambda l, b: (l, b, 0, 0)),
                  pl.BlockSpec((None, D, sub), lambda l, b: (l, 0, b))],
        out_specs=[ospec, ospec],
        out_shape=[out, out],
        compiler_params=_cparams("parallel", "parallel"),
        name="score_weights",
    )(kk, wq)


def _score_kernel(whi_ref, wlo_ref, hhi_ref, hlo_ref, o_ref):
    whi, hhi = whi_ref[...], hhi_ref[...]
    o_ref[...] = _dot_nt(whi, hhi) + _dot_nt(whi, hlo_ref[...]) + _dot_nt(wlo_ref[...], hhi)


def _scores(whi, wlo, hhi, hlo, tm, layer):
    _, R, D = whi.shape
    T = hhi.shape[0]
    tr = min(512, R)
    return pl.pallas_call(
        _score_kernel,
        grid=(R // tr, T // tm),
        in_specs=[pl.BlockSpec((None, tr, D), lambda r, i: (layer, r, 0)), pl.BlockSpec((None, tr, D), lambda r, i: (layer, r, 0)),
                  pl.BlockSpec((tm, D), lambda r, i: (i, 0)), pl.BlockSpec((tm, D), lambda r, i: (i, 0))],
        out_specs=pl.BlockSpec((tr, tm), lambda r, i: (r, i)),
        out_shape=jax.ShapeDtypeStruct((R, T), F32),
        compiler_params=_cparams("parallel", "parallel"),
        name="scores",
    )(whi, wlo, hhi, hlo)


def _batcher_pairs(n):
    pairs = []
    p = 1
    while p < n:
        k = p
        while k >= 1:
            for j in range(k % p, n - k, 2 * k):
                for i in range(min(k, n - j - k)):
                    if (i + j) // (2 * p) == (i + j + k) // (2 * p):
                        pairs.append((i + j, i + j + k))
            k //= 2
        p *= 2
    return pairs


_SORT16 = _batcher_pairs(PEER_TOPK)


def _sort_desc(v):
    v = list(v)
    for i, j in _SORT16:
        v[i], v[j] = jnp.maximum(v[i], v[j]), jnp.minimum(v[i], v[j])
    return v


def _merge_top(a, b):
    n = len(a)
    v = [jnp.maximum(a[i], b[n - 1 - i]) for i in range(n)]
    s = n // 2
    while s >= 1:
        for i in range(n):
            if (i & s) == 0:
                v[i], v[i + s] = jnp.maximum(v[i], v[i + s]), jnp.minimum(v[i], v[i + s])
        s //= 2
    return v


def _top16_of_keys(s_ref, row0, lanes):
    k = PEER_TOPK
    groups = [_sort_desc([s_ref[pl.ds(row0 + PEER_HEADS * (k * g + i), PEER_HEADS), lanes] for i in range(k)])
              for g in range(PEER_KEYS // k)]
    while len(groups) > 1:
        groups = [_merge_top(groups[i], groups[i + 1]) for i in range(0, len(groups), 2)]
    return groups[0]


def _topk_kernel(s_ref, thr_ref, s1n_ref):
    k = PEER_TOPK
    half = PEER_HEADS * PEER_KEYS
    ninf = jnp.full((PEER_HEADS, 128), -jnp.inf, F32)

    def chunk(ch, carry):
        lanes = pl.ds(pl.multiple_of(ch * 128, 128), 128)
        a = _top16_of_keys(s_ref, 0, lanes)
        b = _top16_of_keys(s_ref, half, lanes)
        cand = [[a[i] + b[j] for j in range(k // (i + 1))] for i in range(k)]
        g0 = cand[0]
        g1 = _sort_desc(cand[1] + cand[2] + cand[5] + cand[8])
        g2 = _sort_desc(cand[3] + cand[4] + cand[6] + cand[7] + cand[9] + cand[10] + cand[11] + cand[12] + cand[13])
        g3 = cand[14] + cand[15] + [ninf] * (k - 2)
        top = _merge_top(_merge_top(g0, g1), _merge_top(g2, g3))
        zsum = jnp.zeros((PEER_HEADS, 128), F32)
        for t in top:
            zsum = zsum + jnp.exp2(t - top[0])
        c0 = top[0] + jnp.log2(zsum)
        thr_ref[:, lanes] = top[k - 1] - c0
        for key in range(PEER_KEYS):
            tile = s_ref[pl.ds(half + PEER_HEADS * key, PEER_HEADS), lanes] - c0
            s1n_ref[ch, pl.ds(key, PEER_HEADS, stride=PEER_KEYS), :] = tile
        return carry

    lax.fori_loop(0, s_ref.shape[1] // 128, chunk, 0)


def _topk(st, tm):
    R, T = st.shape
    return pl.pallas_call(
        _topk_kernel,
        grid=(T // tm,),
        in_specs=[pl.BlockSpec((R, tm), lambda i: (0, i))],
        out_specs=[pl.BlockSpec((PEER_HEADS, tm), lambda i: (0, i)),
                   pl.BlockSpec((tm // 128, R // 2, 128), lambda i: (i, 0, 0))],
        out_shape=[jax.ShapeDtypeStruct((PEER_HEADS, T), F32), jax.ShapeDtypeStruct((T // 128, R // 2, 128), F32)],
        compiler_params=_cparams("parallel"),
        name="topk",
    )(st)


def _peer_kernel(ht_ref, s0_ref, s1n_ref, thr_ref, u_ref, vt_ref, x_ref, g2_ref, o_ref,
                 pre_a, pre_b, c_a, c_b, acc_sc):
    s = pl.program_id(1)
    te, tm = c_a.shape
    nk = PEER_KEYS

    @pl.when(s == 0)
    def _():
        acc_sc[...] = jnp.zeros_like(acc_sc)
        pre_b[...] = jnp.zeros_like(pre_b)
        c_a[...] = jnp.zeros_like(c_a)

    d = ht_ref.shape[0]
    ni = te // nk
    jr = 16
    nblk = (tm // 128) * (nk // jr)
    nh = 2 if tm % 256 == 0 else 1
    th = tm // nh
    ka = mc = max(1, min(nblk // nh, d // 256))
    kq, mq = d // ka, d // mc

    def pre_piece(p, pre_w):
        half, q = divmod(p, ka)
        part = _dot(u_ref[:, q * kq:(q + 1) * kq], ht_ref[q * kq:(q + 1) * kq, half * th:(half + 1) * th])
        if q == 0:
            pre_w[:, half * th:(half + 1) * th] = part
        else:
            pre_w[:, half * th:(half + 1) * th] += part

    def value_piece(p, c_r):
        half, q = divmod(p, mc)
        acc_sc[q * mq:(q + 1) * mq, half * th:(half + 1) * th] += _dot(
            vt_ref[q * mq:(q + 1) * mq, :], c_r[:, half * th:(half + 1) * th])

    slots = {}
    for p in range(nh * ka):
        slots.setdefault(p * nblk // (nh * ka), []).append((pre_piece, p))
    for p in range(nh * mc):
        slots.setdefault((p * nblk // (nh * mc) + 1) % nblk, []).append((value_piece, p))

    def stages(pre_w, pre_r, c_w, c_r):
        for blk in range(nblk):
            for piece, p in slots.get(blk, ()):
                piece(p, pre_w if piece is pre_piece else c_r)
            ch, jq = divmod(blk, nk // jr)
            lanes = slice(ch * 128, (ch + 1) * 128)
            gates = [jnp.zeros((jr, 128), F32) for _ in range(ni)]
            for h in range(PEER_HEADS):
                s1 = s1n_ref[ch, h * nk + jq * jr:h * nk + (jq + 1) * jr, :]
                thr = thr_ref[h:h + 1, lanes]
                for ii in range(ni):
                    r0 = ii * PEER_HEADS + h
                    val = s1 + s0_ref[r0:r0 + 1, lanes]
                    gates[ii] = gates[ii] + jnp.where(val >= thr, jnp.exp2(val), 0.0)
            for ii in range(ni):
                rows = slice(ii * nk + jq * jr, ii * nk + (jq + 1) * jr)
                pre = pre_r[rows, lanes]
                act = 0.5 * pre * (1.0 + lax.erf(pre * (2.0 ** -0.5)))
                c_w[rows, lanes] = (gates[ii] * act).astype(BF16)

    @pl.when(s % 2 == 0)
    def _():
        stages(pre_a, pre_b, c_b, c_a)

    @pl.when(s % 2 == 1)
    def _():
        stages(pre_b, pre_a, c_a, c_b)

    @pl.when(s == pl.num_programs(1) - 1)
    def _():
        o_ref[...] = x_ref[...] + g2_ref[...] * acc_sc[...].T


def _peer(ht, st, thr, s1n, u, vt, x, modv, mod_row_of_tile, tm, layer):
    T, D = x.shape
    _, ne, _, te = vt.shape
    half = PEER_HEADS * PEER_KEYS
    tile = lambda s, lag: jnp.clip(s - lag, 0, ne - 1)
    return pl.pallas_call(
        _peer_kernel,
        grid=(T // tm, ne + 2),
        in_specs=[pl.BlockSpec((D, tm), lambda i, s: (0, i)),
                  pl.BlockSpec((te // PEER_KEYS * PEER_HEADS, tm), lambda i, s: (tile(s, 1), i)),
                  pl.BlockSpec((tm // 128, half, 128), lambda i, s: (i, 0, 0)),
                  pl.BlockSpec((PEER_HEADS, tm), lambda i, s: (0, i)),
                  pl.BlockSpec((None, te, D), lambda i, s: (layer, tile(s, 0), 0)),
                  pl.BlockSpec((None, None, D, te), lambda i, s: (layer, tile(s, 2), 0, 0)),
                  pl.BlockSpec((tm, D), lambda i, s: (i, 0)),
                  pl.BlockSpec((None, 1, D), lambda i, s: (mod_row_of_tile(i) + 5, 0, 0))],
        out_specs=pl.BlockSpec((tm, D), lambda i, s: (i, 0)),
        out_shape=jax.ShapeDtypeStruct((T, D), F32),
        scratch_shapes=[pltpu.VMEM((te, tm), F32), pltpu.VMEM((te, tm), F32), pltpu.VMEM((te, tm), BF16),
                        pltpu.VMEM((te, tm), BF16), pltpu.VMEM((D, tm), F32)],
        compiler_params=_cparams("parallel", "arbitrary"),
        name="peer_experts",
    )(ht, st, s1n, thr, u, vt, x, modv)


def _rope_tables(n, ident_rows):
    pos = jnp.arange(n, dtype=jnp.int32)
    row, col = pos // GRID_W, pos % GRID_W
    lane = jnp.arange(HEAD_DIM)

    def build(nfreq, width):
        inv = ROPE_THETA ** (-jnp.arange(nfreq, dtype=F32) / nfreq)
        ar = row.astype(F32)[:, None] * inv[None, :]
        ac = col.astype(F32)[:, None] * inv[None, :]
        cos = jnp.concatenate([jnp.cos(ar), jnp.cos(ar), jnp.cos(ac), jnp.cos(ac)], axis=-1)
        sin = jnp.concatenate([jnp.sin(ar), jnp.sin(ar), jnp.sin(ac), jnp.sin(ac)], axis=-1)
        pad = HEAD_DIM - width
        cos = jnp.pad(cos, ((0, 0), (0, pad)))
        sin = jnp.pad(sin, ((0, 0), (0, pad)))
        first = (lane % (2 * nfreq)) < nfreq
        tab = jnp.stack([cos, jnp.where(first, 0.0, sin), jnp.where(first, -sin, 0.0)])
        ident = jnp.stack([jnp.ones((ident_rows, HEAD_DIM), F32), jnp.zeros((ident_rows, HEAD_DIM), F32),
                           jnp.zeros((ident_rows, HEAD_DIM), F32)])
        return jnp.concatenate([ident, tab], axis=1)

    return build(HEAD_DIM // 4, HEAD_DIM), build(MLA_ROPE // 4, MLA_ROPE)


def kernel(x_prompt, x_sample, c, cache_gqa_k, cache_gqa_v, cache_mla_ckv, cache_mla_krope, state_hgrn, c_ctx,
           w_mod, b_mod, norm1_g, norm2_g, w_in, gqa_qn, gqa_kn, mla_qa_n, mla_kva_n, w_mla_qb, w_mla_kvb,
           mla_qn, mla_kn, hg_lb_logits, hg_on, w_out, peer_wq, peer_keys, peer_u, peer_v):
    B, S, D = x_prompt.shape
    DB, DS, _ = x_sample.shape
    L = w_mod.shape[0]
    P = cache_gqa_k.shape[2]
    TC, TL = B * S, DB * DS
    T = TC + TL
    tm = 512 if (TC % 512 == 0 and DS % 512 == 0) else 128
    tm_small = min(256, tm)

    ngroups = 1 + DB

    def mod_row_of(t, layer):
        def f(i):
            group = jnp.where(i < TC // t, 0, 1 + (i - TC // t) // (DS // t))
            return (layer * ngroups + group) * 6
        return f

    def table_of(t):
        def f(i):
            return jnp.where(i < TC // t, 0, 1 + (i - TC // t) % (DS // t))
        return f

    cond8 = jnp.zeros((8, D), F32).at[0].set(c_ctx).at[1:1 + DB].set(c)
    modv = _modulation(cond8, w_mod, b_mod)[:, :ngroups].reshape(L * ngroups * 6, 1, D)

    lb_cum = jnp.cumsum(jax.nn.softmax(hg_lb_logits.astype(F32), axis=0), axis=0)
    lbs = jnp.maximum(lb_cum - lb_cum[:1], 0.0)
    lbtab = jnp.stack([jnp.log(lbs), jnp.log1p(-lbs), 1.0 - lbs] + [jnp.zeros_like(lbs)] * 5, axis=2)

    sizes = np.cumsum([0, 768, 256, 256, 512, 512, 512, 512, 512, 512, 512, 64])
    part = lambda a, k: a[..., sizes[k]:sizes[k + 1]]
    order = (0, 1, 2, 10, 8, 9, 3, 4, 5, 6, 7)
    cols = []
    for k in order:
        cols.append(part(w_in, k))
        if k == 10:
            cols.append(jnp.zeros((L, D, 256 - MLA_ROPE), w_in.dtype))
    w_in_b = jnp.concatenate(cols, axis=-1).astype(BF16)
    w_in_b = jnp.swapaxes(w_in_b.reshape(L, D, -1, IN_TILE), 1, 2)

    wqb = w_mla_qb.reshape(L, MLA_RANK, MLA_HEADS, MLA_QK)
    wqb = jnp.pad(wqb, ((0, 0), (0, 0), (0, 0), (0, MLA_PAD - MLA_QK))).reshape(L, MLA_RANK, MLA_HEADS * MLA_PAD).astype(BF16)
    wkvb = w_mla_kvb.reshape(L, MLA_RANK, MLA_HEADS, 2, HEAD_DIM)
    wkvb = jnp.moveaxis(wkvb, 3, 2).reshape(L, MLA_RANK, 2 * MLA_HEADS * HEAD_DIM).astype(BF16)
    rpad = lambda g: jnp.pad(g[:, HEAD_DIM:], ((0, 0), (0, HEAD_DIM - MLA_ROPE)))[:, None, :]
    mqn_n, mqn_r = mla_qn[:, None, :HEAD_DIM], rpad(mla_qn)
    mkn_n, mkn_r = mla_kn[:, None, :HEAD_DIM], rpad(mla_kn)
    w_out_b = w_out.astype(BF16)
    u_b = peer_u.astype(BF16)
    vt_b = jnp.swapaxes(peer_v.reshape(L, -1, PEER_EXPERT_TILE, D), 2, 3).astype(BF16)
    def key_major(w):
        return jnp.swapaxes(w.reshape(L, 2, PEER_HEADS, PEER_KEYS, D), 2, 3).reshape(L, 2 * PEER_HEADS * PEER_KEYS, D)

    ws_hi, ws_lo = map(key_major, _score_weights(peer_keys, peer_wq))

    tabg, tabm = _rope_tables(DS, tm_small)

    cck = jnp.moveaxis(cache_mla_ckv, 1, 0).reshape(L, DB * P, MLA_RANK)
    ckr = jnp.pad(jnp.moveaxis(cache_mla_krope, 1, 0).reshape(L, DB * P, MLA_ROPE), ((0, 0), (0, 0), (0, HEAD_DIM - MLA_ROPE)))
    cmk, cmv = _cache_expand(cck, ckr, wkvb, mkn_n, mkn_r)
    cgk = jnp.moveaxis(cache_gqa_k, 1, 0).reshape(L, DB * P, -1).astype(BF16)
    cgv = jnp.transpose(cache_gqa_v.reshape(DB, L, P, -1), (1, 3, 0, 2)).reshape(L, -1, DB * P).astype(BF16)
    s0_ctx = jnp.zeros((B, 2, HG_HEADS, HEAD_DIM, HEAD_DIM), F32)

    x = jnp.concatenate([x_prompt.reshape(TC, D), x_sample.reshape(TL, D)], axis=0)
    kvw = GQA_KV_HEADS * HEAD_DIM
    caches = tuple(jnp.zeros((B, L, S, w), F32) for w in (kvw, kvw, MLA_RANK, MLA_ROPE))
    states = jnp.zeros((B, L, 2, HG_HEADS, HEAD_DIM, HEAD_DIM), F32)
    ga = jnp.zeros((T, GQA_HEADS * HEAD_DIM), BF16)
    ma = jnp.zeros((T, MLA_HEADS * HEAD_DIM), BF16)
    o2 = jnp.zeros((2, T, HG_HEADS * HEAD_DIM), F32)
    tq_c = min(256, S)
    tq_l, tk_l = min(1024, DS), min(512, P, DS)
    for l in range(L):
        z = _in_proj(x, norm1_g[l][None], modv, w_in_b, mod_row_of(tm, l), tm, l)
        gq, gkr, gvt, mq, mk, mvt, *caches = _prep(
            z, tabg, tabm, table_of(tm_small), tm_small, gqa_qn[l][None], gqa_kn[l][None], mla_qa_n[l][None],
            mla_kva_n[l][None], wqb[l], wkvb[l], mqn_n[l], mqn_r[l], mkn_n[l], mkn_r[l], caches, l)

        gqa = dict(heads=GQA_HEADS, group=GQA_HEADS // GQA_KV_HEADS, dq=HEAD_DIM, dv=HEAD_DIM, hb=GQA_HEADS // GQA_KV_HEADS)
        mla = dict(heads=MLA_HEADS, group=1, dq=MLA_PAD, dv=HEAD_DIM, hb=MLA_HEADS // 2)
        ctx_a = dict(batch=B, nq=S, nk=S, row0=0, tq=tq_c, tk=tq_c)
        lat_a = dict(batch=DB, nq=DS, nk=DS, row0=TC, tq=tq_l, tk=tk_l)
        ga = _attention(gq, gkr, gvt, **gqa, **ctx_a, prev=ga)
        ga = _attention(gq, gkr, gvt, **gqa, **lat_a, prev=ga, cache=(cgk[l], cgv[l]))
        ma = _attention(mq, mk, mvt, **mla, **ctx_a, prev=ma)
        ma = _attention(mq, mk, mvt, **mla, **lat_a, prev=ma, cache=(cmk[l], cmv[l]))

        o2, states = _hgrn(z, lbtab[l], s0_ctx, batch=B, n=S, row0=0, prev=o2, states=states, layer=l)
        o2, _ = _hgrn(z, lbtab[l], state_hgrn[:, l].astype(F32), batch=DB, n=DS, row0=TC, prev=o2)

        x, hhi, hlo, ht = _mixer_out(x, ga, o2, z, ma, w_out_b, hg_on[l][None], norm2_g[l][None], modv,
                                 mod_row_of(tm_small, l), tm_small, l)
        st = _scores(ws_hi, ws_lo, hhi, hlo, tm, l)
        thr, s1n = _topk(st, tm)
        x = _peer(ht, st, thr, s1n, u_b, vt_b, x, modv, mod_row_of(tm, l), tm, l)

    new_k, new_v, new_ckv, new_kr = caches
    return (x[:TC].reshape(B, S, D), x[TC:].reshape(DB, DS, D),
            new_k.reshape(B, L, S, GQA_KV_HEADS, HEAD_DIM), new_v.reshape(B, L, S, GQA_KV_HEADS, HEAD_DIM),
            new_ckv, new_kr, states)
```

```python
import functools

import numpy as np
import jax
import jax.numpy as jnp
from jax import lax
from jax.experimental import pallas as pl
from jax.experimental.pallas import tpu as pltpu

F32 = jnp.float32
BF16 = jnp.bfloat16

EPS = 1e-6
LOG2E = 1.4426950408889634
ROPE_THETA = 10000.0
GRID_W = 64

HEAD_DIM = 128
GQA_HEADS, GQA_KV_HEADS = 6, 2
HG_HEADS = 4
MLA_HEADS, MLA_ROPE, MLA_RANK = 6, 64, 512
MLA_QK = HEAD_DIM + MLA_ROPE
MLA_PAD = 2 * HEAD_DIM
PEER_HEADS, PEER_KEYS, PEER_TOPK = 8, 128, 16
HG_CHUNK = 128
PEER_EXPERT_TILE = 512
IN_TILE = 1024

VMEM_LIMIT = 60 * 1024 * 1024


def _cparams(*sem):
    return pltpu.CompilerParams(dimension_semantics=sem, vmem_limit_bytes=VMEM_LIMIT)


def _dot(a, b):
    return jnp.dot(a, b, preferred_element_type=F32)


def _dot_nt(a, b):
    return lax.dot_general(a, b, (((1,), (1,)), ((), ())), preferred_element_type=F32)


def _dot_tn(a, b):
    return lax.dot_general(a, b, (((0,), (0,)), ((), ())), preferred_element_type=F32)


def _split2(a):
    hi = a.astype(BF16)
    lo = (a - hi.astype(F32)).astype(BF16)
    return hi, lo


def _split3(a):
    p1 = a.astype(BF16)
    r = a - p1.astype(F32)
    p2 = r.astype(BF16)
    p3 = (r - p2.astype(F32)).astype(BF16)
    return p1, p2, p3


def _rms(x, n):
    return x * lax.rsqrt(jnp.sum(x * x, axis=-1, keepdims=True) * (1.0 / n) + EPS)


def _mod_kernel(cond_ref, w_ref, b_ref, o_ref):
    c = cond_ref[...]
    s = c * (1.0 / (1.0 + jnp.exp(-c)))
    shi, slo = _split2(s)
    whi, wlo = _split2(w_ref[...])
    o_ref[...] = _dot(shi, whi) + _dot(slo, whi) + _dot(shi, wlo) + b_ref[...]


def _modulation(cond8, w_mod, b_mod):
    L, D, N = w_mod.shape
    tn = min(512, N)
    return pl.pallas_call(
        _mod_kernel,
        grid=(L, N // tn),
        in_specs=[pl.BlockSpec((8, D), lambda l, j: (0, 0)),
                  pl.BlockSpec((None, D, tn), lambda l, j: (l, 0, j)),
                  pl.BlockSpec((None, 1, tn), lambda l, j: (l, 0, j))],
        out_specs=pl.BlockSpec((None, 8, tn), lambda l, j: (l, 0, j)),
        out_shape=jax.ShapeDtypeStruct((L, 8, N), F32),
        compiler_params=_cparams("parallel", "parallel"),
        name="modulation",
    )(cond8, w_mod, b_mod.reshape(L, 1, N))


def _in_kernel(x_ref, g_ref, sh_ref, sc_ref, w_ref, o_ref, h_sc):
    @pl.when(pl.program_id(1) == 0)
    def _():
        x = x_ref[...]
        h = _rms(x, x.shape[-1]) * g_ref[...] * (1.0 + sc_ref[...]) + sh_ref[...]
        h_sc[...] = h.astype(BF16)

    o_ref[...] = _dot(h_sc[...], w_ref[...])


def _in_proj(x, g, modv, w, mod_row_of_tile, tm, layer):
    T, D = x.shape
    _, nj, _, tn = w.shape
    N = nj * tn

    def mspec(k):
        return pl.BlockSpec((None, 1, D), lambda i, j: (mod_row_of_tile(i) + k, 0, 0))

    return pl.pallas_call(
        _in_kernel,
        grid=(T // tm, nj),
        in_specs=[pl.BlockSpec((tm, D), lambda i, j: (i, 0)),
                  pl.BlockSpec((1, D), lambda i, j: (0, 0)),
                  mspec(0), mspec(1),
                  pl.BlockSpec((None, None, D, tn), lambda i, j: (layer, j, 0, 0))],
        out_specs=pl.BlockSpec((tm, tn), lambda i, j: (i, j)),
        out_shape=jax.ShapeDtypeStruct((T, N), F32),
        scratch_shapes=[pltpu.VMEM((tm, D), BF16)],
        compiler_params=_cparams("parallel", "arbitrary"),
        name="in_proj",
    )(x, g, modv, modv, w)


def _rope(x, tab_ref, blk):
    return (x * tab_ref[0] + pltpu.roll(x, blk, 1) * tab_ref[1]
            + pltpu.roll(x, HEAD_DIM - blk, 1) * tab_ref[2])


def _mla_kv(ckv, krope, wkvb_ref, gkn_ref, gkr_ref, tab_ref, mk_ref, mv_ref):
    kv = _dot(ckv.astype(BF16), wkvb_ref[...])
    nv = MLA_HEADS * HEAD_DIM
    mv_ref[...] = kv[:, nv:].T.astype(BF16)
    ssr = jnp.sum(krope * krope, axis=-1, keepdims=True)
    for h in range(MLA_HEADS):
        kn = kv[:, h * HEAD_DIM:(h + 1) * HEAD_DIM]
        rs = lax.rsqrt((jnp.sum(kn * kn, axis=-1, keepdims=True) + ssr) * (1.0 / MLA_QK) + EPS)
        kr = krope * rs * gkr_ref[...]
        if tab_ref is not None:
            kr = _rope(kr, tab_ref, MLA_ROPE // 4)
        mk_ref[:, h * MLA_PAD:h * MLA_PAD + HEAD_DIM] = (kn * rs * gkn_ref[...]).astype(BF16)
        mk_ref[:, h * MLA_PAD + HEAD_DIM:(h + 1) * MLA_PAD] = kr.astype(BF16)


def _prep_kernel(gq_ref, gk_ref, gv_ref, mkr_ref, mqa_ref, mkva_ref, tg_ref, tm_ref,
                 gqn_ref, gkn_ref, qan_ref, kvan_ref, wqb_ref, wkvb_ref,
                 mqn_n_ref, mqn_r_ref, mkn_n_ref, mkn_r_ref, *rest, n_ctx_tiles):
    gq_o, gkr_o, gvt_o, mq_o, mk_o, mv_o, nk_o, nv_o, nckv_o, nkr_o = rest[4:]
    is_ctx = pl.program_id(0) < n_ctx_tiles
    gscale = HEAD_DIM ** -0.5 * LOG2E
    for h in range(GQA_HEADS):
        sl = slice(h * HEAD_DIM, (h + 1) * HEAD_DIM)
        q = _rms(gq_ref[:, sl], HEAD_DIM) * gqn_ref[...]
        gq_o[:, sl] = (_rope(q, tg_ref, HEAD_DIM // 4) * gscale).astype(BF16)
    for h in range(GQA_KV_HEADS):
        sl = slice(h * HEAD_DIM, (h + 1) * HEAD_DIM)
        k = _rms(gk_ref[:, sl], HEAD_DIM) * gkn_ref[...]
        gkr_o[:, sl] = _rope(k, tg_ref, HEAD_DIM // 4).astype(BF16)

        @pl.when(is_ctx)
        def _():
            nk_o[:, sl] = k
    gvt_o[...] = gv_ref[...].T.astype(BF16)

    qa = _rms(mqa_ref[...], MLA_RANK) * qan_ref[...]
    mq = _dot(qa.astype(BF16), wqb_ref[...])
    mscale = MLA_QK ** -0.5 * LOG2E
    for h in range(MLA_HEADS):
        qn = mq[:, h * MLA_PAD:h * MLA_PAD + HEAD_DIM]
        qr = mq[:, h * MLA_PAD + HEAD_DIM:(h + 1) * MLA_PAD]
        ss = jnp.sum(qn * qn, axis=-1, keepdims=True) + jnp.sum(qr * qr, axis=-1, keepdims=True)
        rs = lax.rsqrt(ss * (1.0 / MLA_QK) + EPS) * mscale
        mq_o[:, h * MLA_PAD:h * MLA_PAD + HEAD_DIM] = (qn * rs * mqn_n_ref[...]).astype(BF16)
        mq_o[:, h * MLA_PAD + HEAD_DIM:(h + 1) * MLA_PAD] = _rope(qr * rs * mqn_r_ref[...], tm_ref, MLA_ROPE // 4).astype(BF16)

    ckv = _rms(mkva_ref[...], MLA_RANK) * kvan_ref[...]

    @pl.when(is_ctx)
    def _():
        nv_o[...] = gv_ref[...]
        nckv_o[...] = ckv
        nkr_o[...] = mkr_ref[:, :MLA_ROPE]

    _mla_kv(ckv, mkr_ref[...], wkvb_ref, mkn_n_ref, mkn_r_ref, tm_ref, mk_o, mv_o)


def _prep(z, tabg, tabm, table_of_tile, tm, gqn, gkn, qan, kvan, wqb, wkvb, mqn_n, mqn_r, mkn_n, mkn_r, caches, layer):
    T = z.shape[0]
    B, _, S, _ = caches[0].shape
    per_seq = S // tm
    n_ctx_tiles = B * per_seq
    row = lambda w: pl.BlockSpec((1, w), lambda i: (0, 0))
    full = lambda a: pl.BlockSpec(a.shape, lambda i: (0, 0))
    tab = pl.BlockSpec((3, tm, HEAD_DIM), lambda i: (0, table_of_tile(i), 0))
    zspec = lambda w, blk: pl.BlockSpec((tm, w), lambda i: (i, blk))
    widths = (GQA_HEADS * HEAD_DIM, GQA_KV_HEADS * HEAD_DIM, GQA_KV_HEADS * HEAD_DIM,
              MLA_HEADS * MLA_PAD, MLA_HEADS * MLA_PAD, MLA_HEADS * HEAD_DIM)
    transposed = (False, False, True, False, False, True)
    ospec = lambda w, t: pl.BlockSpec((w, tm), lambda i: (0, i)) if t else pl.BlockSpec((tm, w), lambda i: (i, 0))

    def cspec(c):
        def idx(i):
            t = jnp.minimum(i, n_ctx_tiles - 1)
            return (t // per_seq, layer, t % per_seq, 0)
        return pl.BlockSpec((None, None, tm, c.shape[-1]), idx)

    n_in = 18
    return pl.pallas_call(
        functools.partial(_prep_kernel, n_ctx_tiles=n_ctx_tiles),
        grid=(T // tm,),
        in_specs=[zspec(768, 0), zspec(256, 3), zspec(256, 4), zspec(128, 10), zspec(512, 3), zspec(512, 4),
                  tab, tab, row(128), row(128), row(512), row(512), full(wqb), full(wkvb),
                  row(128), row(128), row(128), row(128)] + [pl.BlockSpec(memory_space=pl.ANY)] * 4,
        out_specs=[ospec(w, t) for w, t in zip(widths, transposed)] + [cspec(c) for c in caches],
        out_shape=[jax.ShapeDtypeStruct((w, T) if t else (T, w), BF16) for w, t in zip(widths, transposed)]
        + [jax.ShapeDtypeStruct(c.shape, c.dtype) for c in caches],
        input_output_aliases={n_in + k: len(widths) + k for k in range(4)},
        compiler_params=_cparams("arbitrary"),
        name="prep",
    )(z, z, z, z, z, z, tabg, tabm, gqn, gkn, qan, kvan, wqb, wkvb, mqn_n, mqn_r, mkn_n, mkn_r, *caches)


def _cache_kernel(ckv_ref, kr_ref, wkvb_ref, gkn_ref, gkr_ref, mk_o, mv_o):
    _mla_kv(ckv_ref[...], kr_ref[...], wkvb_ref, gkn_ref, gkr_ref, None, mk_o, mv_o)


def _cache_expand(ckv, kr, wkvb, mkn_n, mkn_r):
    L, R, _ = ckv.shape
    tm = min(512, R)
    return pl.pallas_call(
        _cache_kernel,
        grid=(L, R // tm),
        in_specs=[pl.BlockSpec((None, tm, MLA_RANK), lambda l, i: (l, i, 0)),
                  pl.BlockSpec((None, tm, HEAD_DIM), lambda l, i: (l, i, 0)),
                  pl.BlockSpec((None,) + wkvb.shape[1:], lambda l, i: (l, 0, 0)),
                  pl.BlockSpec((None, 1, HEAD_DIM), lambda l, i: (l, 0, 0)),
                  pl.BlockSpec((None, 1, HEAD_DIM), lambda l, i: (l, 0, 0))],
        out_specs=[pl.BlockSpec((None, tm, MLA_HEADS * MLA_PAD), lambda l, i: (l, i, 0)),
                   pl.BlockSpec((None, MLA_HEADS * HEAD_DIM, tm), lambda l, i: (l, 0, i))],
        out_shape=[jax.ShapeDtypeStruct((L, R, MLA_HEADS * MLA_PAD), BF16),
                   jax.ShapeDtypeStruct((L, MLA_HEADS * HEAD_DIM, R), BF16)],
        compiler_params=_cparams("parallel", "parallel"),
        name="cache_expand",
    )(ckv, kr, wkvb, mkn_n, mkn_r)


def _attn_kernel(*refs, heads, group, dq, dv, ncb):
    n_in = (5 if ncb else 3) + 1
    q_ref = refs[0]
    o_ref, m_sc, l_sc, acc_sc = refs[n_in:]
    ki = pl.program_id(3)

    @pl.when(ki == 0)
    def _():
        m_sc[...] = jnp.full_like(m_sc, -jnp.inf)
        l_sc[...] = jnp.zeros_like(l_sc)
        acc_sc[...] = jnp.zeros_like(acc_sc)

    def step(k_ref, vt_ref):
        def scores(h):
            kv = h // group
            return _dot_nt(k_ref[:, kv * dq:(kv + 1) * dq], q_ref[:, h * dq:(h + 1) * dq])

        s_next = scores(0)
        for h in range(heads):
            kv = h // group
            s = s_next
            if h + 1 < heads:
                s_next = scores(h + 1)
            m_prev = m_sc[h]
            m_new = jnp.maximum(m_prev, jnp.max(s, axis=0, keepdims=True))
            p = jnp.exp2(s - m_new)
            alpha = jnp.exp2(m_prev - m_new)
            l_sc[h] = alpha * l_sc[h] + jnp.sum(p, axis=0, keepdims=True)
            acc_sc[h] = alpha * acc_sc[h] + _dot(vt_ref[kv * dv:(kv + 1) * dv, :], p.astype(BF16))
            m_sc[h] = m_new

    if ncb:
        pl.when(ki < ncb)(lambda: step(refs[1], refs[2]))
        pl.when(ki >= ncb)(lambda: step(refs[3], refs[4]))
    else:
        step(refs[1], refs[2])

    @pl.when(ki == pl.num_programs(3) - 1)
    def _():
        for h in range(heads):
            o_ref[:, h * dv:(h + 1) * dv] = (acc_sc[h] / l_sc[h]).T.astype(o_ref.dtype)


def _attention(q, k, vt, *, batch, nq, nk, row0, heads, group, dq, dv, tq, tk, hb, prev, cache=None):
    T = q.shape[0]
    nqb, nnb = nq // tq, nk // tk
    ncb = 0 if cache is None else cache[0].shape[0] // batch // tk
    qb0, kb0 = row0 // tq, row0 // tk
    kvb = max(1, hb // group)
    new_blk = lambda b, j: kb0 + b * nnb + jnp.maximum(j - ncb, 0)
    in_specs = [pl.BlockSpec((tq, hb * dq), lambda b, h, i, j: (qb0 + b * nqb + i, h))]
    operands = [q]
    if ncb:
        cache_blk = lambda b, j: b * ncb + jnp.minimum(j, ncb - 1)
        in_specs += [pl.BlockSpec((tk, kvb * dq), lambda b, h, i, j: (cache_blk(b, j), h)),
                     pl.BlockSpec((kvb * dv, tk), lambda b, h, i, j: (h, cache_blk(b, j)))]
        operands += list(cache)
    in_specs += [pl.BlockSpec((tk, kvb * dq), lambda b, h, i, j: (new_blk(b, j), h)),
                 pl.BlockSpec((kvb * dv, tk), lambda b, h, i, j: (h, new_blk(b, j)))]
    operands += [k, vt]
    in_specs.append(pl.BlockSpec(memory_space=pl.ANY))
    aliases = {len(operands): 0}
    operands.append(prev)
    return pl.pallas_call(
        functools.partial(_attn_kernel, heads=hb, group=group, dq=dq, dv=dv, ncb=ncb),
        grid=(batch, heads // hb, nqb, ncb + nnb),
        in_specs=in_specs,
        out_specs=pl.BlockSpec((tq, hb * dv), lambda b, h, i, j: (qb0 + b * nqb + i, h)),
        out_shape=jax.ShapeDtypeStruct((T, heads * dv), BF16),
        scratch_shapes=[pltpu.VMEM((hb, 1, tq), F32), pltpu.VMEM((hb, 1, tq), F32), pltpu.VMEM((hb, dv, tq), F32)],
        input_output_aliases=aliases,
        compiler_params=_cparams("parallel", "parallel", "parallel", "arbitrary"),
        name="attention",
    )(*operands)


def _hgrn_constants(c):
    levels = int(np.log2(c))
    tri = np.zeros((2, c, c), np.float32)
    M = np.zeros((2, levels, c, c), np.float32)
    tri[0] = np.tril(np.ones((c, c), np.float32))
    for lv in range(levels):
        m = c >> (lv + 1)
        for a in range(0, c, 2 * m):
            M[0, lv, a + m:a + 2 * m, a:a + m] = 1.0
    tri[1] = tri[0][::-1, ::-1]
    M[1] = M[0][:, ::-1, ::-1]
    return tri, M


def _level_exponents(bq, bwd, c):
    rowid = lax.broadcasted_iota(jnp.int32, bq.shape, 0)
    out = []
    m = c // 2
    while m >= 1:
        if 2 * m >= 8:
            blocks = [jnp.broadcast_to(jnp.where(bwd, bq[a + m:a + m + 1, :], bq[a + m - 1:a + m, :]), (2 * m, bq.shape[1]))
                      for a in range(0, c, 2 * m)]
            ref = blocks[0] if len(blocks) == 1 else jnp.concatenate(blocks, axis=0)
        else:
            target = m - 1 + bwd.astype(jnp.int32)
            rowmod = rowid % (2 * m)
            ref = bq
            for delta in range(-m, m + 1):
                if delta != 0:
                    ref = jnp.where(rowmod + delta == target, pltpu.roll(bq, (-delta) % c, 0), ref)
        out.append(-jnp.abs(bq - ref))
        m //= 2
    return out


def _hgrn_kernel(q_ref, z_ref, v_ref, lb_ref, s0_ref, L_ref, M_ref, o_ref, sf_ref, s_sc):
    c = q_ref.shape[0]
    levels = M_ref.shape[0]

    @pl.when(pl.program_id(2) == 0)
    def _():
        s_sc[...] = s0_ref[...]

    z = z_ref[...]
    e = jnp.exp(-jnp.abs(z))
    r = 1.0 / (1.0 + e)
    logsig = jnp.minimum(z, 0.0) + jnp.log(r)
    a = lb_ref[0:1, :]
    cc = lb_ref[1:2, :] + logsig
    logf = jnp.maximum(a, cc) + jnp.log(1.0 + jnp.exp(-jnp.abs(a - cc)))
    kk = lb_ref[2:3, :] * jnp.where(z >= 0.0, e * r, r)

    tri = L_ref[...]
    p1, p2, p3 = _split3(logf)
    bq = _dot(tri, p1) + _dot(tri, p2) + _dot(tri, p3)
    tot = jnp.minimum(bq[0:1, :], bq[c - 1:c, :])
    ex_q = jnp.exp(bq)
    ex_k = jnp.exp(tot - bq)
    ex_lv = [jnp.exp(t) for t in _level_exponents(bq, pl.program_id(0) == 1, c)]
    ones = jnp.ones((c, HEAD_DIM), BF16)
    for h in range(HG_HEADS):
        sl = slice(h * HEAD_DIM, (h + 1) * HEAD_DIM)
        q, k, v = q_ref[:, sl], kk[:, sl], v_ref[:, sl]
        vb = v.astype(BF16)
        amat = jnp.zeros((c, c), F32)
        for lv in range(levels):
            el = ex_lv[lv][:, sl]
            amat = amat + M_ref[lv] * _dot_nt((q * el).astype(BF16), (k * el).astype(BF16))
        s_prev = s_sc[h]
        o = (_dot(amat.astype(BF16), vb) + jnp.sum(q * k, axis=-1, keepdims=True) * v
             + _dot((q * ex_q[:, sl]).astype(BF16), s_prev.astype(BF16)))
        o_ref[:, sl] = o
        dcol = jnp.exp(_dot_tn(p1[:, sl], ones) + _dot_tn(p2[:, sl], ones) + _dot_tn(p3[:, sl], ones))
        s_new = dcol * s_prev + _dot_tn((k * ex_k[:, sl]).astype(BF16), vb)
        s_sc[h] = s_new
        sf_ref[h] = s_new


def _hgrn(z, lbtab, s0, *, batch, n, row0, prev, states=None, layer=0):
    c = min(HG_CHUNK, n)
    nc = n // c
    cb0 = row0 // c
    Lm, Mm = _hgrn_constants(c)
    Lm, Mm = jnp.asarray(Lm, BF16), jnp.asarray(Mm, F32)
    w = HG_HEADS * HEAD_DIM

    def rows(d, b, j):
        return cb0 + b * nc + j + d * (nc - 1 - 2 * j)

    sdims = (HG_HEADS, HEAD_DIM, HEAD_DIM)
    extra, aliases = [prev], {7: 0}
    if states is None:
        s_spec = pl.BlockSpec((None, None) + sdims, lambda d, b, j: (b, d, 0, 0, 0))
        s_shape = jax.ShapeDtypeStruct((batch, 2) + sdims, F32)
    else:
        s_spec = pl.BlockSpec((None, None, None) + sdims, lambda d, b, j: (b, layer, d, 0, 0, 0))
        s_shape = jax.ShapeDtypeStruct(states.shape, F32)
        extra, aliases = [prev, states], {7: 0, 8: 1}
    n_in = 7
    return pl.pallas_call(
        lambda *refs: _hgrn_kernel(*refs[:n_in], *refs[n_in + len(extra):]),
        grid=(2, batch, nc),
        in_specs=[pl.BlockSpec((c, w), lambda d, b, j: (rows(d, b, j), 5)),
                  pl.BlockSpec((c, w), lambda d, b, j: (rows(d, b, j), 6 + d)),
                  pl.BlockSpec((c, w), lambda d, b, j: (rows(d, b, j), 8)),
                  pl.BlockSpec((None, 8, w), lambda d, b, j: (d, 0, 0)),
                  pl.BlockSpec((None, None, HG_HEADS, HEAD_DIM, HEAD_DIM), lambda d, b, j: (b, d, 0, 0, 0)),
                  pl.BlockSpec((None,) + Lm.shape[1:], lambda d, b, j: (d, 0, 0)),
                  pl.BlockSpec((None,) + Mm.shape[1:], lambda d, b, j: (d, 0, 0, 0))]
        + [pl.BlockSpec(memory_space=pl.ANY)] * len(extra),
        out_specs=[pl.BlockSpec((None, c, w), lambda d, b, j: (d, rows(d, b, j), 0)), s_spec],
        out_shape=[jax.ShapeDtypeStruct((2, z.shape[0], w), F32), s_shape],
        scratch_shapes=[pltpu.VMEM((HG_HEADS, HEAD_DIM, HEAD_DIM), F32)],
        input_output_aliases=aliases,
        compiler_params=_cparams("parallel", "parallel", "arbitrary"),
        name="hgrn_scan",
    )(z, z, z, lbtab, s0, Lm, Mm, *extra)


def _out_kernel(x_ref, ga_ref, of_ref, ob_ref, hg_ref, ma_ref, w_ref, on_ref, g2_ref,
                g1_ref, sh2_ref, sc2_ref, xo_ref, hhi_ref, hlo_ref, ht_ref):
    o = of_ref[...] + ob_ref[...]
    g = hg_ref[...]
    g = g * (1.0 / (1.0 + jnp.exp(-g)))
    n_g, n_h = GQA_HEADS * HEAD_DIM, HG_HEADS * HEAD_DIM
    mix = _dot(ga_ref[...], w_ref[0:n_g, :]) + _dot(ma_ref[...], w_ref[n_g + n_h:, :])
    for h in range(HG_HEADS):
        sl = slice(h * HEAD_DIM, (h + 1) * HEAD_DIM)
        oh = _rms(o[:, sl], HEAD_DIM) * on_ref[...] * g[:, sl]
        mix = mix + _dot(oh.astype(BF16), w_ref[n_g + h * HEAD_DIM:n_g + (h + 1) * HEAD_DIM, :])
    x = x_ref[...] + g1_ref[...] * mix
    xo_ref[...] = x
    h2 = _rms(x, x.shape[-1]) * g2_ref[...] * (1.0 + sc2_ref[...]) + sh2_ref[...]
    hi, lo = _split2(h2)
    hhi_ref[...] = hi
    hlo_ref[...] = lo
    ht_ref[...] = h2.T.astype(BF16)


def _mixer_out(x, ga, o2, z, ma, w, on, g2, modv, mod_row_of_tile, tm, layer):
    T, D = x.shape
    wh = HG_HEADS * HEAD_DIM

    def mspec(k):
        return pl.BlockSpec((None, 1, D), lambda i: (mod_row_of_tile(i) + k, 0, 0))

    return pl.pallas_call(
        _out_kernel,
        grid=(T // tm,),
        in_specs=[pl.BlockSpec((tm, D), lambda i: (i, 0)),
                  pl.BlockSpec((tm, ga.shape[1]), lambda i: (i, 0)),
                  pl.BlockSpec((None, tm, wh), lambda i: (0, i, 0)),
                  pl.BlockSpec((None, tm, wh), lambda i: (1, i, 0)),
                  pl.BlockSpec((tm, wh), lambda i: (i, 9)),
                  pl.BlockSpec((tm, ma.shape[1]), lambda i: (i, 0)),
                  pl.BlockSpec((None,) + w.shape[1:], lambda i: (layer, 0, 0)),
                  pl.BlockSpec((1, HEAD_DIM), lambda i: (0, 0)),
                  pl.BlockSpec((1, D), lambda i: (0, 0)),
                  mspec(2), mspec(3), mspec(4)],
        out_specs=[pl.BlockSpec((tm, D), lambda i: (i, 0))] * 3 + [pl.BlockSpec((D, tm), lambda i: (0, i))],
        out_shape=[jax.ShapeDtypeStruct((T, D), F32), jax.ShapeDtypeStruct((T, D), BF16),
                   jax.ShapeDtypeStruct((T, D), BF16), jax.ShapeDtypeStruct((D, T), BF16)],
        compiler_params=_cparams("parallel"),
        name="mixer_out",
    )(x, ga, o2, o2, z, ma, w, on, g2, modv, modv, modv)


def _ws_kernel(k_ref, wq_ref, hi_ref, lo_ref):
    khi, klo = _split2(k_ref[...])
    whi, wlo = _split2(wq_ref[...])
    acc = _dot_nt(khi, whi) + _dot_nt(klo, whi) + _dot_nt(khi, wlo)
    hi, lo = _split2(acc * LOG2E)
    hi_ref[...] = hi
    lo_ref[...] = lo


def _score_weights(keys, wq):
    L, D, _ = wq.shape
    nb = 2 * PEER_HEADS
    sub = keys.shape[-1]
    kk = keys.reshape(L, nb, PEER_KEYS, sub)
    out = jax.ShapeDtypeStruct((L, nb * PEER_KEYS, D), BF16)
    ospec = pl.BlockSpec((None, PEER_KEYS, D), lambda l, b: (l, (b % 2) * PEER_HEADS + b // 2, 0))
    return pl.pallas_call(
        _ws_kernel,
        grid=(L, nb),
        in_specs=[pl.BlockSpec((None, None, PEER_KEYS, sub), lambda l, b: (l, b, 0, 0)),
                  pl.BlockSpec((None, D, sub), lambda l, b: (l, 0, b))],
        out_specs=[ospec, ospec],
        out_shape=[out, out],
        compiler_params=_cparams("parallel", "parallel"),
        name="score_weights",
    )(kk, wq)


def _score_kernel(whi_ref, wlo_ref, hhi_ref, hlo_ref, o_ref):
    whi, hhi = whi_ref[...], hhi_ref[...]
    o_ref[...] = _dot_nt(whi, hhi) + _dot_nt(whi, hlo_ref[...]) + _dot_nt(wlo_ref[...], hhi)


def _scores(whi, wlo, hhi, hlo, tm, layer):
    _, R, D = whi.shape
    T = hhi.shape[0]
    tr = min(512, R)
    return pl.pallas_call(
        _score_kernel,
        grid=(R // tr, T // tm),
        in_specs=[pl.BlockSpec((None, tr, D), lambda r, i: (layer, r, 0)), pl.BlockSpec((None, tr, D), lambda r, i: (layer, r, 0)),
                  pl.BlockSpec((tm, D), lambda r, i: (i, 0)), pl.BlockSpec((tm, D), lambda r, i: (i, 0))],
        out_specs=pl.BlockSpec((tr, tm), lambda r, i: (r, i)),
        out_shape=jax.ShapeDtypeStruct((R, T), F32),
        compiler_params=_cparams("parallel", "parallel"),
        name="scores",
    )(whi, wlo, hhi, hlo)


def _batcher_pairs(n):
    pairs = []
    p = 1
    while p < n:
        k = p
        while k >= 1:
            for j in range(k % p, n - k, 2 * k):
                for i in range(min(k, n - j - k)):
                    if (i + j) // (2 * p) == (i + j + k) // (2 * p):
                        pairs.append((i + j, i + j + k))
            k //= 2
        p *= 2
    return pairs


_SORT16 = _batcher_pairs(PEER_TOPK)


def _sort_desc(v):
    v = list(v)
    for i, j in _SORT16:
        v[i], v[j] = jnp.maximum(v[i], v[j]), jnp.minimum(v[i], v[j])
    return v


def _merge_top(a, b):
    n = len(a)
    v = [jnp.maximum(a[i], b[n - 1 - i]) for i in range(n)]
    s = n // 2
    while s >= 1:
        for i in range(n):
            if (i & s) == 0:
                v[i], v[i + s] = jnp.maximum(v[i], v[i + s]), jnp.minimum(v[i], v[i + s])
        s //= 2
    return v


def _top16_of_keys(s_ref, row0, lanes):
    k = PEER_TOPK
    groups = [_sort_desc([s_ref[pl.ds(row0 + PEER_HEADS * (k * g + i), PEER_HEADS), lanes] for i in range(k)])
              for g in range(PEER_KEYS // k)]
    while len(groups) > 1:
        groups = [_merge_top(groups[i], groups[i + 1]) for i in range(0, len(groups), 2)]
    return groups[0]


def _topk_kernel(s_ref, thr_ref, s1n_ref):
    k = PEER_TOPK
    half = PEER_HEADS * PEER_KEYS
    ninf = jnp.full((PEER_HEADS, 128), -jnp.inf, F32)

    def chunk(ch, carry):
        lanes = pl.ds(pl.multiple_of(ch * 128, 128), 128)
        a = _top16_of_keys(s_ref, 0, lanes)
        b = _top16_of_keys(s_ref, half, lanes)
        cand = [[a[i] + b[j] for j in range(k // (i + 1))] for i in range(k)]
        g0 = cand[0]
        g1 = _sort_desc(cand[1] + cand[2] + cand[5] + cand[8])
        g2 = _sort_desc(cand[3] + cand[4] + cand[6] + cand[7] + cand[9] + cand[10] + cand[11] + cand[12] + cand[13])
        g3 = cand[14] + cand[15] + [ninf] * (k - 2)
        top = _merge_top(_merge_top(g0, g1), _merge_top(g2, g3))
        zsum = jnp.zeros((PEER_HEADS, 128), F32)
        for t in top:
            zsum = zsum + jnp.exp2(t - top[0])
        c0 = top[0] + jnp.log2(zsum)
        thr_ref[:, lanes] = top[k - 1] - c0
        for key in range(PEER_KEYS):
            tile = s_ref[pl.ds(half + PEER_HEADS * key, PEER_HEADS), lanes] - c0
            s1n_ref[ch, pl.ds(key, PEER_HEADS, stride=PEER_KEYS), :] = tile
        return carry

    lax.fori_loop(0, s_ref.shape[1] // 128, chunk, 0)


def _topk(st, tm):
    R, T = st.shape
    return pl.pallas_call(
        _topk_kernel,
        grid=(T // tm,),
        in_specs=[pl.BlockSpec((R, tm), lambda i: (0, i))],
        out_specs=[pl.BlockSpec((PEER_HEADS, tm), lambda i: (0, i)),
                   pl.BlockSpec((tm // 128, R // 2, 128), lambda i: (i, 0, 0))],
        out_shape=[jax.ShapeDtypeStruct((PEER_HEADS, T), F32), jax.ShapeDtypeStruct((T // 128, R // 2, 128), F32)],
        compiler_params=_cparams("parallel"),
        name="topk",
    )(st)


def _peer_kernel(ht_ref, s0_ref, s1n_ref, thr_ref, u_ref, vt_ref, x_ref, g2_ref, o_ref,
                 pre_a, pre_b, c_a, c_b, acc_sc):
    s = pl.program_id(1)
    te, tm = c_a.shape
    nk = PEER_KEYS

    @pl.when(s == 0)
    def _():
        acc_sc[...] = jnp.zeros_like(acc_sc)
        pre_b[...] = jnp.zeros_like(pre_b)
        c_a[...] = jnp.zeros_like(c_a)

    d = ht_ref.shape[0]
    ni = te // nk
    jr = 16
    nblk = (tm // 128) * (nk // jr)
    nh = 2 if tm % 256 == 0 else 1
    th = tm // nh
    ka = mc = max(1, min(nblk // nh, d // 256))
    kq, mq = d // ka, d // mc

    def pre_piece(p, pre_w):
        half, q = divmod(p, ka)
        part = _dot(u_ref[:, q * kq:(q + 1) * kq], ht_ref[q * kq:(q + 1) * kq, half * th:(half + 1) * th])
        if q == 0:
            pre_w[:, half * th:(half + 1) * th] = part
        else:
            pre_w[:, half * th:(half + 1) * th] += part

    def value_piece(p, c_r):
        half, q = divmod(p, mc)
        acc_sc[q * mq:(q + 1) * mq, half * th:(half + 1) * th] += _dot(
            vt_ref[q * mq:(q + 1) * mq, :], c_r[:, half * th:(half + 1) * th])

    slots = {}
    for p in range(nh * ka):
        slots.setdefault(p * nblk // (nh * ka), []).append((pre_piece, p))
    for p in range(nh * mc):
        slots.setdefault((p * nblk // (nh * mc) + 1) % nblk, []).append((value_piece, p))

    def stages(pre_w, pre_r, c_w, c_r):
        for blk in range(nblk):
            for piece, p in slots.get(blk, ()):
                piece(p, pre_w if piece is pre_piece else c_r)
            ch, jq = divmod(blk, nk // jr)
            lanes = slice(ch * 128, (ch + 1) * 128)
            gates = [jnp.zeros((jr, 128), F32) for _ in range(ni)]
            for h in range(PEER_HEADS):
                s1 = s1n_ref[ch, h * nk + jq * jr:h * nk + (jq + 1) * jr, :]
                thr = thr_ref[h:h + 1, lanes]
                for ii in range(ni):
                    r0 = ii * PEER_HEADS + h
                    val = s1 + s0_ref[r0:r0 + 1, lanes]
                    gates[ii] = gates[ii] + jnp.where(val >= thr, jnp.exp2(val), 0.0)
            for ii in range(ni):
                rows = slice(ii * nk + jq * jr, ii * nk + (jq + 1) * jr)
                pre = pre_r[rows, lanes]
                act = 0.5 * pre * (1.0 + lax.erf(pre * (2.0 ** -0.5)))
                c_w[rows, lanes] = (gates[ii] * act).astype(BF16)

    @pl.when(s % 2 == 0)
    def _():
        stages(pre_a, pre_b, c_b, c_a)

    @pl.when(s % 2 == 1)
    def _():
        stages(pre_b, pre_a, c_a, c_b)

    @pl.when(s == pl.num_programs(1) - 1)
    def _():
        o_ref[...] = x_ref[...] + g2_ref[...] * acc_sc[...].T


def _peer(ht, st, thr, s1n, u, vt, x, modv, mod_row_of_tile, tm, layer):
    T, D = x.shape
    _, ne, _, te = vt.shape
    half = PEER_HEADS * PEER_KEYS
    tile = lambda s, lag: jnp.clip(s - lag, 0, ne - 1)
    return pl.pallas_call(
        _peer_kernel,
        grid=(T // tm, ne + 2),
        in_specs=[pl.BlockSpec((D, tm), lambda i, s: (0, i)),
                  pl.BlockSpec((te // PEER_KEYS * PEER_HEADS, tm), lambda i, s: (tile(s, 1), i)),
                  pl.BlockSpec((tm // 128, half, 128), lambda i, s: (i, 0, 0)),
                  pl.BlockSpec((PEER_HEADS, tm), lambda i, s: (0, i)),
                  pl.BlockSpec((None, te, D), lambda i, s: (layer, tile(s, 0), 0)),
                  pl.BlockSpec((None, None, D, te), lambda i, s: (layer, tile(s, 2), 0, 0)),
                  pl.BlockSpec((tm, D), lambda i, s: (i, 0)),
                  pl.BlockSpec((None, 1, D), lambda i, s: (mod_row_of_tile(i) + 5, 0, 0))],
        out_specs=pl.BlockSpec((tm, D), lambda i, s: (i, 0)),
        out_shape=jax.ShapeDtypeStruct((T, D), F32),
        scratch_shapes=[pltpu.VMEM((te, tm), F32), pltpu.VMEM((te, tm), F32), pltpu.VMEM((te, tm), BF16),
                        pltpu.VMEM((te, tm), BF16), pltpu.VMEM((D, tm), F32)],
        compiler_params=_cparams("parallel", "arbitrary"),
        name="peer_experts",
    )(ht, st, s1n, thr, u, vt, x, modv)


def _rope_tables(n, ident_rows):
    pos = jnp.arange(n, dtype=jnp.int32)
    row, col = pos // GRID_W, pos % GRID_W
    lane = jnp.arange(HEAD_DIM)

    def build(nfreq, width):
        inv = ROPE_THETA ** (-jnp.arange(nfreq, dtype=F32) / nfreq)
        ar = row.astype(F32)[:, None] * inv[None, :]
        ac = col.astype(F32)[:, None] * inv[None, :]
        cos = jnp.concatenate([jnp.cos(ar), jnp.cos(ar), jnp.cos(ac), jnp.cos(ac)], axis=-1)
        sin = jnp.concatenate([jnp.sin(ar), jnp.sin(ar), jnp.sin(ac), jnp.sin(ac)], axis=-1)
        pad = HEAD_DIM - width
        cos = jnp.pad(cos, ((0, 0), (0, pad)))
        sin = jnp.pad(sin, ((0, 0), (0, pad)))
        first = (lane % (2 * nfreq)) < nfreq
        tab = jnp.stack([cos, jnp.where(first, 0.0, sin), jnp.where(first, -sin, 0.0)])
        ident = jnp.stack([jnp.ones((ident_rows, HEAD_DIM), F32), jnp.zeros((ident_rows, HEAD_DIM), F32),
                           jnp.zeros((ident_rows, HEAD_DIM), F32)])
        return jnp.concatenate([ident, tab], axis=1)

    return build(HEAD_DIM // 4, HEAD_DIM), build(MLA_ROPE // 4, MLA_ROPE)


def kernel(x_prompt, x_sample, c, cache_gqa_k, cache_gqa_v, cache_mla_ckv, cache_mla_krope, state_hgrn, c_ctx,
           w_mod, b_mod, norm1_g, norm2_g, w_in, gqa_qn, gqa_kn, mla_qa_n, mla_kva_n, w_mla_qb, w_mla_kvb,
           mla_qn, mla_kn, hg_lb_logits, hg_on, w_out, peer_wq, peer_keys, peer_u, peer_v):
    B, S, D = x_prompt.shape
    DB, DS, _ = x_sample.shape
    L = w_mod.shape[0]
    P = cache_gqa_k.shape[2]
    TC, TL = B * S, DB * DS
    T = TC + TL
    tm = 512 if (TC % 512 == 0 and DS % 512 == 0) else 128
    tm_in = 1024 if (TC % 1024 == 0 and DS % 1024 == 0) else tm
    tm_small = min(256, tm)

    ngroups = 1 + DB

    def mod_row_of(t, layer):
        def f(i):
            group = jnp.where(i < TC // t, 0, 1 + (i - TC // t) // (DS // t))
            return (layer * ngroups + group) * 6
        return f

    def table_of(t):
        def f(i):
            return jnp.where(i < TC // t, 0, 1 + (i - TC // t) % (DS // t))
        return f

    cond8 = jnp.zeros((8, D), F32).at[0].set(c_ctx).at[1:1 + DB].set(c)
    modv = _modulation(cond8, w_mod, b_mod)[:, :ngroups].reshape(L * ngroups * 6, 1, D)

    lb_cum = jnp.cumsum(jax.nn.softmax(hg_lb_logits.astype(F32), axis=0), axis=0)
    lbs = jnp.maximum(lb_cum - lb_cum[:1], 0.0)
    lbtab = jnp.stack([jnp.log(lbs), jnp.log1p(-lbs), 1.0 - lbs] + [jnp.zeros_like(lbs)] * 5, axis=2)

    sizes = np.cumsum([0, 768, 256, 256, 512, 512, 512, 512, 512, 512, 512, 64])
    part = lambda a, k: a[..., sizes[k]:sizes[k + 1]]
    order = (0, 1, 2, 10, 8, 9, 3, 4, 5, 6, 7)
    cols = []
    for k in order:
        cols.append(part(w_in, k))
        if k == 10:
            cols.append(jnp.zeros((L, D, 256 - MLA_ROPE), w_in.dtype))
    w_in_b = jnp.concatenate(cols, axis=-1).astype(BF16)
    w_in_b = jnp.swapaxes(w_in_b.reshape(L, D, -1, IN_TILE), 1, 2)

    wqb = w_mla_qb.reshape(L, MLA_RANK, MLA_HEADS, MLA_QK)
    wqb = jnp.pad(wqb, ((0, 0), (0, 0), (0, 0), (0, MLA_PAD - MLA_QK))).reshape(L, MLA_RANK, MLA_HEADS * MLA_PAD).astype(BF16)
    wkvb = w_mla_kvb.reshape(L, MLA_RANK, MLA_HEADS, 2, HEAD_DIM)
    wkvb = jnp.moveaxis(wkvb, 3, 2).reshape(L, MLA_RANK, 2 * MLA_HEADS * HEAD_DIM).astype(BF16)
    rpad = lambda g: jnp.pad(g[:, HEAD_DIM:], ((0, 0), (0, HEAD_DIM - MLA_ROPE)))[:, None, :]
    mqn_n, mqn_r = mla_qn[:, None, :HEAD_DIM], rpad(mla_qn)
    mkn_n, mkn_r = mla_kn[:, None, :HEAD_DIM], rpad(mla_kn)
    w_out_b = w_out.astype(BF16)
    u_b = peer_u.astype(BF16)
    vt_b = jnp.swapaxes(peer_v.reshape(L, -1, PEER_EXPERT_TILE, D), 2, 3).astype(BF16)
    def key_major(w):
        return jnp.swapaxes(w.reshape(L, 2, PEER_HEADS, PEER_KEYS, D), 2, 3).reshape(L, 2 * PEER_HEADS * PEER_KEYS, D)

    ws_hi, ws_lo = map(key_major, _score_weights(peer_keys, peer_wq))

    tabg, tabm = _rope_tables(DS, tm_small)

    cck = jnp.moveaxis(cache_mla_ckv, 1, 0).reshape(L, DB * P, MLA_RANK)
    ckr = jnp.pad(jnp.moveaxis(cache_mla_krope, 1, 0).reshape(L, DB * P, MLA_ROPE), ((0, 0), (0, 0), (0, HEAD_DIM - MLA_ROPE)))
    cmk, cmv = _cache_expand(cck, ckr, wkvb, mkn_n, mkn_r)
    cgk = jnp.moveaxis(cache_gqa_k, 1, 0).reshape(L, DB * P, -1).astype(BF16)
    cgv = jnp.transpose(cache_gqa_v.reshape(DB, L, P, -1), (1, 3, 0, 2)).reshape(L, -1, DB * P).astype(BF16)
    s0_ctx = jnp.zeros((B, 2, HG_HEADS, HEAD_DIM, HEAD_DIM), F32)

    x = jnp.concatenate([x_prompt.reshape(TC, D), x_sample.reshape(TL, D)], axis=0)
    kvw = GQA_KV_HEADS * HEAD_DIM
    caches = tuple(jnp.zeros((B, L, S, w), F32) for w in (kvw, kvw, MLA_RANK, MLA_ROPE))
    states = jnp.zeros((B, L, 2, HG_HEADS, HEAD_DIM, HEAD_DIM), F32)
    ga = jnp.zeros((T, GQA_HEADS * HEAD_DIM), BF16)
    ma = jnp.zeros((T, MLA_HEADS * HEAD_DIM), BF16)
    o2 = jnp.zeros((2, T, HG_HEADS * HEAD_DIM), F32)
    tq_c = min(256, S)
    tq_l, tk_l = min(1024, DS), min(512, P, DS)
    for l in range(L):
        z = _in_proj(x, norm1_g[l][None], modv, w_in_b, mod_row_of(tm_in, l), tm_in, l)
        gq, gkr, gvt, mq, mk, mvt, *caches = _prep(
            z, tabg, tabm, table_of(tm_small), tm_small, gqa_qn[l][None], gqa_kn[l][None], mla_qa_n[l][None],
            mla_kva_n[l][None], wqb[l], wkvb[l], mqn_n[l], mqn_r[l], mkn_n[l], mkn_r[l], caches, l)

        gqa = dict(heads=GQA_HEADS, group=GQA_HEADS // GQA_KV_HEADS, dq=HEAD_DIM, dv=HEAD_DIM, hb=GQA_HEADS // GQA_KV_HEADS)
        mla = dict(heads=MLA_HEADS, group=1, dq=MLA_PAD, dv=HEAD_DIM, hb=MLA_HEADS // 2)
        ctx_a = dict(batch=B, nq=S, nk=S, row0=0, tq=tq_c, tk=tq_c)
        lat_a = dict(batch=DB, nq=DS, nk=DS, row0=TC, tq=tq_l, tk=tk_l)
        ga = _attention(gq, gkr, gvt, **dict(gqa, hb=GQA_HEADS), **ctx_a, prev=ga)
        ga = _attention(gq, gkr, gvt, **gqa, **lat_a, prev=ga, cache=(cgk[l], cgv[l]))
        ma = _attention(mq, mk, mvt, **dict(mla, hb=MLA_HEADS), **ctx_a, prev=ma)
        ma = _attention(mq, mk, mvt, **mla, **lat_a, prev=ma, cache=(cmk[l], cmv[l]))

        o2, states = _hgrn(z, lbtab[l], s0_ctx, batch=B, n=S, row0=0, prev=o2, states=states, layer=l)
        o2, _ = _hgrn(z, lbtab[l], state_hgrn[:, l].astype(F32), batch=DB, n=DS, row0=TC, prev=o2)

        x, hhi, hlo, ht = _mixer_out(x, ga, o2, z, ma, w_out_b, hg_on[l][None], norm2_g[l][None], modv,
                                 mod_row_of(tm_small, l), tm_small, l)
        st = _scores(ws_hi, ws_lo, hhi, hlo, tm, l)
        thr, s1n = _topk(st, tm)
        x = _peer(ht, st, thr, s1n, u_b, vt_b, x, modv, mod_row_of(tm, l), tm, l)

    new_k, new_v, new_ckv, new_kr = caches
    return (x[:TC].reshape(B, S, D), x[TC:].reshape(DB, DS, D),
            new_k.reshape(B, L, S, GQA_KV_HEADS, HEAD_DIM), new_v.reshape(B, L, S, GQA_KV_HEADS, HEAD_DIM),
            new_ckv, new_kr, states)
```

```python
import functools

import numpy as np
import jax
import jax.numpy as jnp
from jax import lax
from jax.experimental import pallas as pl
from jax.experimental.pallas import tpu as pltpu

F32 = jnp.float32
BF16 = jnp.bfloat16

EPS = 1e-6
LOG2E = 1.4426950408889634
ROPE_THETA = 10000.0
GRID_W = 64

HEAD_DIM = 128
GQA_HEADS, GQA_KV_HEADS = 6, 2
HG_HEADS = 4
MLA_HEADS, MLA_ROPE, MLA_RANK = 6, 64, 512
MLA_QK = HEAD_DIM + MLA_ROPE
MLA_PAD = 2 * HEAD_DIM
PEER_HEADS, PEER_KEYS, PEER_TOPK = 8, 128, 16
HG_CHUNK = 128
PEER_EXPERT_TILE = 512
IN_TILE = 1024

VMEM_LIMIT = 60 * 1024 * 1024


def _cparams(*sem):
    return pltpu.CompilerParams(dimension_semantics=sem, vmem_limit_bytes=VMEM_LIMIT)


def _dot(a, b):
    return jnp.dot(a, b, preferred_element_type=F32)


def _dot_nt(a, b):
    return lax.dot_general(a, b, (((1,), (1,)), ((), ())), preferred_element_type=F32)


def _dot_tn(a, b):
    return lax.dot_general(a, b, (((0,), (0,)), ((), ())), preferred_element_type=F32)


def _split2(a):
    hi = a.astype(BF16)
    lo = (a - hi.astype(F32)).astype(BF16)
    return hi, lo


def _split3(a):
    p1 = a.astype(BF16)
    r = a - p1.astype(F32)
    p2 = r.astype(BF16)
    p3 = (r - p2.astype(F32)).astype(BF16)
    return p1, p2, p3


def _rms(x, n):
    return x * lax.rsqrt(jnp.sum(x * x, axis=-1, keepdims=True) * (1.0 / n) + EPS)


def _mod_kernel(cond_ref, w_ref, b_ref, o_ref):
    c = cond_ref[...]
    s = c * (1.0 / (1.0 + jnp.exp(-c)))
    shi, slo = _split2(s)
    whi, wlo = _split2(w_ref[...])
    o_ref[...] = _dot(shi, whi) + _dot(slo, whi) + _dot(shi, wlo) + b_ref[...]


def _modulation(cond8, w_mod, b_mod):
    L, D, N = w_mod.shape
    tn = min(512, N)
    return pl.pallas_call(
        _mod_kernel,
        grid=(L, N // tn),
        in_specs=[pl.BlockSpec((8, D), lambda l, j: (0, 0)),
                  pl.BlockSpec((None, D, tn), lambda l, j: (l, 0, j)),
                  pl.BlockSpec((None, 1, tn), lambda l, j: (l, 0, j))],
        out_specs=pl.BlockSpec((None, 8, tn), lambda l, j: (l, 0, j)),
        out_shape=jax.ShapeDtypeStruct((L, 8, N), F32),
        compiler_params=_cparams("parallel", "parallel"),
        name="modulation",
    )(cond8, w_mod, b_mod.reshape(L, 1, N))


def _in_kernel(x_ref, g_ref, sh_ref, sc_ref, w_ref, o_ref, h_sc):
    @pl.when(pl.program_id(1) == 0)
    def _():
        x = x_ref[...]
        h = _rms(x, x.shape[-1]) * g_ref[...] * (1.0 + sc_ref[...]) + sh_ref[...]
        h_sc[...] = h.astype(BF16)

    o_ref[...] = _dot(h_sc[...], w_ref[...])


def _in_proj(x, g, modv, w, mod_row_of_tile, tm, layer):
    T, D = x.shape
    _, nj, _, tn = w.shape
    N = nj * tn

    def mspec(k):
        return pl.BlockSpec((None, 1, D), lambda i, j: (mod_row_of_tile(i) + k, 0, 0))

    return pl.pallas_call(
        _in_kernel,
        grid=(T // tm, nj),
        in_specs=[pl.BlockSpec((tm, D), lambda i, j: (i, 0)),
                  pl.BlockSpec((1, D), lambda i, j: (0, 0)),
                  mspec(0), mspec(1),
                  pl.BlockSpec((None, None, D, tn), lambda i, j: (layer, j, 0, 0))],
        out_specs=pl.BlockSpec((tm, tn), lambda i, j: (i, j)),
        out_shape=jax.ShapeDtypeStruct((T, N), F32),
        scratch_shapes=[pltpu.VMEM((tm, D), BF16)],
        compiler_params=_cparams("parallel", "arbitrary"),
        name="in_proj",
    )(x, g, modv, modv, w)


def _rope(x, tab_ref, blk):
    return (x * tab_ref[0] + pltpu.roll(x, blk, 1) * tab_ref[1]
            + pltpu.roll(x, HEAD_DIM - blk, 1) * tab_ref[2])


def _mla_kv(ckv, krope, wkvb_ref, gkn_ref, gkr_ref, tab_ref, mk_ref, mv_ref):
    kv = _dot(ckv.astype(BF16), wkvb_ref[...])
    nv = MLA_HEADS * HEAD_DIM
    mv_ref[...] = kv[:, nv:].T.astype(BF16)
    ssr = jnp.sum(krope * krope, axis=-1, keepdims=True)
    for h in range(MLA_HEADS):
        kn = kv[:, h * HEAD_DIM:(h + 1) * HEAD_DIM]
        rs = lax.rsqrt((jnp.sum(kn * kn, axis=-1, keepdims=True) + ssr) * (1.0 / MLA_QK) + EPS)
        kr = krope * rs * gkr_ref[...]
        if tab_ref is not None:
            kr = _rope(kr, tab_ref, MLA_ROPE // 4)
        mk_ref[:, h * MLA_PAD:h * MLA_PAD + HEAD_DIM] = (kn * rs * gkn_ref[...]).astype(BF16)
        mk_ref[:, h * MLA_PAD + HEAD_DIM:(h + 1) * MLA_PAD] = kr.astype(BF16)


def _prep_kernel(gq_ref, gk_ref, gv_ref, mkr_ref, mqa_ref, mkva_ref, tg_ref, tm_ref,
                 gqn_ref, gkn_ref, qan_ref, kvan_ref, wqb_ref, wkvb_ref,
                 mqn_n_ref, mqn_r_ref, mkn_n_ref, mkn_r_ref, *rest, n_ctx_tiles):
    gq_o, gkr_o, gvt_o, mq_o, mk_o, mv_o, nk_o, nv_o, nckv_o, nkr_o = rest[4:]
    is_ctx = pl.program_id(0) < n_ctx_tiles
    gscale = HEAD_DIM ** -0.5 * LOG2E
    for h in range(GQA_HEADS):
        sl = slice(h * HEAD_DIM, (h + 1) * HEAD_DIM)
        q = _rms(gq_ref[:, sl], HEAD_DIM) * gqn_ref[...]
        gq_o[:, sl] = (_rope(q, tg_ref, HEAD_DIM // 4) * gscale).astype(BF16)
    for h in range(GQA_KV_HEADS):
        sl = slice(h * HEAD_DIM, (h + 1) * HEAD_DIM)
        k = _rms(gk_ref[:, sl], HEAD_DIM) * gkn_ref[...]
        gkr_o[:, sl] = _rope(k, tg_ref, HEAD_DIM // 4).astype(BF16)

        @pl.when(is_ctx)
        def _():
            nk_o[:, sl] = k
    gvt_o[...] = gv_ref[...].T.astype(BF16)

    qa = _rms(mqa_ref[...], MLA_RANK) * qan_ref[...]
    mq = _dot(qa.astype(BF16), wqb_ref[...])
    mscale = MLA_QK ** -0.5 * LOG2E
    for h in range(MLA_HEADS):
        qn = mq[:, h * MLA_PAD:h * MLA_PAD + HEAD_DIM]
        qr = mq[:, h * MLA_PAD + HEAD_DIM:(h + 1) * MLA_PAD]
        ss = jnp.sum(qn * qn, axis=-1, keepdims=True) + jnp.sum(qr * qr, axis=-1, keepdims=True)
        rs = lax.rsqrt(ss * (1.0 / MLA_QK) + EPS) * mscale
        mq_o[:, h * MLA_PAD:h * MLA_PAD + HEAD_DIM] = (qn * rs * mqn_n_ref[...]).astype(BF16)
        mq_o[:, h * MLA_PAD + HEAD_DIM:(h + 1) * MLA_PAD] = _rope(qr * rs * mqn_r_ref[...], tm_ref, MLA_ROPE // 4).astype(BF16)

    ckv = _rms(mkva_ref[...], MLA_RANK) * kvan_ref[...]

    @pl.when(is_ctx)
    def _():
        nv_o[...] = gv_ref[...]
        nckv_o[...] = ckv
        nkr_o[...] = mkr_ref[:, :MLA_ROPE]

    _mla_kv(ckv, mkr_ref[...], wkvb_ref, mkn_n_ref, mkn_r_ref, tm_ref, mk_o, mv_o)


def _prep(z, tabg, tabm, table_of_tile, tm, gqn, gkn, qan, kvan, wqb, wkvb, mqn_n, mqn_r, mkn_n, mkn_r, caches, layer):
    T = z.shape[0]
    B, _, S, _ = caches[0].shape
    per_seq = S // tm
    n_ctx_tiles = B * per_seq
    row = lambda w: pl.BlockSpec((1, w), lambda i: (0, 0))
    full = lambda a: pl.BlockSpec(a.shape, lambda i: (0, 0))
    tab = pl.BlockSpec((3, tm, HEAD_DIM), lambda i: (0, table_of_tile(i), 0))
    zspec = lambda w, blk: pl.BlockSpec((tm, w), lambda i: (i, blk))
    widths = (GQA_HEADS * HEAD_DIM, GQA_KV_HEADS * HEAD_DIM, GQA_KV_HEADS * HEAD_DIM,
              MLA_HEADS * MLA_PAD, MLA_HEADS * MLA_PAD, MLA_HEADS * HEAD_DIM)
    transposed = (False, False, True, False, False, True)
    ospec = lambda w, t: pl.BlockSpec((w, tm), lambda i: (0, i)) if t else pl.BlockSpec((tm, w), lambda i: (i, 0))

    def cspec(c):
        def idx(i):
            t = jnp.minimum(i, n_ctx_tiles - 1)
            return (t // per_seq, layer, t % per_seq, 0)
        return pl.BlockSpec((None, None, tm, c.shape[-1]), idx)

    n_in = 18
    return pl.pallas_call(
        functools.partial(_prep_kernel, n_ctx_tiles=n_ctx_tiles),
        grid=(T // tm,),
        in_specs=[zspec(768, 0), zspec(256, 3), zspec(256, 4), zspec(128, 10), zspec(512, 3), zspec(512, 4),
                  tab, tab, row(128), row(128), row(512), row(512), full(wqb), full(wkvb),
                  row(128), row(128), row(128), row(128)] + [pl.BlockSpec(memory_space=pl.ANY)] * 4,
        out_specs=[ospec(w, t) for w, t in zip(widths, transposed)] + [cspec(c) for c in caches],
        out_shape=[jax.ShapeDtypeStruct((w, T) if t else (T, w), BF16) for w, t in zip(widths, transposed)]
        + [jax.ShapeDtypeStruct(c.shape, c.dtype) for c in caches],
        input_output_aliases={n_in + k: len(widths) + k for k in range(4)},
        compiler_params=_cparams("arbitrary"),
        name="prep",
    )(z, z, z, z, z, z, tabg, tabm, gqn, gkn, qan, kvan, wqb, wkvb, mqn_n, mqn_r, mkn_n, mkn_r, *caches)


def _cache_kernel(ckv_ref, kr_ref, wkvb_ref, gkn_ref, gkr_ref, mk_o, mv_o):
    _mla_kv(ckv_ref[...], kr_ref[...], wkvb_ref, gkn_ref, gkr_ref, None, mk_o, mv_o)


def _cache_expand(ckv, kr, wkvb, mkn_n, mkn_r):
    L, R, _ = ckv.shape
    tm = min(512, R)
    return pl.pallas_call(
        _cache_kernel,
        grid=(L, R // tm),
        in_specs=[pl.BlockSpec((None, tm, MLA_RANK), lambda l, i: (l, i, 0)),
                  pl.BlockSpec((None, tm, HEAD_DIM), lambda l, i: (l, i, 0)),
                  pl.BlockSpec((None,) + wkvb.shape[1:], lambda l, i: (l, 0, 0)),
                  pl.BlockSpec((None, 1, HEAD_DIM), lambda l, i: (l, 0, 0)),
                  pl.BlockSpec((None, 1, HEAD_DIM), lambda l, i: (l, 0, 0))],
        out_specs=[pl.BlockSpec((None, tm, MLA_HEADS * MLA_PAD), lambda l, i: (l, i, 0)),
                   pl.BlockSpec((None, MLA_HEADS * HEAD_DIM, tm), lambda l, i: (l, 0, i))],
        out_shape=[jax.ShapeDtypeStruct((L, R, MLA_HEADS * MLA_PAD), BF16),
                   jax.ShapeDtypeStruct((L, MLA_HEADS * HEAD_DIM, R), BF16)],
        compiler_params=_cparams("parallel", "parallel"),
        name="cache_expand",
    )(ckv, kr, wkvb, mkn_n, mkn_r)


def _attn_kernel(*refs, heads, group, dq, dv, ncb):
    n_in = (5 if ncb else 3) + 1
    q_ref = refs[0]
    o_ref, m_sc, l_sc, acc_sc = refs[n_in:]
    ki = pl.program_id(3)

    @pl.when(ki == 0)
    def _():
        m_sc[...] = jnp.full_like(m_sc, -jnp.inf)
        l_sc[...] = jnp.zeros_like(l_sc)
        acc_sc[...] = jnp.zeros_like(acc_sc)

    def step(k_ref, vt_ref):
        def scores(h):
            kv = h // group
            return _dot_nt(k_ref[:, kv * dq:(kv + 1) * dq], q_ref[:, h * dq:(h + 1) * dq])

        s_next = scores(0)
        for h in range(heads):
            kv = h // group
            s = s_next
            if h + 1 < heads:
                s_next = scores(h + 1)
            m_prev = m_sc[h]
            m_new = jnp.maximum(m_prev, jnp.max(s, axis=0, keepdims=True))
            p = jnp.exp2(s - m_new)
            alpha = jnp.exp2(m_prev - m_new)
            l_sc[h] = alpha * l_sc[h] + jnp.sum(p, axis=0, keepdims=True)
            acc_sc[h] = alpha * acc_sc[h] + _dot(vt_ref[kv * dv:(kv + 1) * dv, :], p.astype(BF16))
            m_sc[h] = m_new

    if ncb:
        pl.when(ki < ncb)(lambda: step(refs[1], refs[2]))
        pl.when(ki >= ncb)(lambda: step(refs[3], refs[4]))
    else:
        step(refs[1], refs[2])

    @pl.when(ki == pl.num_programs(3) - 1)
    def _():
        for h in range(heads):
            o_ref[:, h * dv:(h + 1) * dv] = (acc_sc[h] / l_sc[h]).T.astype(o_ref.dtype)


def _attention(q, k, vt, *, batch, nq, nk, row0, heads, group, dq, dv, tq, tk, hb, prev, cache=None):
    T = q.shape[0]
    nqb, nnb = nq // tq, nk // tk
    ncb = 0 if cache is None else cache[0].shape[0] // batch // tk
    qb0, kb0 = row0 // tq, row0 // tk
    kvb = max(1, hb // group)
    new_blk = lambda b, j: kb0 + b * nnb + jnp.maximum(j - ncb, 0)
    in_specs = [pl.BlockSpec((tq, hb * dq), lambda b, h, i, j: (qb0 + b * nqb + i, h))]
    operands = [q]
    if ncb:
        cache_blk = lambda b, j: b * ncb + jnp.minimum(j, ncb - 1)
        in_specs += [pl.BlockSpec((tk, kvb * dq), lambda b, h, i, j: (cache_blk(b, j), h)),
                     pl.BlockSpec((kvb * dv, tk), lambda b, h, i, j: (h, cache_blk(b, j)))]
        operands += list(cache)
    in_specs += [pl.BlockSpec((tk, kvb * dq), lambda b, h, i, j: (new_blk(b, j), h)),
                 pl.BlockSpec((kvb * dv, tk), lambda b, h, i, j: (h, new_blk(b, j)))]
    operands += [k, vt]
    in_specs.append(pl.BlockSpec(memory_space=pl.ANY))
    aliases = {len(operands): 0}
    operands.append(prev)
    return pl.pallas_call(
        functools.partial(_attn_kernel, heads=hb, group=group, dq=dq, dv=dv, ncb=ncb),
        grid=(batch, heads // hb, nqb, ncb + nnb),
        in_specs=in_specs,
        out_specs=pl.BlockSpec((tq, hb * dv), lambda b, h, i, j: (qb0 + b * nqb + i, h)),
        out_shape=jax.ShapeDtypeStruct((T, heads * dv), BF16),
        scratch_shapes=[pltpu.VMEM((hb, 1, tq), F32), pltpu.VMEM((hb, 1, tq), F32), pltpu.VMEM((hb, dv, tq), F32)],
        input_output_aliases=aliases,
        compiler_params=_cparams("parallel", "parallel", "parallel", "arbitrary"),
        name="attention",
    )(*operands)


def _hgrn_constants(c):
    levels = int(np.log2(c))
    tri = np.zeros((2, c, c), np.float32)
    M = np.zeros((2, levels, c, c), np.float32)
    tri[0] = np.tril(np.ones((c, c), np.float32))
    for lv in range(levels):
        m = c >> (lv + 1)
        for a in range(0, c, 2 * m):
            M[0, lv, a + m:a + 2 * m, a:a + m] = 1.0
    tri[1] = tri[0][::-1, ::-1]
    M[1] = M[0][:, ::-1, ::-1]
    return tri, M


def _level_exponents(bq, bwd, c):
    rowid = lax.broadcasted_iota(jnp.int32, bq.shape, 0)
    out = []
    m = c // 2
    while m >= 1:
        if 2 * m >= 8:
            blocks = [jnp.broadcast_to(jnp.where(bwd, bq[a + m:a + m + 1, :], bq[a + m - 1:a + m, :]), (2 * m, bq.shape[1]))
                      for a in range(0, c, 2 * m)]
            ref = blocks[0] if len(blocks) == 1 else jnp.concatenate(blocks, axis=0)
        else:
            target = m - 1 + bwd.astype(jnp.int32)
            rowmod = rowid % (2 * m)
            ref = bq
            for delta in range(-m, m + 1):
                if delta != 0:
                    ref = jnp.where(rowmod + delta == target, pltpu.roll(bq, (-delta) % c, 0), ref)
        out.append(-jnp.abs(bq - ref))
        m //= 2
    return out


def _hgrn_kernel(q_ref, z_ref, v_ref, lb_ref, s0_ref, L_ref, M_ref, o_ref, sf_ref, s_sc):
    c = q_ref.shape[0]
    levels = M_ref.shape[0]

    @pl.when(pl.program_id(2) == 0)
    def _():
        s_sc[...] = s0_ref[...]

    z = z_ref[...]
    e = jnp.exp(-jnp.abs(z))
    r = 1.0 / (1.0 + e)
    logsig = jnp.minimum(z, 0.0) + jnp.log(r)
    a = lb_ref[0:1, :]
    cc = lb_ref[1:2, :] + logsig
    logf = jnp.maximum(a, cc) + jnp.log(1.0 + jnp.exp(-jnp.abs(a - cc)))
    kk = lb_ref[2:3, :] * jnp.where(z >= 0.0, e * r, r)

    tri = L_ref[...]
    p1, p2, p3 = _split3(logf)
    bq = _dot(tri, p1) + _dot(tri, p2) + _dot(tri, p3)
    tot = jnp.minimum(bq[0:1, :], bq[c - 1:c, :])
    ex_q = jnp.exp(bq)
    ex_k = jnp.exp(tot - bq)
    ex_lv = [jnp.exp(t) for t in _level_exponents(bq, pl.program_id(0) == 1, c)]
    ones = jnp.ones((c, HEAD_DIM), BF16)
    for h in range(HG_HEADS):
        sl = slice(h * HEAD_DIM, (h + 1) * HEAD_DIM)
        q, k, v = q_ref[:, sl], kk[:, sl], v_ref[:, sl]
        vb = v.astype(BF16)
        amat = jnp.zeros((c, c), F32)
        for lv in range(levels):
            el = ex_lv[lv][:, sl]
            amat = amat + M_ref[lv] * _dot_nt((q * el).astype(BF16), (k * el).astype(BF16))
        s_prev = s_sc[h]
        o = (_dot(amat.astype(BF16), vb) + jnp.sum(q * k, axis=-1, keepdims=True) * v
             + _dot((q * ex_q[:, sl]).astype(BF16), s_prev.astype(BF16)))
        o_ref[:, sl] = o
        dcol = jnp.exp(_dot_tn(p1[:, sl], ones) + _dot_tn(p2[:, sl], ones) + _dot_tn(p3[:, sl], ones))
        s_new = dcol * s_prev + _dot_tn((k * ex_k[:, sl]).astype(BF16), vb)
        s_sc[h] = s_new
        sf_ref[h] = s_new


def _hgrn(z, lbtab, s0, *, batch, n, row0, prev, states=None, layer=0):
    c = min(HG_CHUNK, n)
    nc = n // c
    cb0 = row0 // c
    Lm, Mm = _hgrn_constants(c)
    Lm, Mm = jnp.asarray(Lm, BF16), jnp.asarray(Mm, F32)
    w = HG_HEADS * HEAD_DIM

    def rows(d, b, j):
        return cb0 + b * nc + j + d * (nc - 1 - 2 * j)

    sdims = (HG_HEADS, HEAD_DIM, HEAD_DIM)
    extra, aliases = [prev], {7: 0}
    if states is None:
        s_spec = pl.BlockSpec((None, None) + sdims, lambda d, b, j: (b, d, 0, 0, 0))
        s_shape = jax.ShapeDtypeStruct((batch, 2) + sdims, F32)
    else:
        s_spec = pl.BlockSpec((None, None, None) + sdims, lambda d, b, j: (b, layer, d, 0, 0, 0))
        s_shape = jax.ShapeDtypeStruct(states.shape, F32)
        extra, aliases = [prev, states], {7: 0, 8: 1}
    n_in = 7
    return pl.pallas_call(
        lambda *refs: _hgrn_kernel(*refs[:n_in], *refs[n_in + len(extra):]),
        grid=(2, batch, nc),
        in_specs=[pl.BlockSpec((c, w), lambda d, b, j: (rows(d, b, j), 5)),
                  pl.BlockSpec((c, w), lambda d, b, j: (rows(d, b, j), 6 + d)),
                  pl.BlockSpec((c, w), lambda d, b, j: (rows(d, b, j), 8)),
                  pl.BlockSpec((None, 8, w), lambda d, b, j: (d, 0, 0)),
                  pl.BlockSpec((None, None, HG_HEADS, HEAD_DIM, HEAD_DIM), lambda d, b, j: (b, d, 0, 0, 0)),
                  pl.BlockSpec((None,) + Lm.shape[1:], lambda d, b, j: (d, 0, 0)),
                  pl.BlockSpec((None,) + Mm.shape[1:], lambda d, b, j: (d, 0, 0, 0))]
        + [pl.BlockSpec(memory_space=pl.ANY)] * len(extra),
        out_specs=[pl.BlockSpec((None, c, w), lambda d, b, j: (d, rows(d, b, j), 0)), s_spec],
        out_shape=[jax.ShapeDtypeStruct((2, z.shape[0], w), F32), s_shape],
        scratch_shapes=[pltpu.VMEM((HG_HEADS, HEAD_DIM, HEAD_DIM), F32)],
        input_output_aliases=aliases,
        compiler_params=_cparams("parallel", "parallel", "arbitrary"),
        name="hgrn_scan",
    )(z, z, z, lbtab, s0, Lm, Mm, *extra)


def _out_kernel(x_ref, ga_ref, of_ref, ob_ref, hg_ref, ma_ref, w_ref, on_ref, g2_ref,
                g1_ref, sh2_ref, sc2_ref, xo_ref, hhi_ref, hlo_ref, ht_ref):
    o = of_ref[...] + ob_ref[...]
    g = hg_ref[...]
    g = g * (1.0 / (1.0 + jnp.exp(-g)))
    n_g, n_h = GQA_HEADS * HEAD_DIM, HG_HEADS * HEAD_DIM
    mix = _dot(ga_ref[...], w_ref[0:n_g, :]) + _dot(ma_ref[...], w_ref[n_g + n_h:, :])
    for h in range(HG_HEADS):
        sl = slice(h * HEAD_DIM, (h + 1) * HEAD_DIM)
        oh = _rms(o[:, sl], HEAD_DIM) * on_ref[...] * g[:, sl]
        mix = mix + _dot(oh.astype(BF16), w_ref[n_g + h * HEAD_DIM:n_g + (h + 1) * HEAD_DIM, :])
    x = x_ref[...] + g1_ref[...] * mix
    xo_ref[...] = x
    h2 = _rms(x, x.shape[-1]) * g2_ref[...] * (1.0 + sc2_ref[...]) + sh2_ref[...]
    hi, lo = _split2(h2)
    hhi_ref[...] = hi
    hlo_ref[...] = lo
    ht_ref[...] = h2.T.astype(BF16)


def _mixer_out(x, ga, o2, z, ma, w, on, g2, modv, mod_row_of_tile, tm, layer):
    T, D = x.shape
    wh = HG_HEADS * HEAD_DIM

    def mspec(k):
        return pl.BlockSpec((None, 1, D), lambda i: (mod_row_of_tile(i) + k, 0, 0))

    return pl.pallas_call(
        _out_kernel,
        grid=(T // tm,),
        in_specs=[pl.BlockSpec((tm, D), lambda i: (i, 0)),
                  pl.BlockSpec((tm, ga.shape[1]), lambda i: (i, 0)),
                  pl.BlockSpec((None, tm, wh), lambda i: (0, i, 0)),
                  pl.BlockSpec((None, tm, wh), lambda i: (1, i, 0)),
                  pl.BlockSpec((tm, wh), lambda i: (i, 9)),
                  pl.BlockSpec((tm, ma.shape[1]), lambda i: (i, 0)),
                  pl.BlockSpec((None,) + w.shape[1:], lambda i: (layer, 0, 0)),
                  pl.BlockSpec((1, HEAD_DIM), lambda i: (0, 0)),
                  pl.BlockSpec((1, D), lambda i: (0, 0)),
                  mspec(2), mspec(3), mspec(4)],
        out_specs=[pl.BlockSpec((tm, D), lambda i: (i, 0))] * 3 + [pl.BlockSpec((D, tm), lambda i: (0, i))],
        out_shape=[jax.ShapeDtypeStruct((T, D), F32), jax.ShapeDtypeStruct((T, D), BF16),
                   jax.ShapeDtypeStruct((T, D), BF16), jax.ShapeDtypeStruct((D, T), BF16)],
        compiler_params=_cparams("parallel"),
        name="mixer_out",
    )(x, ga, o2, o2, z, ma, w, on, g2, modv, modv, modv)


def _ws_kernel(k_ref, wq_ref, hi_ref, lo_ref):
    khi, klo = _split2(k_ref[...])
    whi, wlo = _split2(wq_ref[...])
    acc = _dot_nt(khi, whi) + _dot_nt(klo, whi) + _dot_nt(khi, wlo)
    hi, lo = _split2(acc * LOG2E)
    hi_ref[...] = hi
    lo_ref[...] = lo


def _score_weights(keys, wq):
    L, D, _ = wq.shape
    nb = 2 * PEER_HEADS
    sub = keys.shape[-1]
    kk = keys.reshape(L, nb, PEER_KEYS, sub)
    out = jax.ShapeDtypeStruct((L, nb * PEER_KEYS, D), BF16)
    ospec = pl.BlockSpec((None, PEER_KEYS, D), lambda l, b: (l, (b % 2) * PEER_HEADS + b // 2, 0))
    return pl.pallas_call(
        _ws_kernel,
        grid=(L, nb),
        in_specs=[pl.BlockSpec((None, None, PEER_KEYS, sub), lambda l, b: (l, b, 0, 0)),
                  pl.BlockSpec((None, D, sub), lambda l, b: (l, 0, b))],
        out_specs=[ospec, ospec],
        out_shape=[out, out],
        compiler_params=_cparams("parallel", "parallel"),
        name="score_weights",
    )(kk, wq)


def _score_kernel(whi_ref, wlo_ref, hhi_ref, hlo_ref, o_ref):
    whi, hhi = whi_ref[...], hhi_ref[...]
    o_ref[...] = _dot_nt(whi, hhi) + _dot_nt(whi, hlo_ref[...]) + _dot_nt(wlo_ref[...], hhi)


def _scores(whi, wlo, hhi, hlo, tm, layer):
    _, R, D = whi.shape
    T = hhi.shape[0]
    tr = min(512, R)
    return pl.pallas_call(
        _score_kernel,
        grid=(R // tr, T // tm),
        in_specs=[pl.BlockSpec((None, tr, D), lambda r, i: (layer, r, 0)), pl.BlockSpec((None, tr, D), lambda r, i: (layer, r, 0)),
                  pl.BlockSpec((tm, D), lambda r, i: (i, 0)), pl.BlockSpec((tm, D), lambda r, i: (i, 0))],
        out_specs=pl.BlockSpec((tr, tm), lambda r, i: (r, i)),
        out_shape=jax.ShapeDtypeStruct((R, T), F32),
        compiler_params=_cparams("parallel", "parallel"),
        name="scores",
    )(whi, wlo, hhi, hlo)


def _batcher_pairs(n):
    pairs = []
    p = 1
    while p < n:
        k = p
        while k >= 1:
            for j in range(k % p, n - k, 2 * k):
                for i in range(min(k, n - j - k)):
                    if (i + j) // (2 * p) == (i + j + k) // (2 * p):
                        pairs.append((i + j, i + j + k))
            k //= 2
        p *= 2
    return pairs


_SORT16 = _batcher_pairs(PEER_TOPK)


def _sort_desc(v):
    v = list(v)
    for i, j in _SORT16:
        v[i], v[j] = jnp.maximum(v[i], v[j]), jnp.minimum(v[i], v[j])
    return v


def _merge_top(a, b):
    n = len(a)
    v = [jnp.maximum(a[i], b[n - 1 - i]) for i in range(n)]
    s = n // 2
    while s >= 1:
        for i in range(n):
            if (i & s) == 0:
                v[i], v[i + s] = jnp.maximum(v[i], v[i + s]), jnp.minimum(v[i], v[i + s])
        s //= 2
    return v


def _top16_of_keys(s_ref, row0, lanes):
    k = PEER_TOPK
    groups = [_sort_desc([s_ref[pl.ds(row0 + PEER_HEADS * (k * g + i), PEER_HEADS), lanes] for i in range(k)])
              for g in range(PEER_KEYS // k)]
    while len(groups) > 1:
        groups = [_merge_top(groups[i], groups[i + 1]) for i in range(0, len(groups), 2)]
    return groups[0]


def _topk_kernel(s_ref, thr_ref, s1n_ref):
    k = PEER_TOPK
    half = PEER_HEADS * PEER_KEYS
    ninf = jnp.full((PEER_HEADS, 128), -jnp.inf, F32)

    def chunk(ch, carry):
        lanes = pl.ds(pl.multiple_of(ch * 128, 128), 128)
        a = _top16_of_keys(s_ref, 0, lanes)
        b = _top16_of_keys(s_ref, half, lanes)
        cand = [[a[i] + b[j] for j in range(k // (i + 1))] for i in range(k)]
        g0 = cand[0]
        g1 = _sort_desc(cand[1] + cand[2] + cand[5] + cand[8])
        g2 = _sort_desc(cand[3] + cand[4] + cand[6] + cand[7] + cand[9] + cand[10] + cand[11] + cand[12] + cand[13])
        g3 = cand[14] + cand[15] + [ninf] * (k - 2)
        top = _merge_top(_merge_top(g0, g1), _merge_top(g2, g3))
        zsum = jnp.zeros((PEER_HEADS, 128), F32)
        for t in top:
            zsum = zsum + jnp.exp2(t - top[0])
        c0 = top[0] + jnp.log2(zsum)
        thr_ref[:, lanes] = top[k - 1] - c0
        for key in range(PEER_KEYS):
            tile = s_ref[pl.ds(half + PEER_HEADS * key, PEER_HEADS), lanes] - c0
            s1n_ref[ch, pl.ds(key, PEER_HEADS, stride=PEER_KEYS), :] = tile
        return carry

    lax.fori_loop(0, s_ref.shape[1] // 128, chunk, 0)


def _topk(st, tm):
    R, T = st.shape
    return pl.pallas_call(
        _topk_kernel,
        grid=(T // tm,),
        in_specs=[pl.BlockSpec((R, tm), lambda i: (0, i))],
        out_specs=[pl.BlockSpec((PEER_HEADS, tm), lambda i: (0, i)),
                   pl.BlockSpec((tm // 128, R // 2, 128), lambda i: (i, 0, 0))],
        out_shape=[jax.ShapeDtypeStruct((PEER_HEADS, T), F32), jax.ShapeDtypeStruct((T // 128, R // 2, 128), F32)],
        compiler_params=_cparams("parallel"),
        name="topk",
    )(st)


def _peer_kernel(ht_ref, s0_ref, s1n_ref, thr_ref, u_ref, vt_ref, x_ref, g2_ref, o_ref,
                 pre_a, pre_b, c_a, c_b, acc_sc):
    s = pl.program_id(1)
    te, tm = c_a.shape
    nk = PEER_KEYS

    @pl.when(s == 0)
    def _():
        acc_sc[...] = jnp.zeros_like(acc_sc)
        pre_b[...] = jnp.zeros_like(pre_b)
        c_a[...] = jnp.zeros_like(c_a)

    d = ht_ref.shape[0]
    ni = te // nk
    jr = 16
    nblk = (tm // 128) * (nk // jr)
    nh = 2 if tm % 256 == 0 else 1
    th = tm // nh
    ka = mc = max(1, min(nblk // nh, d // 256))
    kq, mq = d // ka, d // mc

    def pre_piece(p, pre_w):
        half, q = divmod(p, ka)
        part = _dot(u_ref[:, q * kq:(q + 1) * kq], ht_ref[q * kq:(q + 1) * kq, half * th:(half + 1) * th])
        if q == 0:
            pre_w[:, half * th:(half + 1) * th] = part
        else:
            pre_w[:, half * th:(half + 1) * th] += part

    def value_piece(p, c_r):
        half, q = divmod(p, mc)
        acc_sc[q * mq:(q + 1) * mq, half * th:(half + 1) * th] += _dot(
            vt_ref[q * mq:(q + 1) * mq, :], c_r[:, half * th:(half + 1) * th])

    slots = {}
    for p in range(nh * ka):
        slots.setdefault(p * nblk // (nh * ka), []).append((pre_piece, p))
    for p in range(nh * mc):
        slots.setdefault((p * nblk // (nh * mc) + 1) % nblk, []).append((value_piece, p))

    def stages(pre_w, pre_r, c_w, c_r):
        for blk in range(nblk):
            for piece, p in slots.get(blk, ()):
                piece(p, pre_w if piece is pre_piece else c_r)
            ch, jq = divmod(blk, nk // jr)
            lanes = slice(ch * 128, (ch + 1) * 128)
            gates = [jnp.zeros((jr, 128), F32) for _ in range(ni)]
            for h in range(PEER_HEADS):
                s1 = s1n_ref[ch, h * nk + jq * jr:h * nk + (jq + 1) * jr, :]
                thr = thr_ref[h:h + 1, lanes]
                for ii in range(ni):
                    r0 = ii * PEER_HEADS + h
                    val = s1 + s0_ref[r0:r0 + 1, lanes]
                    gates[ii] = gates[ii] + jnp.where(val >= thr, jnp.exp2(val), 0.0)
            for ii in range(ni):
                rows = slice(ii * nk + jq * jr, ii * nk + (jq + 1) * jr)
                pre = pre_r[rows, lanes]
                act = 0.5 * pre * (1.0 + lax.erf(pre * (2.0 ** -0.5)))
                c_w[rows, lanes] = (gates[ii] * act).astype(BF16)

    @pl.when(s % 2 == 0)
    def _():
        stages(pre_a, pre_b, c_b, c_a)

    @pl.when(s % 2 == 1)
    def _():
        stages(pre_b, pre_a, c_a, c_b)

    @pl.when(s == pl.num_programs(1) - 1)
    def _():
        o_ref[...] = x_ref[...] + g2_ref[...] * acc_sc[...].T


def _peer(ht, st, thr, s1n, u, vt, x, modv, mod_row_of_tile, tm, layer):
    T, D = x.shape
    _, ne, _, te = vt.shape
    half = PEER_HEADS * PEER_KEYS
    tile = lambda s, lag: jnp.clip(s - lag, 0, ne - 1)
    return pl.pallas_call(
        _peer_kernel,
        grid=(T // tm, ne + 2),
        in_specs=[pl.BlockSpec((D, tm), lambda i, s: (0, i)),
                  pl.BlockSpec((te // PEER_KEYS * PEER_HEADS, tm), lambda i, s: (tile(s, 1), i)),
                  pl.BlockSpec((tm // 128, half, 128), lambda i, s: (i, 0, 0)),
                  pl.BlockSpec((PEER_HEADS, tm), lambda i, s: (0, i)),
                  pl.BlockSpec((None, te, D), lambda i, s: (layer, tile(s, 0), 0)),
                  pl.BlockSpec((None, None, D, te), lambda i, s: (layer, tile(s, 2), 0, 0)),
                  pl.BlockSpec((tm, D), lambda i, s: (i, 0)),
                  pl.BlockSpec((None, 1, D), lambda i, s: (mod_row_of_tile(i) + 5, 0, 0))],
        out_specs=pl.BlockSpec((tm, D), lambda i, s: (i, 0)),
        out_shape=jax.ShapeDtypeStruct((T, D), F32),
        scratch_shapes=[pltpu.VMEM((te, tm), F32), pltpu.VMEM((te, tm), F32), pltpu.VMEM((te, tm), BF16),
                        pltpu.VMEM((te, tm), BF16), pltpu.VMEM((D, tm), F32)],
        compiler_params=_cparams("parallel", "arbitrary"),
        name="peer_experts",
    )(ht, st, s1n, thr, u, vt, x, modv)


def _rope_tables(n, ident_rows):
    pos = jnp.arange(n, dtype=jnp.int32)
    row, col = pos // GRID_W, pos % GRID_W
    lane = jnp.arange(HEAD_DIM)

    def build(nfreq, width):
        inv = ROPE_THETA ** (-jnp.arange(nfreq, dtype=F32) / nfreq)
        ar = row.astype(F32)[:, None] * inv[None, :]
        ac = col.astype(F32)[:, None] * inv[None, :]
        cos = jnp.concatenate([jnp.cos(ar), jnp.cos(ar), jnp.cos(ac), jnp.cos(ac)], axis=-1)
        sin = jnp.concatenate([jnp.sin(ar), jnp.sin(ar), jnp.sin(ac), jnp.sin(ac)], axis=-1)
        pad = HEAD_DIM - width
        cos = jnp.pad(cos, ((0, 0), (0, pad)))
        sin = jnp.pad(sin, ((0, 0), (0, pad)))
        first = (lane % (2 * nfreq)) < nfreq
        tab = jnp.stack([cos, jnp.where(first, 0.0, sin), jnp.where(first, -sin, 0.0)])
        ident = jnp.stack([jnp.ones((ident_rows, HEAD_DIM), F32), jnp.zeros((ident_rows, HEAD_DIM), F32),
                           jnp.zeros((ident_rows, HEAD_DIM), F32)])
        return jnp.concatenate([ident, tab], axis=1)

    return build(HEAD_DIM // 4, HEAD_DIM), build(MLA_ROPE // 4, MLA_ROPE)


def kernel(x_prompt, x_sample, c, cache_gqa_k, cache_gqa_v, cache_mla_ckv, cache_mla_krope, state_hgrn, c_ctx,
           w_mod, b_mod, norm1_g, norm2_g, w_in, gqa_qn, gqa_kn, mla_qa_n, mla_kva_n, w_mla_qb, w_mla_kvb,
           mla_qn, mla_kn, hg_lb_logits, hg_on, w_out, peer_wq, peer_keys, peer_u, peer_v):
    B, S, D = x_prompt.shape
    DB, DS, _ = x_sample.shape
    L = w_mod.shape[0]
    P = cache_gqa_k.shape[2]
    TC, TL = B * S, DB * DS
    T = TC + TL
    tm = 512 if (TC % 512 == 0 and DS % 512 == 0) else 128
    tm_in = 1024 if (TC % 1024 == 0 and DS % 1024 == 0) else tm
    tm_small = min(256, tm)

    ngroups = 1 + DB

    def mod_row_of(t, layer):
        def f(i):
            group = jnp.where(i < TC // t, 0, 1 + (i - TC // t) // (DS // t))
            return (layer * ngroups + group) * 6
        return f

    def table_of(t):
        def f(i):
            return jnp.where(i < TC // t, 0, 1 + (i - TC // t) % (DS // t))
        return f

    cond8 = jnp.zeros((8, D), F32).at[0].set(c_ctx).at[1:1 + DB].set(c)
    modv = _modulation(cond8, w_mod, b_mod)[:, :ngroups].reshape(L * ngroups * 6, 1, D)

    lb_cum = jnp.cumsum(jax.nn.softmax(hg_lb_logits.astype(F32), axis=0), axis=0)
    lbs = jnp.maximum(lb_cum - lb_cum[:1], 0.0)
    lbtab = jnp.stack([jnp.log(lbs), jnp.log1p(-lbs), 1.0 - lbs] + [jnp.zeros_like(lbs)] * 5, axis=2)

    sizes = np.cumsum([0, 768, 256, 256, 512, 512, 512, 512, 512, 512, 512, 64])
    part = lambda a, k: a[..., sizes[k]:sizes[k + 1]]
    order = (0, 1, 2, 10, 8, 9, 3, 4, 5, 6, 7)
    cols = []
    for k in order:
        cols.append(part(w_in, k))
        if k == 10:
            cols.append(jnp.zeros((L, D, 256 - MLA_ROPE), w_in.dtype))
    w_in_b = jnp.concatenate(cols, axis=-1).astype(BF16)
    w_in_b = jnp.swapaxes(w_in_b.reshape(L, D, -1, IN_TILE), 1, 2)

    wqb = w_mla_qb.reshape(L, MLA_RANK, MLA_HEADS, MLA_QK)
    wqb = jnp.pad(wqb, ((0, 0), (0, 0), (0, 0), (0, MLA_PAD - MLA_QK))).reshape(L, MLA_RANK, MLA_HEADS * MLA_PAD).astype(BF16)
    wkvb = w_mla_kvb.reshape(L, MLA_RANK, MLA_HEADS, 2, HEAD_DIM)
    wkvb = jnp.moveaxis(wkvb, 3, 2).reshape(L, MLA_RANK, 2 * MLA_HEADS * HEAD_DIM).astype(BF16)
    rpad = lambda g: jnp.pad(g[:, HEAD_DIM:], ((0, 0), (0, HEAD_DIM - MLA_ROPE)))[:, None, :]
    mqn_n, mqn_r = mla_qn[:, None, :HEAD_DIM], rpad(mla_qn)
    mkn_n, mkn_r = mla_kn[:, None, :HEAD_DIM], rpad(mla_kn)
    w_out_b = w_out.astype(BF16)
    u_b = peer_u.astype(BF16)
    vt_b = jnp.swapaxes(peer_v.reshape(L, -1, PEER_EXPERT_TILE, D), 2, 3).astype(BF16)
    def key_major(w):
        return jnp.swapaxes(w.reshape(L, 2, PEER_HEADS, PEER_KEYS, D), 2, 3).reshape(L, 2 * PEER_HEADS * PEER_KEYS, D)

    ws_hi, ws_lo = map(key_major, _score_weights(peer_keys, peer_wq))

    tabg, tabm = _rope_tables(DS, tm_small)

    cck = jnp.moveaxis(cache_mla_ckv, 1, 0).reshape(L, DB * P, MLA_RANK)
    ckr = jnp.pad(jnp.moveaxis(cache_mla_krope, 1, 0).reshape(L, DB * P, MLA_ROPE), ((0, 0), (0, 0), (0, HEAD_DIM - MLA_ROPE)))
    cmk, cmv = _cache_expand(cck, ckr, wkvb, mkn_n, mkn_r)
    cgk = jnp.moveaxis(cache_gqa_k, 1, 0).reshape(L, DB * P, -1).astype(BF16)
    cgv = jnp.transpose(cache_gqa_v.reshape(DB, L, P, -1), (1, 3, 0, 2)).reshape(L, -1, DB * P).astype(BF16)
    s0_ctx = jnp.zeros((B, 2, HG_HEADS, HEAD_DIM, HEAD_DIM), F32)

    x = jnp.concatenate([x_prompt.reshape(TC, D), x_sample.reshape(TL, D)], axis=0)
    kvw = GQA_KV_HEADS * HEAD_DIM
    caches = tuple(jnp.zeros((B, L, S, w), F32) for w in (kvw, kvw, MLA_RANK, MLA_ROPE))
    states = jnp.zeros((B, L, 2, HG_HEADS, HEAD_DIM, HEAD_DIM), F32)
    ga = jnp.zeros((T, GQA_HEADS * HEAD_DIM), BF16)
    ma = jnp.zeros((T, MLA_HEADS * HEAD_DIM), BF16)
    o2 = jnp.zeros((2, T, HG_HEADS * HEAD_DIM), F32)
    tq_c = min(256, S)
    tq_l, tk_l = min(2048, DS), min(512, P, DS)
    for l in range(L):
        z = _in_proj(x, norm1_g[l][None], modv, w_in_b, mod_row_of(tm_in, l), tm_in, l)
        gq, gkr, gvt, mq, mk, mvt, *caches = _prep(
            z, tabg, tabm, table_of(tm_small), tm_small, gqa_qn[l][None], gqa_kn[l][None], mla_qa_n[l][None],
            mla_kva_n[l][None], wqb[l], wkvb[l], mqn_n[l], mqn_r[l], mkn_n[l], mkn_r[l], caches, l)

        gqa = dict(heads=GQA_HEADS, group=GQA_HEADS // GQA_KV_HEADS, dq=HEAD_DIM, dv=HEAD_DIM, hb=GQA_HEADS // GQA_KV_HEADS)
        mla = dict(heads=MLA_HEADS, group=1, dq=MLA_PAD, dv=HEAD_DIM, hb=MLA_HEADS // 2)
        ctx_a = dict(batch=B, nq=S, nk=S, row0=0, tq=tq_c, tk=tq_c)
        lat_a = dict(batch=DB, nq=DS, nk=DS, row0=TC, tq=tq_l, tk=tk_l)
        ga = _attention(gq, gkr, gvt, **dict(gqa, hb=GQA_HEADS), **ctx_a, prev=ga)
        ga = _attention(gq, gkr, gvt, **gqa, **lat_a, prev=ga, cache=(cgk[l], cgv[l]))
        ma = _attention(mq, mk, mvt, **dict(mla, hb=MLA_HEADS), **ctx_a, prev=ma)
        ma = _attention(mq, mk, mvt, **mla, **lat_a, prev=ma, cache=(cmk[l], cmv[l]))

        o2, states = _hgrn(z, lbtab[l], s0_ctx, batch=B, n=S, row0=0, prev=o2, states=states, layer=l)
        o2, _ = _hgrn(z, lbtab[l], state_hgrn[:, l].astype(F32), batch=DB, n=DS, row0=TC, prev=o2)

        x, hhi, hlo, ht = _mixer_out(x, ga, o2, z, ma, w_out_b, hg_on[l][None], norm2_g[l][None], modv,
                                 mod_row_of(tm_small, l), tm_small, l)
        st = _scores(ws_hi, ws_lo, hhi, hlo, tm, l)
        thr, s1n = _topk(st, tm)
        x = _peer(ht, st, thr, s1n, u_b, vt_b, x, modv, mod_row_of(tm, l), tm, l)

    new_k, new_v, new_ckv, new_kr = caches
    return (x[:TC].reshape(B, S, D), x[TC:].reshape(DB, DS, D),
            new_k.reshape(B, L, S, GQA_KV_HEADS, HEAD_DIM), new_v.reshape(B, L, S, GQA_KV_HEADS, HEAD_DIM),
            new_ckv, new_kr, states)
```
